```python
import math
import jax
import jax.numpy as jnp
from jax import lax
import numpy as np

D_MODEL = 1024
BATCH = 2
SEQ = 8192
DEPTH = 1
DEC_BATCH = 32
DEC_SEQ = 1
PAST_LEN = 8192
PAGE_SIZE = 128

DA_HEADS = 4
DA_HD = 64
DA_VD = 2 * DA_HD
NSA_HEADS = 8
NSA_KV = 2
NSA_HPG = NSA_HEADS // NSA_KV
NSA_HD = 64
CMP_STRIDE = 16
CMP_LEN = 2 * CMP_STRIDE
CMP_HID = 64
SEL_BLOCK = 64
N_SEL = 16
WINDOW = 512
MIX_A = DA_HEADS * DA_VD
MIX_B = NSA_HEADS * NSA_HD
D_FF = 4 * D_MODEL
N_BUCKETS = 32
MAX_DIST = 128
N_HEADS_TOTAL = DA_HEADS + NSA_HEADS
Q_BLOCK = 128
ALPHA = (2 * DEPTH) ** 0.25
BETA = (8 * DEPTH) ** -0.25
LN_EPS = 1e-5
RMS_EPS = 1e-5
NEG = -1e30
SEL_BIG = 1e9
SPLIT_SIZES = (DA_HEADS * 2 * DA_HD, DA_HEADS * 2 * DA_HD, DA_HEADS * DA_VD,
               NSA_HEADS * NSA_HD, 2 * NSA_KV * NSA_HD, 2 * NSA_KV * NSA_HD, 2 * NSA_KV * NSA_HD,
               3 * NSA_HEADS, D_MODEL, D_MODEL)
D_IN = sum(SPLIT_SIZES)
SPLIT_POINTS = tuple(int(v) for v in np.cumsum(SPLIT_SIZES)[:-1])

kernel_name = 'diffattn_nsa_gated_hybrid_step'


def _layer_norm(x, g, b):
    xf = x.astype(jnp.float32)
    mu = jnp.mean(xf, -1, keepdims=True)
    var = jnp.mean(jnp.square(xf - mu), -1, keepdims=True)
    return ((xf - mu) * lax.rsqrt(var + LN_EPS) * g + b).astype(x.dtype)


def _t5_bucket(dist):
    n = jnp.maximum(dist, 0)
    max_exact = N_BUCKETS // 2
    nf = jnp.maximum(n, 1).astype(jnp.float32)
    large = max_exact + (jnp.log(nf / max_exact) / math.log(MAX_DIST / max_exact)
                         * (N_BUCKETS - max_exact)).astype(jnp.int32)
    return jnp.where(n < max_exact, n, jnp.minimum(large, N_BUCKETS - 1))


def _masked_softmax(logits, mask):
    l = jnp.where(mask, logits.astype(jnp.float32), NEG)
    m = jnp.max(l, -1, keepdims=True)
    e = jnp.where(mask, jnp.exp(l - m), 0.0)
    return e / jnp.maximum(jnp.sum(e, -1, keepdims=True), 1e-30)


def _over_query_blocks(fn, q_args, q_pos):
    sq = q_pos.shape[0]
    qb = min(Q_BLOCK, sq)
    nb = sq // qb
    if nb == 1:
        return fn(q_args, q_pos)
    def split(a):
        return jnp.moveaxis(a.reshape(a.shape[0], nb, qb, *a.shape[2:]), 1, 0)
    out = lax.map(lambda xs: fn(xs[0], xs[1]), (tuple(split(a) for a in q_args), q_pos.reshape(nb, qb)))
    out = jnp.moveaxis(out, 0, 1)
    return out.reshape(out.shape[0], sq, *out.shape[3:])


def _diff_attn_block(q, t, kv, k_pos, lam, tbl_a):
    B, L = kv.shape[:2]
    k = kv[..., :2 * DA_HD].reshape(B, L, DA_HEADS, 2, DA_HD)
    v = kv[..., 2 * DA_HD:]
    s = jnp.einsum('bqhcd,bkhcd->bhcqk', q, k).astype(jnp.float32) * (DA_HD ** -0.5)
    bias = tbl_a[_t5_bucket(t[:, None] - k_pos[None, :])]
    s = s + jnp.transpose(bias, (2, 0, 1))[None, :, None]
    p = _masked_softmax(s, k_pos[None, :] <= t[:, None])
    a = p[:, :, 0] - lam * p[:, :, 1]
    return jnp.einsum('bhqk,bkhe->bqhe', a.astype(v.dtype), v)


def _compress(rows, cmp_pe, cmp_w1, cmp_w2):
    B, L = rows.shape[:2]
    n_ch = L // CMP_STRIDE
    ch = rows[:, :n_ch * CMP_STRIDE].reshape(B, n_ch, CMP_STRIDE, NSA_KV, 2, NSA_HD)
    blocks = jnp.concatenate([ch[:, :-1], ch[:, 1:]], axis=2)
    blocks = blocks + jnp.transpose(cmp_pe, (1, 0, 2))[:, None]
    hid = jax.nn.gelu(jnp.einsum('bnlgcd,cldh->bngch', blocks, cmp_w1))
    out = jnp.einsum('bngch,chd->bngcd', hid, cmp_w2)
    return out[..., 0, :], out[..., 1, :]


def _nsa_block(q, gate, t, kc, vc, kv_sel, kv_win, w_pos0, tbl_b):
    B, qb = q.shape[:2]
    scale = NSA_HD ** -0.5
    qg = q.reshape(B, qb, NSA_KV, NSA_HPG, NSA_HD)
    tq = t[:, None]
    n_cmp = kc.shape[1]
    c_start = jnp.arange(n_cmp, dtype=jnp.int32) * CMP_STRIDE
    s_c = jnp.einsum('bqghd,bngd->bghqn', qg, kc).astype(jnp.float32) * scale
    p_c = _masked_softmax(s_c, c_start[None, :] + (CMP_LEN - 1) <= tq)
    o_c = jnp.einsum('bghqn,bngd->bqghd', p_c.astype(vc.dtype), vc)
    n_slc = kv_sel.shape[1] // SEL_BLOCK
    n_top = min(N_SEL, n_slc)
    s_start = jnp.arange(n_slc, dtype=jnp.int32) * SEL_BLOCK
    ovl = ((c_start[None, :] < s_start[:, None] + SEL_BLOCK)
           & (s_start[:, None] < c_start[None, :] + CMP_LEN)).astype(jnp.float32)
    imp = jnp.einsum('bghqn,jn->bgqj', p_c, ovl)
    blk = jnp.arange(n_slc, dtype=jnp.int32)[None, :]
    cur = tq // SEL_BLOCK
    forced = (blk == 0) | (blk == cur) | (blk == cur - 1)
    score = jnp.where(forced, SEL_BIG, jnp.where(s_start[None, :] <= tq, imp, -SEL_BIG))
    _, idx = lax.top_k(score, n_top)
    kvb = kv_sel.reshape(B, n_slc, SEL_BLOCK, NSA_KV, 2 * NSA_HD).transpose(0, 3, 1, 2, 4)
    sel = jnp.take_along_axis(kvb, idx.reshape(B, NSA_KV, qb * n_top)[:, :, :, None, None], axis=2)
    sel = sel.reshape(B, NSA_KV, qb, n_top * SEL_BLOCK, 2 * NSA_HD)
    k_pos = (idx[..., None] * SEL_BLOCK + jnp.arange(SEL_BLOCK, dtype=jnp.int32)).reshape(B, NSA_KV, qb, n_top * SEL_BLOCK)
    tbl3 = tbl_b.reshape(N_BUCKETS, NSA_KV, NSA_HPG).transpose(1, 0, 2)
    bias_s = tbl3[jnp.arange(NSA_KV)[:, None, None], _t5_bucket(tq - k_pos)]
    s_s = jnp.einsum('bqghd,bgqkd->bghqk', qg, sel[..., :NSA_HD]).astype(jnp.float32) * scale + jnp.moveaxis(bias_s, -1, 2)
    p_s = _masked_softmax(s_s, (k_pos <= tq)[:, :, None])
    o_s = jnp.einsum('bghqk,bgqkd->bqghd', p_s.astype(sel.dtype), sel[..., NSA_HD:])
    lw = WINDOW + qb
    win = lax.dynamic_slice_in_dim(kv_win, t[0] - WINDOW - w_pos0, lw, axis=1)
    wpos = t[0] - WINDOW + jnp.arange(lw, dtype=jnp.int32)
    d = tq - wpos[None, :]
    m_w = (wpos[None, :] >= 0) & (d >= 0) & (d <= WINDOW)
    bias_w = tbl_b[_t5_bucket(d)].reshape(qb, lw, NSA_KV, NSA_HPG).transpose(2, 3, 0, 1)
    s_w = jnp.einsum('bqghd,bkgd->bghqk', qg, win[..., :NSA_HD]).astype(jnp.float32) * scale + bias_w
    p_w = _masked_softmax(s_w, m_w)
    o_w = jnp.einsum('bghqk,bkgd->bqghd', p_w.astype(win.dtype), win[..., NSA_HD:])
    g = jax.nn.sigmoid(gate).reshape(B, qb, NSA_KV, NSA_HPG, 3, 1)
    out = g[..., 0, :] * o_c + g[..., 1, :] * o_s + g[..., 2, :] * o_w
    return out.reshape(B, qb, MIX_B)


def _split_heads(parts):
    qa, ka, va, qn, kvc, kvs, kvw, gb, ma, mb = parts
    B, S = qa.shape[:2]
    return dict(
        q_a=qa.reshape(B, S, DA_HEADS, 2, DA_HD),
        da_rows=jnp.concatenate([ka.reshape(B, S, DA_HEADS, 2 * DA_HD), va.reshape(B, S, DA_HEADS, DA_VD)], -1),
        q_b=qn.reshape(B, S, NSA_HEADS, NSA_HD),
        kv_c=kvc.reshape(B, S, NSA_KV, 2 * NSA_HD),
        kv_s=kvs.reshape(B, S, NSA_KV, 2 * NSA_HD),
        kv_w=kvw.reshape(B, S, NSA_KV, 2 * NSA_HD),
        gate_b=gb.reshape(B, S, NSA_HEADS, 3),
        merge_a=ma, merge_b=mb)


def _prompt_mixers(hd, lam, rel_bias, lp):
    S = hd['q_a'].shape[1]
    pos = jnp.arange(S, dtype=jnp.int32)
    da_rows = hd['da_rows']
    tbl_a = rel_bias[:, :DA_HEADS]
    tbl_b = rel_bias[:, DA_HEADS:]
    o_a = _over_query_blocks(lambda qs, t: _diff_attn_block(qs[0], t, da_rows, pos, lam, tbl_a), (hd['q_a'],), pos)
    kc, vc = _compress(hd['kv_c'], lp['cmp_pe'], lp['cmp_w1'], lp['cmp_w2'])
    kv_s = hd['kv_s']
    kv_w = jnp.pad(hd['kv_w'], ((0, 0), (WINDOW, 0), (0, 0), (0, 0)))
    o_b = _over_query_blocks(lambda qs, t: _nsa_block(qs[0], qs[1], t, kc, vc, kv_s, kv_w, -WINDOW, tbl_b),
                             (hd['q_b'], hd['gate_b']), pos)
    state = (da_rows, jnp.concatenate([hd['kv_c'], kv_s], -1), hd['kv_w'][:, S - min(WINDOW, S):])
    return o_a, o_b, state


def _sample_mixers(hd, lam, rel_bias, lp, cache_da, cache_nsa, cache_swa, page_table):
    B, S = hd['q_a'].shape[:2]
    past = page_table.shape[1] * PAGE_SIZE
    q_pos = past + jnp.arange(S, dtype=jnp.int32)
    k_pos = jnp.arange(past + S, dtype=jnp.int32)
    tbl_a = rel_bias[:, :DA_HEADS]
    tbl_b = rel_bias[:, DA_HEADS:]
    da_all = jnp.concatenate([cache_da[page_table].reshape(B, past, DA_HEADS, 4 * DA_HD), hd['da_rows']], 1)
    o_a = _over_query_blocks(lambda qs, t: _diff_attn_block(qs[0], t, da_all, k_pos, lam, tbl_a), (hd['q_a'],), q_pos)
    nsa_past = cache_nsa[page_table].reshape(B, past, NSA_KV, 4 * NSA_HD)
    kv_c_all = jnp.concatenate([nsa_past[..., :2 * NSA_HD], hd['kv_c']], 1)
    kv_s_all = jnp.concatenate([nsa_past[..., 2 * NSA_HD:], hd['kv_s']], 1)
    n_slc = -(-(past + S) // SEL_BLOCK)
    kv_s_all = jnp.pad(kv_s_all, ((0, 0), (0, n_slc * SEL_BLOCK - past - S), (0, 0), (0, 0)))
    kc, vc = _compress(kv_c_all, lp['cmp_pe'], lp['cmp_w1'], lp['cmp_w2'])
    win_all = jnp.concatenate([cache_swa, hd['kv_w']], 1)
    kv_w = jnp.pad(win_all, ((0, 0), (WINDOW - cache_swa.shape[1], 0), (0, 0), (0, 0)))
    o_b = _over_query_blocks(lambda qs, t: _nsa_block(qs[0], qs[1], t, kc, vc, kv_s_all, kv_w, past - WINDOW, tbl_b),
                             (hd['q_b'], hd['gate_b']), q_pos)
    state = (hd['da_rows'], jnp.concatenate([hd['kv_c'], hd['kv_s']], -1), win_all[:, S:])
    return o_a, o_b, state


def _diff_lambda(lam_p, lam_init):
    l = lam_p.astype(jnp.float32)
    return jnp.exp(jnp.sum(l[0] * l[1])) - jnp.exp(jnp.sum(l[2] * l[3])) + lam_init


def _layer(x, c, mixers, lam_init, lp):
    B, S, _ = x.shape
    ada = (jax.nn.silu(c) @ lp['w_ada'] + lp['b_ada']).reshape(B, 6, 1, D_MODEL)
    h = x * (1 + ada[:, 1]) + ada[:, 0]
    hd = _split_heads(jnp.split(h @ lp['w_in'], SPLIT_POINTS, axis=-1))
    o_a, o_b, state = mixers(hd)
    of = o_a.astype(jnp.float32)
    o_a = (of * lax.rsqrt(jnp.mean(of * of, -1, keepdims=True) + RMS_EPS) * lp['da_subln'] * (1.0 - lam_init)).astype(x.dtype)
    y = (jax.nn.sigmoid(hd['merge_a']) * (o_a.reshape(B, S, MIX_A) @ lp['w_br_a'])
         + jax.nn.sigmoid(hd['merge_b']) * (o_b @ lp['w_br_b']))
    x = _layer_norm(ALPHA * x + ada[:, 2] * (y @ lp['w_out']), lp['ln1_g'], lp['ln1_b'])
    h2 = x * (1 + ada[:, 4]) + ada[:, 3]
    f = jnp.square(jax.nn.relu(h2 @ lp['w_up'])) @ lp['w_down']
    x = _layer_norm(ALPHA * x + ada[:, 5] * f, lp['ln2_g'], lp['ln2_b'])
    return x, state


def setup_inputs(seed: int = 0) -> dict:
    key = jax.random.key(seed)
    ks = jax.random.split(key, 32)
    n_pages = PAST_LEN // PAGE_SIZE
    n_pool = (DEC_BATCH * n_pages * 5) // 4
    swa_buf = min(WINDOW, PAST_LEN)
    def nrm(k, shape, scale):
        return jax.random.normal(k, shape, jnp.float32) * scale
    page_table = jax.random.permutation(ks[7], n_pool)[:DEC_BATCH * n_pages].reshape(DEC_BATCH, n_pages).astype(jnp.int32)
    return {
        'x_prompt': nrm(ks[0], (BATCH, SEQ, D_MODEL), 1.0),
        'x_sample': nrm(ks[1], (DEC_BATCH, DEC_SEQ, D_MODEL), 1.0),
        'cache_da_kv': nrm(ks[4], (DEPTH, n_pool, PAGE_SIZE, DA_HEADS, 4 * DA_HD), 1.0),
        'cache_nsa_kv': nrm(ks[5], (DEPTH, n_pool, PAGE_SIZE, NSA_KV, 4 * NSA_HD), 1.0),
        'cache_swa_kv': nrm(ks[6], (DEPTH, DEC_BATCH, swa_buf, NSA_KV, 2 * NSA_HD), 1.0),
        'page_table': page_table,
        'c_prompt': nrm(ks[2], (BATCH, D_MODEL), 1.0),
        'c_sample': nrm(ks[3], (DEC_BATCH, D_MODEL), 1.0),
        'rel_bias': nrm(ks[8], (N_BUCKETS, N_HEADS_TOTAL), 0.5),
        'w_ada': nrm(ks[9], (DEPTH, D_MODEL, 6 * D_MODEL), D_MODEL ** -0.5),
        'b_ada': nrm(ks[10], (DEPTH, 6 * D_MODEL), 0.01),
        'w_in': nrm(ks[11], (DEPTH, D_MODEL, D_IN), D_MODEL ** -0.5),
        'da_lambda': nrm(ks[12], (DEPTH, 4, DA_HD), 0.1),
        'da_subln': 1.0 + nrm(ks[13], (DEPTH, DA_VD), 0.01),
        'cmp_pe': nrm(ks[14], (DEPTH, 2, CMP_LEN, NSA_HD), 0.1),
        'cmp_w1': nrm(ks[15], (DEPTH, 2, CMP_LEN, NSA_HD, CMP_HID), (CMP_LEN * NSA_HD) ** -0.5),
        'cmp_w2': nrm(ks[16], (DEPTH, 2, CMP_HID, NSA_HD), CMP_HID ** -0.5),
        'w_br_a': nrm(ks[17], (DEPTH, MIX_A, D_MODEL), MIX_A ** -0.5),
        'w_br_b': nrm(ks[18], (DEPTH, MIX_B, D_MODEL), MIX_B ** -0.5),
        'w_out': nrm(ks[19], (DEPTH, D_MODEL, D_MODEL), BETA * D_MODEL ** -0.5),
        'ln1_g': 1.0 + nrm(ks[20], (DEPTH, D_MODEL), 0.01),
        'ln1_b': nrm(ks[21], (DEPTH, D_MODEL), 0.01),
        'w_up': nrm(ks[22], (DEPTH, D_MODEL, D_FF), D_MODEL ** -0.5),
        'w_down': nrm(ks[23], (DEPTH, D_FF, D_MODEL), BETA * D_FF ** -0.5),
        'ln2_g': 1.0 + nrm(ks[24], (DEPTH, D_MODEL), 0.01),
        'ln2_b': nrm(ks[25], (DEPTH, D_MODEL), 0.01),
    }


def reference(x_prompt, x_sample, cache_da_kv, cache_nsa_kv, cache_swa_kv, page_table, c_prompt, c_sample,
              rel_bias, w_ada, b_ada, w_in, da_lambda, da_subln, cmp_pe, cmp_w1, cmp_w2, w_br_a, w_br_b,
              w_out, ln1_g, ln1_b, w_up, w_down, ln2_g, ln2_b):
    lp_all = dict(w_ada=w_ada, b_ada=b_ada, w_in=w_in, da_lambda=da_lambda, da_subln=da_subln,
                  cmp_pe=cmp_pe, cmp_w1=cmp_w1, cmp_w2=cmp_w2, w_br_a=w_br_a, w_br_b=w_br_b, w_out=w_out,
                  ln1_g=ln1_g, ln1_b=ln1_b, w_up=w_up, w_down=w_down, ln2_g=ln2_g, ln2_b=ln2_b)
    y_p, y_s = x_prompt, x_sample
    st_p, st_s = [], []
    for i in range(DEPTH):
        lp = {name: arr[i] for name, arr in lp_all.items()}
        lam_init = 0.8 - 0.6 * math.exp(-0.3 * i)
        lam = _diff_lambda(lp['da_lambda'], lam_init)
        y_p, s_p = _layer(y_p, c_prompt, lambda hd: _prompt_mixers(hd, lam, rel_bias, lp), lam_init, lp)
        y_s, s_s = _layer(y_s, c_sample,
                          lambda hd: _sample_mixers(hd, lam, rel_bias, lp, cache_da_kv[i], cache_nsa_kv[i],
                                                    cache_swa_kv[i], page_table), lam_init, lp)
        st_p.append(s_p)
        st_s.append(s_s)
    new_da_p = jnp.stack([s[0] for s in st_p])
    new_nsa_p = jnp.stack([s[1] for s in st_p])
    new_swa_p = jnp.stack([s[2] for s in st_p])
    new_da_s = jnp.stack([s[0] for s in st_s])
    new_nsa_s = jnp.stack([s[1] for s in st_s])
    new_swa_s = jnp.stack([s[2] for s in st_s])
    return (y_p, y_s, new_da_p, new_nsa_p, new_swa_p, new_da_s, new_nsa_s, new_swa_s)
```

```python
import functools
import math

import numpy as np
import jax
import jax.numpy as jnp
from jax import lax
from jax.experimental import pallas as pl
from jax.experimental.pallas import tpu as pltpu

F32 = jnp.float32
BF16 = jnp.bfloat16

D_MODEL = 1024
PAGE = 128
DA_HEADS = 4
DA_HD = 64
DA_VD = 128
NSA_HEADS = 8
NSA_KV = 2
NSA_HPG = 4
NSA_HD = 64
CMP_STRIDE = 16
CMP_LEN = 32
CMP_HID = 64
SEL_BLOCK = 64
N_SEL = 16
WINDOW = 512
D_FF = 4096
N_BUCKETS = 32
MAX_DIST = 128
DEPTH = 1
ALPHA = (2 * DEPTH) ** 0.25
LN_EPS = 1e-5
RMS_EPS = 1e-5
NEG = -1e30
SEL_BIG = 1e9
LAM_INIT = 0.8 - 0.6 * math.exp(-0.3 * 0)
SPLIT_SIZES = (512, 512, 512, 512, 256, 256, 256, 24, 1024, 1024)

LANE = 128
VMEM_LIMIT = 56 * 1024 * 1024
TQ = 128
TK = 512
TDA = 512
SEL_LANES = 128
REMOVED = -3e38
PP = 8


def _cparams(sem):
    return pltpu.CompilerParams(dimension_semantics=sem, vmem_limit_bytes=VMEM_LIMIT)


def _mm(a, b):
    return jnp.dot(a.astype(BF16), b.astype(BF16), preferred_element_type=F32)


def _mm_nt(a, b):
    return lax.dot_general(a.astype(BF16), b.astype(BF16), (((1,), (1,)), ((), ())),
                           preferred_element_type=F32)


def _mm3(x, w):
    hi = x.astype(BF16)
    r = x - hi.astype(F32)
    mid = r.astype(BF16)
    lo = (r - mid.astype(F32)).astype(BF16)
    return (jnp.dot(hi, w, preferred_element_type=F32) + jnp.dot(mid, w, preferred_element_type=F32)
            + jnp.dot(lo, w, preferred_element_type=F32))


def _masked_softmax(s, valid):
    l = jnp.where(valid, s, NEG)
    m = jnp.max(l, axis=-1, keepdims=True)
    e = jnp.where(valid, jnp.exp(l - m), 0.0)
    return e / jnp.maximum(jnp.sum(e, axis=-1, keepdims=True), 1e-30)


def _online_update(s, valid, v, m_ref, l_ref, a_ref):
    if valid is not None:
        s = jnp.where(valid, s, NEG)
    m_old = m_ref[...]
    m_new = jnp.maximum(m_old, jnp.max(s, axis=-1, keepdims=True))
    p = jnp.exp(s - m_new)
    if valid is not None:
        p = jnp.where(valid, p, 0.0)
    alpha = jnp.exp(m_old - m_new)
    l_ref[...] = alpha * l_ref[...] + jnp.sum(p, axis=-1, keepdims=True)
    a_ref[...] = alpha * a_ref[...] + _mm(p, v)
    m_ref[...] = m_new


def _t5_bucket(dist):
    n = jnp.maximum(dist, 0)
    max_exact = N_BUCKETS // 2
    nf = jnp.maximum(n, 1).astype(F32)
    large = max_exact + (jnp.log(nf / max_exact) / math.log(MAX_DIST / max_exact)
                         * (N_BUCKETS - max_exact)).astype(jnp.int32)
    return jnp.where(n < max_exact, n, jnp.minimum(large, N_BUCKETS - 1))


def _dist_bias(tbl, n):
    d = jnp.arange(n, dtype=jnp.int32)
    g = tbl[_t5_bucket(d)] - tbl[N_BUCKETS - 1][None, :]
    return jnp.transpose(g)


def _toeplitz(gd, offset, rows, cols):
    i = jnp.arange(rows, dtype=jnp.int32)[:, None]
    j = jnp.arange(cols, dtype=jnp.int32)[None, :]
    d = jnp.clip(offset + i - j, 0, gd.shape[1] - 1)
    return gd[:, d]


def _diff_lambda(lam_ref):
    l = lam_ref[...]
    a = jnp.sum(l[0:1, :] * l[1:2, :], axis=-1, keepdims=True)
    b = jnp.sum(l[2:3, :] * l[3:4, :], axis=-1, keepdims=True)
    return jnp.exp(a) - jnp.exp(b) + LAM_INIT


def _ada_kernel(c_ref, w_ref, b_ref, o_ref):
    c = c_ref[...]
    o_ref[...] = _mm(c * jax.nn.sigmoid(c), w_ref[...]) + b_ref[...]


def _ada(c, w_ada, b_ada):
    m = c.shape[0]
    n = w_ada.shape[1]
    tn = 512
    return pl.pallas_call(
        _ada_kernel,
        grid=(n // tn,),
        in_specs=[pl.BlockSpec((m, D_MODEL), lambda j: (0, 0)),
                  pl.BlockSpec((D_MODEL, tn), lambda j: (0, j)),
                  pl.BlockSpec((1, tn), lambda j: (0, j))],
        out_specs=pl.BlockSpec((m, tn), lambda j: (0, j)),
        out_shape=jax.ShapeDtypeStruct((m, n), F32),
        compiler_params=_cparams(("arbitrary",)),
        name="ada",
    )(c, w_ada, b_ada.reshape(1, n))


PROJ_GROUPS = (("qa", 512, False, True), ("da", 1024, True, True), ("qb", 512, False, True),
               ("nsa", 512, True, False), ("kvw", 256, True, False), ("gate", 128, True, False),
               ("ma", 1024, True, False), ("mb", 1024, True, False),
               ("ksel", 512, False, True), ("kwin", 512, False, True))
PROJ_OUTS = tuple((n, w, dt) for n, w, f, b in PROJ_GROUPS for dt, on in ((F32, f), (BF16, b)) if on)
PROJ_W = sum(w for _, w, _, _ in PROJ_GROUPS)


def _perm_w_in(w_in):
    parts = jnp.split(w_in, np.cumsum(SPLIT_SIZES)[:-1].tolist(), axis=1)
    qa, ka, va, qn, kvc, kvs, kvw, gb, ma, mb = parts
    da = jnp.concatenate([jnp.concatenate([ka[:, h * 128:(h + 1) * 128], va[:, h * 128:(h + 1) * 128]], 1)
                          for h in range(DA_HEADS)], 1)
    nsa = jnp.concatenate([jnp.concatenate([kvc[:, g * 128:(g + 1) * 128], kvs[:, g * 128:(g + 1) * 128]], 1)
                           for g in range(NSA_KV)], 1)
    gate = jnp.pad(gb, ((0, 0), (0, LANE - gb.shape[1])))

    def dup(kv):
        out = []
        for g in range(NSA_KV):
            k = kv[:, g * 128:g * 128 + 64]
            v = kv[:, g * 128 + 64:g * 128 + 128]
            out += [k, k, v, v]
        return jnp.concatenate(out, 1)

    cols = dict(qa=qa, da=da, qb=qn, nsa=nsa, kvw=kvw, gate=gate, ma=ma, mb=mb, ksel=dup(kvs), kwin=dup(kvw))
    return jnp.concatenate([cols[n] for n, _, _, _ in PROJ_GROUPS], 1).astype(BF16)


def _proj_kernel(x_ref, ada_ref, w_ref, *o_refs):
    h = (x_ref[...] * (1.0 + ada_ref[1]) + ada_ref[0]).astype(BF16)
    off = 0
    k = 0
    for _, width, f32_on, bf_on in PROJ_GROUPS:
        acc = jnp.dot(h, w_ref[:, off:off + width], preferred_element_type=F32)
        if f32_on:
            o_refs[k][...] = acc
            k += 1
        if bf_on:
            o_refs[k][...] = acc.astype(BF16)
            k += 1
        off += width


def _ada_spec(ada, tm, tiles_per_group):
    r = ada.shape[2]
    return pl.BlockSpec((6, None, r, D_MODEL), lambda i, *_: (0, i // tiles_per_group, 0, 0))


def _proj(x2d, ada, w_perm, tm, tiles_per_group):
    m = x2d.shape[0]
    outs = pl.pallas_call(
        _proj_kernel,
        grid=(m // tm,),
        in_specs=[pl.BlockSpec((tm, D_MODEL), lambda i: (i, 0)),
                  _ada_spec(ada, tm, tiles_per_group),
                  pl.BlockSpec((D_MODEL, PROJ_W), lambda i: (0, 0))],
        out_specs=[pl.BlockSpec((tm, w), lambda i: (i, 0)) for _, w, _ in PROJ_OUTS],
        out_shape=[jax.ShapeDtypeStruct((m, w), dt) for _, w, dt in PROJ_OUTS],
        compiler_params=_cparams(("arbitrary",)),
        name="proj",
    )(x2d, ada, w_perm)
    return {(n, dt): o for (n, _, dt), o in zip(PROJ_OUTS, outs)}


def _da_kernel(qi_tab, kj_tab, q_ref, kv_ref, bd_ref, bs_ref, lam_ref, o_ref,
               m1, l1, a1, m2, l2, a2):
    s_id = pl.program_id(2)
    qi = qi_tab[s_id]
    kj = kj_tab[s_id]
    scale = DA_HD ** -0.5

    @pl.when(kj == 0)
    def _():
        for m_ref, l_ref, a_ref in ((m1, l1, a1), (m2, l2, a2)):
            m_ref[...] = jnp.full(m_ref.shape, NEG, F32)
            l_ref[...] = jnp.zeros(l_ref.shape, F32)
            a_ref[...] = jnp.zeros(a_ref.shape, F32)

    def step(bias, causal):
        q = q_ref[...]
        lane = lax.broadcasted_iota(jnp.int32, q.shape, 1)
        zero = jnp.zeros_like(q)
        kk = kv_ref[:, 0:128]
        v = kv_ref[:, 128:256]
        valid = None
        if causal:
            row = lax.broadcasted_iota(jnp.int32, (TDA, TDA), 0)
            col = lax.broadcasted_iota(jnp.int32, (TDA, TDA), 1)
            valid = col <= row
        for qm, m_ref, l_ref, a_ref in ((jnp.where(lane < DA_HD, q, zero), m1, l1, a1),
                                        (jnp.where(lane >= DA_HD, q, zero), m2, l2, a2)):
            s = _mm_nt(qm, kk) * scale
            if bias is not None:
                s = s + bias
            _online_update(s, valid, v, m_ref, l_ref, a_ref)

    @pl.when(kj == qi)
    def _():
        step(bd_ref[...], True)
        lam = _diff_lambda(lam_ref)
        o_ref[...] = a1[...] / l1[...] - lam * (a2[...] / l2[...])

    @pl.when(kj == qi - 1)
    def _():
        step(bs_ref[...], False)

    @pl.when(kj < qi - 1)
    def _():
        step(None, False)


def _da_prompt(qa, da_bf, gd_a, da_lambda):
    b, s, _ = qa.shape
    nq = s // TDA
    steps = [(qi, kj) for qi in range(nq) for kj in range(qi + 1)]
    qi_tab = jnp.asarray(np.array([p[0] for p in steps], np.int32))
    kj_tab = jnp.asarray(np.array([p[1] for p in steps], np.int32))
    bd = _toeplitz(gd_a, 0, TDA, TDA)
    bs = _toeplitz(gd_a, TDA, TDA, TDA)
    grid_spec = pltpu.PrefetchScalarGridSpec(
        num_scalar_prefetch=2,
        grid=(b, DA_HEADS, len(steps)),
        in_specs=[pl.BlockSpec((None, TDA, 128), lambda bi, h, t, qt, kt: (bi, qt[t], h)),
                  pl.BlockSpec((None, TDA, 256), lambda bi, h, t, qt, kt: (bi, kt[t], h)),
                  pl.BlockSpec((None, TDA, TDA), lambda bi, h, t, qt, kt: (h, 0, 0)),
                  pl.BlockSpec((None, TDA, TDA), lambda bi, h, t, qt, kt: (h, 0, 0)),
                  pl.BlockSpec((4, DA_HD), lambda bi, h, t, qt, kt: (0, 0))],
        out_specs=pl.BlockSpec((None, TDA, 128), lambda bi, h, t, qt, kt: (bi, qt[t], h)),
        scratch_shapes=[pltpu.VMEM((TDA, 1), F32), pltpu.VMEM((TDA, 1), F32), pltpu.VMEM((TDA, 128), F32),
                        pltpu.VMEM((TDA, 1), F32), pltpu.VMEM((TDA, 1), F32), pltpu.VMEM((TDA, 128), F32)])
    return pl.pallas_call(
        _da_kernel,
        grid_spec=grid_spec,
        out_shape=jax.ShapeDtypeStruct((b, s, DA_HEADS * DA_VD), F32),
        compiler_params=_cparams(("arbitrary", "arbitrary", "arbitrary")),
        name="da_prompt",
    )(qi_tab, kj_tab, qa, da_bf, bd, bs, da_lambda)


def _cmp_weights(cmp_w1, cmp_w2):
    w1 = jnp.zeros((CMP_STRIDE, 2, NSA_HD, 2, 2, CMP_HID), F32)
    for half in range(2):
        for c in range(2):
            blk = cmp_w1[c, half * CMP_STRIDE:(half + 1) * CMP_STRIDE]
            w1 = w1.at[:, c, :, half, c, :].set(blk)
    w1 = w1.reshape(CMP_STRIDE * 2 * NSA_HD, 2 * 2 * CMP_HID).astype(BF16)
    w2 = jnp.zeros((2, CMP_HID, 2, 2, NSA_HD), F32)
    for c in range(2):
        for rep in range(2):
            w2 = w2.at[c, :, c, rep, :].set(cmp_w2[c])
    w2 = w2.reshape(2 * CMP_HID, 2 * 2 * NSA_HD).astype(BF16)
    return w1, w2


def _cmp1_kernel(x_ref, w_ref, o_ref):
    x = x_ref[...]
    for g in range(NSA_KV):
        xg = jnp.concatenate([x[:, l * 512 + g * 256:l * 512 + g * 256 + 128] for l in range(CMP_STRIDE)],
                             axis=1)
        o_ref[:, g * 256:(g + 1) * 256] = _mm(xg, w_ref[...])


def _cmp_stage1(x2d, w1):
    r = x2d.shape[0]
    tm = math.gcd(r, 256)
    return pl.pallas_call(
        _cmp1_kernel,
        grid=(r // tm,),
        in_specs=[pl.BlockSpec((tm, CMP_STRIDE * 512), lambda i: (i, 0)),
                  pl.BlockSpec(w1.shape, lambda i: (0, 0))],
        out_specs=pl.BlockSpec((tm, 512), lambda i: (i, 0)),
        out_shape=jax.ShapeDtypeStruct((r, 512), F32),
        compiler_params=_cparams(("arbitrary",)),
        name="cmp_stage1",
    )(x2d, w1)


def _cmp_stage2(ab, pe_ref, w1c_ref, w2_ref):
    n = ab.shape[0]
    cst = jnp.concatenate([_mm(pe_ref[c], w1c_ref[c])[0:1, :] for c in range(2)], axis=1)
    outs = []
    for g in range(NSA_KV):
        a = ab[:, g * 256:g * 256 + 128]
        bn = pltpu.roll(ab[:, g * 256 + 128:g * 256 + 256], n - 1, 0)
        hid = jax.nn.gelu(a + bn + cst)
        outs.append(_mm(hid, w2_ref[...]))
    return jnp.concatenate(outs, axis=1)


def _cmp2_kernel(ab_ref, pe_ref, w1c_ref, w2_ref, o_ref):
    o_ref[...] = _cmp_stage2(ab_ref[...], pe_ref, w1c_ref, w2_ref).astype(BF16)


def _cmp_stage2_prompt(ab, pe8, w1c, w2):
    b, n, _ = ab.shape
    return pl.pallas_call(
        _cmp2_kernel,
        grid=(b,),
        in_specs=[pl.BlockSpec((None, n, 512), lambda i: (i, 0, 0)),
                  pl.BlockSpec(pe8.shape, lambda i: (0, 0, 0)),
                  pl.BlockSpec(w1c.shape, lambda i: (0, 0, 0)),
                  pl.BlockSpec(w2.shape, lambda i: (0, 0))],
        out_specs=pl.BlockSpec((None, n, 512), lambda i: (i, 0, 0)),
        out_shape=jax.ShapeDtypeStruct((b, n, 512), BF16),
        compiler_params=_cparams(("arbitrary",)),
        name="cmp_stage2",
    )(ab, pe8, w1c, w2)


def _ovl_t(n_chunks):
    n = np.arange(n_chunks)[:, None]
    j = np.arange(SEL_LANES)[None, :]
    ovl = (n * CMP_STRIDE < j * SEL_BLOCK + SEL_BLOCK) & (j * SEL_BLOCK < n * CMP_STRIDE + CMP_LEN)
    ovl &= n < n_chunks - 1
    return jnp.asarray(ovl.astype(np.float32)).astype(BF16)


def _topk_select(score, n_pick):
    lane = lax.broadcasted_iota(jnp.int32, score.shape, 1).astype(F32)
    sel = jnp.zeros(score.shape, F32)
    picks = jnp.zeros(score.shape, F32)
    sc = score
    for it in range(n_pick):
        m = jnp.max(sc, axis=-1, keepdims=True)
        first = jnp.min(jnp.where(sc == m, lane, float(SEL_LANES)), axis=-1, keepdims=True)
        hit = lane == first
        sel = jnp.where(hit, 1.0, sel)
        picks = jnp.where(lane == float(it), first, picks)
        sc = jnp.where(hit, REMOVED, sc)
    return sel, picks


def _stack_heads(q):
    lane = lax.broadcasted_iota(jnp.int32, (q.shape[0], LANE), 1)
    zero = jnp.zeros((q.shape[0], LANE), q.dtype)
    parts = []
    for hp in range(NSA_HPG):
        blk = q[:, (hp // 2) * LANE:(hp // 2 + 1) * LANE]
        keep = (lane < NSA_HD) if hp % 2 == 0 else (lane >= NSA_HD)
        parts.append(jnp.where(keep, blk, zero))
    return jnp.concatenate(parts, axis=0)


def _unstack_heads(o, tq):
    lane = lax.broadcasted_iota(jnp.int32, (tq, LANE), 1)
    pairs = [jnp.where(lane < NSA_HD, o[(2 * m) * tq:(2 * m + 1) * tq], o[(2 * m + 1) * tq:(2 * m + 2) * tq])
             for m in range(2)]
    return jnp.concatenate(pairs, axis=1)


def _nsa_cmp_kernel(q_ref, kcvc_ref, ovl_ref, oc_ref, sel_ref, *, n_cmp, n_slc):
    qi = pl.program_id(2)
    n_chunks = kcvc_ref.shape[0]
    qs = _stack_heads(q_ref[...])
    kc2 = kcvc_ref[:, 0:128]
    vc2 = kcvc_ref[:, 128:256]
    s = _mm_nt(qs, kc2) * (NSA_HD ** -0.5)
    rows = NSA_HPG * TQ
    t = qi * TQ + (lax.broadcasted_iota(jnp.int32, (rows, n_chunks), 0) & (TQ - 1))
    n = lax.broadcasted_iota(jnp.int32, (rows, n_chunks), 1)
    p = _masked_softmax(s, (n * CMP_STRIDE + (CMP_LEN - 1) <= t) & (n < n_cmp))
    oc_ref[...] = _unstack_heads(_mm(p, vc2), TQ)
    psum = p[0:TQ] + p[TQ:2 * TQ] + p[2 * TQ:3 * TQ] + p[3 * TQ:4 * TQ]
    imp = _mm3(psum, ovl_ref[...])
    tq = qi * TQ + lax.broadcasted_iota(jnp.int32, (TQ, SEL_LANES), 0)
    blk = lax.broadcasted_iota(jnp.int32, (TQ, SEL_LANES), 1)
    cur = lax.shift_right_logical(tq, 6)
    forced = (blk == 0) | (blk == cur) | (blk == cur - 1)
    score = jnp.where(forced, SEL_BIG, jnp.where(blk * SEL_BLOCK <= tq, imp, -SEL_BIG))
    score = jnp.where(blk < n_slc, score, REMOVED)
    sel, _ = _topk_select(score, min(N_SEL, n_slc))
    sel_ref[...] = sel.astype(BF16)


def _nsa_cmp_prompt(qb, kcvc, ovl):
    b, s, _ = qb.shape
    n_chunks = kcvc.shape[1]
    kern = functools.partial(_nsa_cmp_kernel, n_cmp=n_chunks - 1, n_slc=s // SEL_BLOCK)
    return pl.pallas_call(
        kern,
        grid=(b, NSA_KV, s // TQ),
        in_specs=[pl.BlockSpec((None, TQ, 256), lambda bi, g, qi: (bi, qi, g)),
                  pl.BlockSpec((None, n_chunks, 256), lambda bi, g, qi: (bi, 0, g)),
                  pl.BlockSpec(ovl.shape, lambda bi, g, qi: (0, 0))],
        out_specs=[pl.BlockSpec((None, TQ, 256), lambda bi, g, qi: (bi, qi, g)),
                   pl.BlockSpec((None, None, TQ, SEL_LANES), lambda bi, g, qi: (bi, g, qi, 0))],
        out_shape=[jax.ShapeDtypeStruct((b, s, 512), F32),
                   jax.ShapeDtypeStruct((b, NSA_KV, s, SEL_LANES), BF16)],
        compiler_params=_cparams(("arbitrary", "arbitrary", "arbitrary")),
        name="nsa_cmp",
    )(qb, kcvc, ovl)


def _nsa_sw_kernel(qi_tab, kj_tab, var_tab, kw_tab, q_ref, ks_ref, kw_ref, sel_ref, e_ref, nb_ref,
                   os_ref, ow_ref, ms, ls, as_, mw, lw, aw):
    s_id = pl.program_id(2)
    qi = qi_tab[s_id]
    kj = kj_tab[s_id]
    kjmax = lax.shift_right_logical(qi, 2)
    r = qi & 3
    scale = NSA_HD ** -0.5
    rows = NSA_HPG * TQ

    @pl.when(kj == 0)
    def _():
        for m_ref, l_ref, a_ref in ((ms, ls, as_), (mw, lw, aw)):
            m_ref[...] = jnp.full(m_ref.shape, NEG, F32)
            l_ref[...] = jnp.zeros(l_ref.shape, F32)
            a_ref[...] = jnp.zeros(a_ref.shape, F32)

    def step(use_bias, near, window):
        qs = _stack_heads(q_ref[...])
        mexp = jnp.dot(sel_ref[...], e_ref[...], preferred_element_type=F32)
        chosen = jnp.concatenate([mexp] * NSA_HPG, axis=0) > 0.5
        i = lax.broadcasted_iota(jnp.int32, (rows, TK), 0) & (TQ - 1)
        j = lax.broadcasted_iota(jnp.int32, (rows, TK), 1)
        bias = nb_ref[...].reshape(rows, TK) if use_bias else None
        off = r * TQ if near else TK + r * TQ
        d = off + i - j
        s = _mm_nt(qs, ks_ref[:, 0:128]) * scale
        if bias is not None:
            s = s + bias
        valid = (chosen & (d >= 0)) if near else chosen
        _online_update(s, valid, ks_ref[:, 128:256], ms, ls, as_)
        if window:
            sw = _mm_nt(qs, kw_ref[:, 0:128]) * scale
            if bias is not None:
                sw = sw + bias
            _online_update(sw, (d >= 0) & (d <= WINDOW), kw_ref[:, 128:256], mw, lw, aw)

    @pl.when(kj == kjmax)
    def _():
        step(True, True, True)
        os_ref[...] = _unstack_heads(as_[...] / ls[...], TQ)
        ow_ref[...] = _unstack_heads(aw[...] / lw[...], TQ)

    @pl.when((kj == kjmax - 1) & (r == 0))
    def _():
        step(True, False, True)

    @pl.when((kj == kjmax - 1) & (r != 0))
    def _():
        step(False, False, True)

    @pl.when(kj < kjmax - 1)
    def _():
        step(False, False, False)


def _nsa_sw_prompt(qb, ksel, kwin, sel, gd_b):
    b, s, _ = qb.shape
    nq = s // TQ
    nk = s // TK
    qi_l, kj_l, var_l, kw_l = [], [], [], []
    for qi in range(nq):
        kjmax = qi // 4
        r = qi % 4
        for kj in range(kjmax + 1):
            qi_l.append(qi)
            kj_l.append(kj)
            var_l.append(r if (r != 0 or kj == kjmax) else 4)
            kw_l.append(max(kj, max(kjmax - 1, 0)))
    tabs = [jnp.asarray(np.array(a, np.int32)) for a in (qi_l, kj_l, var_l, kw_l)]
    nb = jnp.stack([_toeplitz(gd_b, o, TQ, TK) for o in (0, 128, 256, 384, 512)], 0)
    nb = nb.reshape(5, NSA_KV, NSA_HPG, TQ, TK).transpose(1, 0, 2, 3, 4)
    e = np.zeros((nk, SEL_LANES, TK), np.float32)
    for kj in range(nk):
        for k in range(TK):
            e[kj, kj * (TK // SEL_BLOCK) + k // SEL_BLOCK, k] = 1.0
    e = jnp.asarray(e).astype(BF16)
    im = lambda f: (lambda bi, g, t, qt, kt, vt, wt: f(bi, g, t, qt, kt, vt, wt))
    grid_spec = pltpu.PrefetchScalarGridSpec(
        num_scalar_prefetch=4,
        grid=(b, NSA_KV, len(qi_l)),
        in_specs=[pl.BlockSpec((None, TQ, 256), im(lambda bi, g, t, qt, kt, vt, wt: (bi, qt[t], g))),
                  pl.BlockSpec((None, TK, 256), im(lambda bi, g, t, qt, kt, vt, wt: (bi, kt[t], g))),
                  pl.BlockSpec((None, TK, 256), im(lambda bi, g, t, qt, kt, vt, wt: (bi, wt[t], g))),
                  pl.BlockSpec((None, None, TQ, SEL_LANES), im(lambda bi, g, t, qt, kt, vt, wt: (bi, g, qt[t], 0))),
                  pl.BlockSpec((None, SEL_LANES, TK), im(lambda bi, g, t, qt, kt, vt, wt: (kt[t], 0, 0))),
                  pl.BlockSpec((None, None, NSA_HPG, TQ, TK),
                               im(lambda bi, g, t, qt, kt, vt, wt: (g, vt[t], 0, 0, 0)))],
        out_specs=[pl.BlockSpec((None, TQ, 256), im(lambda bi, g, t, qt, kt, vt, wt: (bi, qt[t], g))),
                   pl.BlockSpec((None, TQ, 256), im(lambda bi, g, t, qt, kt, vt, wt: (bi, qt[t], g)))],
        scratch_shapes=[pltpu.VMEM((NSA_HPG * TQ, 1), F32), pltpu.VMEM((NSA_HPG * TQ, 1), F32),
                        pltpu.VMEM((NSA_HPG * TQ, 128), F32),
                        pltpu.VMEM((NSA_HPG * TQ, 1), F32), pltpu.VMEM((NSA_HPG * TQ, 1), F32),
                        pltpu.VMEM((NSA_HPG * TQ, 128), F32)])
    return pl.pallas_call(
        _nsa_sw_kernel,
        grid_spec=grid_spec,
        out_shape=[jax.ShapeDtypeStruct((b, s, 512), F32), jax.ShapeDtypeStruct((b, s, 512), F32)],
        compiler_params=_cparams(("arbitrary", "arbitrary", "arbitrary")),
        name="nsa_sel_win",
    )(*tabs, qb, ksel, kwin, sel, e, nb)


def _layer_norm(x, g, b):
    mu = jnp.mean(x, axis=-1, keepdims=True)
    xc = x - mu
    var = jnp.mean(xc * xc, axis=-1, keepdims=True)
    return xc * lax.rsqrt(var + LN_EPS) * g + b


def _gate_expand():
    e = np.zeros((3, LANE, NSA_HEADS * NSA_HD), np.float32)
    for h in range(NSA_HEADS):
        for j in range(3):
            e[j, h * 3 + j, h * NSA_HD:(h + 1) * NSA_HD] = 1.0
    return jnp.asarray(e).astype(BF16)


def _tail1_kernel(oa_ref, oc_ref, os_ref, ow_ref, gate_ref, ma_ref, mb_ref, x_ref, ada_ref,
                  wa_ref, wb_ref, wo_ref, sub_ref, eg_ref, g1_ref, b1_ref, o_ref):
    oa = oa_ref[...]
    parts = []
    for h in range(DA_HEADS):
        of = oa[:, h * DA_VD:(h + 1) * DA_VD]
        rr = lax.rsqrt(jnp.mean(of * of, axis=-1, keepdims=True) + RMS_EPS)
        parts.append(of * rr * sub_ref[...] * (1.0 - LAM_INIT))
    oan = jnp.concatenate(parts, axis=1)
    sg = jax.nn.sigmoid(gate_ref[...])
    ob = (_mm3(sg, eg_ref[0]) * oc_ref[...] + _mm3(sg, eg_ref[1]) * os_ref[...]
          + _mm3(sg, eg_ref[2]) * ow_ref[...])
    y = (jax.nn.sigmoid(ma_ref[...]) * _mm(oan, wa_ref[...])
         + jax.nn.sigmoid(mb_ref[...]) * _mm(ob, wb_ref[...]))
    z = ALPHA * x_ref[...] + ada_ref[2] * _mm(y, wo_ref[...])
    o_ref[...] = _layer_norm(z, g1_ref[...], b1_ref[...])


def _tail1(oa, oc, os_, ow, gate, ma, mb, x2d, ada, wa, wb, wo, sub, eg, g1, b1, tm, tiles_per_group):
    m = x2d.shape[0]
    row = lambda w: pl.BlockSpec((tm, w), lambda i: (i, 0))
    full = lambda a: pl.BlockSpec(a.shape, lambda i: (0,) * a.ndim)
    return pl.pallas_call(
        _tail1_kernel,
        grid=(m // tm,),
        in_specs=[row(512), row(512), row(512), row(512), row(128), row(1024), row(1024), row(1024),
                  _ada_spec(ada, tm, tiles_per_group),
                  full(wa), full(wb), full(wo), full(sub), full(eg), full(g1), full(b1)],
        out_specs=row(1024),
        out_shape=jax.ShapeDtypeStruct((m, D_MODEL), F32),
        compiler_params=_cparams(("arbitrary",)),
        name="tail_merge",
    )(oa, oc, os_, ow, gate, ma, mb, x2d, ada, wa, wb, wo, sub, eg, g1, b1)


def _tail2_kernel(x_ref, ada_ref, wu_ref, wd_ref, g2_ref, b2_ref, o_ref, h_scr, acc):
    f = pl.program_id(1)

    @pl.when(f == 0)
    def _():
        h_scr[...] = (x_ref[...] * (1.0 + ada_ref[4]) + ada_ref[3]).astype(BF16)
        acc[...] = jnp.zeros(acc.shape, F32)

    u = jnp.maximum(jnp.dot(h_scr[...], wu_ref[...], preferred_element_type=F32), 0.0)
    acc[...] += _mm(u * u, wd_ref[...])

    @pl.when(f == pl.num_programs(1) - 1)
    def _():
        z = ALPHA * x_ref[...] + ada_ref[5] * acc[...]
        o_ref[...] = _layer_norm(z, g2_ref[...], b2_ref[...])


def _tail2(x1, ada, wu, wd, g2, b2, tm, tiles_per_group):
    m = x1.shape[0]
    tf = 1024
    return pl.pallas_call(
        _tail2_kernel,
        grid=(m // tm, D_FF // tf),
        in_specs=[pl.BlockSpec((tm, D_MODEL), lambda i, f: (i, 0)),
                  _ada_spec(ada, tm, tiles_per_group),
                  pl.BlockSpec((D_MODEL, tf), lambda i, f: (0, f)),
                  pl.BlockSpec((tf, D_MODEL), lambda i, f: (f, 0)),
                  pl.BlockSpec((1, D_MODEL), lambda i, f: (0, 0)),
                  pl.BlockSpec((1, D_MODEL), lambda i, f: (0, 0))],
        out_specs=pl.BlockSpec((tm, D_MODEL), lambda i, f: (i, 0)),
        out_shape=jax.ShapeDtypeStruct((m, D_MODEL), F32),
        scratch_shapes=[pltpu.VMEM((tm, D_MODEL), BF16), pltpu.VMEM((tm, D_MODEL), F32)],
        compiler_params=_cparams(("arbitrary", "arbitrary")),
        name="tail_mlp",
    )(x1, ada, wu, wd, g2, b2)


def _da_decode_kernel(pt_ref, *refs):
    pages = refs[:PP]
    q_ref, new_ref, bl_ref, b0_ref, lam_ref, o_ref, m_ref, l_ref, a_ref = refs[PP:]
    j = pl.program_id(1)
    last = j == pl.num_programs(1) - 1
    scale = DA_HD ** -0.5

    @pl.when(j == 0)
    def _():
        m_ref[...] = jnp.full(m_ref.shape, NEG, F32)
        l_ref[...] = jnp.zeros(l_ref.shape, F32)
        a_ref[...] = jnp.zeros(a_ref.shape, F32)

    q = q_ref[...]
    scores = []
    vals = []
    for k in range(PP):
        pg = pages[k][...].astype(BF16)
        sc = _mm_nt(q, pg) * scale
        if k == PP - 1:
            sc = sc + jnp.where(last, bl_ref[...], 0.0)
        scores.append(sc)
        vals.append(jnp.concatenate([pg[:, h * 256 + 128:(h + 1) * 256] for h in range(DA_HEADS)], axis=1))
    s = jnp.concatenate(scores, axis=1)
    m_old = m_ref[...]
    m_new = jnp.maximum(m_old, jnp.max(s, axis=-1, keepdims=True))
    p = jnp.exp(s - m_new)
    alpha = jnp.exp(m_old - m_new)
    l_ref[...] = alpha * l_ref[...] + jnp.sum(p, axis=-1, keepdims=True)
    acc = alpha * a_ref[...]
    for k in range(PP):
        acc = acc + _mm(p[:, k * PAGE:(k + 1) * PAGE], vals[k])
    a_ref[...] = acc
    m_ref[...] = m_new

    @pl.when(last)
    def _():
        new = new_ref[...]
        s_new = jnp.sum(q.astype(F32) * new, axis=-1, keepdims=True) * scale + b0_ref[:, 0:1]
        m_o = m_ref[...]
        m_n = jnp.maximum(m_o, s_new)
        p_new = jnp.exp(s_new - m_n)
        al = jnp.exp(m_o - m_n)
        v_new = jnp.concatenate([new[:, h * 256 + 128:(h + 1) * 256] for h in range(DA_HEADS)], axis=1)
        raw = (al * a_ref[...] + p_new * v_new) / (al * l_ref[...] + p_new)
        row = lax.broadcasted_iota(jnp.int32, raw.shape, 0)
        lane = lax.broadcasted_iota(jnp.int32, raw.shape, 1)
        own = lax.shift_right_logical(lane, 7) == lax.shift_right_logical(row, 1)
        o1 = jnp.sum(jnp.where(own & ((row & 1) == 0), raw, 0.0), axis=0, keepdims=True)
        o2 = jnp.sum(jnp.where(own & ((row & 1) == 1), raw, 0.0), axis=0, keepdims=True)
        o_ref[...] = o1 - _diff_lambda(lam_ref) * o2


def _da_decode(page_table, cache_da, q8, new_row, bl, b0, da_lambda):
    b, n_pages = page_table.shape
    page_spec = lambda k: pl.BlockSpec((None, PAGE, 1024), lambda bi, j, pt: (pt[bi, j * PP + k], 0, 0))
    grid_spec = pltpu.PrefetchScalarGridSpec(
        num_scalar_prefetch=1,
        grid=(b, n_pages // PP),
        in_specs=[page_spec(k) for k in range(PP)] + [
            pl.BlockSpec((None, 8, 1024), lambda bi, j, pt: (bi, 0, 0)),
            pl.BlockSpec((None, 1, 1024), lambda bi, j, pt: (bi, 0, 0)),
            pl.BlockSpec((8, PAGE), lambda bi, j, pt: (0, 0)),
            pl.BlockSpec((8, LANE), lambda bi, j, pt: (0, 0)),
            pl.BlockSpec((4, DA_HD), lambda bi, j, pt: (0, 0))],
        out_specs=pl.BlockSpec((None, 1, 512), lambda bi, j, pt: (bi, 0, 0)),
        scratch_shapes=[pltpu.VMEM((8, 1), F32), pltpu.VMEM((8, 1), F32), pltpu.VMEM((8, 512), F32)])
    return pl.pallas_call(
        _da_decode_kernel,
        grid_spec=grid_spec,
        out_shape=jax.ShapeDtypeStruct((b, 1, 512), F32),
        compiler_params=_cparams(("arbitrary", "arbitrary")),
        name="da_decode",
    )(page_table, *([cache_da] * PP), q8, new_row, bl, b0, da_lambda)


def _nsa_decode1_kernel(pt_ref, ab_hbm, q_ref, swa_ref, new_ref, bw_ref, pe_ref, w1c_ref, w2_ref, ovl_ref,
                        oc_ref, ow_ref, idx_ref, abuf, sem, *, n_pages):
    b = pl.program_id(0)
    copies = [pltpu.make_async_copy(ab_hbm.at[pt_ref[b, p]], abuf.at[p], sem) for p in range(n_pages)]
    for c in copies:
        c.start()
    for c in copies:
        c.wait()
    n_chunks = n_pages * (PAGE // CMP_STRIDE)
    kcvc = _cmp_stage2(abuf[...].reshape(n_chunks, 512), pe_ref, w1c_ref, w2_ref)
    scale = NSA_HD ** -0.5
    swa = swa_ref[...]
    new = new_ref[...]
    for g in range(NSA_KV):
        q = q_ref[g]
        s = _mm_nt(q, kcvc[:, g * 256:g * 256 + 128]) * scale
        n = lax.broadcasted_iota(jnp.int32, s.shape, 1)
        p = _masked_softmax(s, n < n_chunks - 1)
        oc_ref[g] = _mm(p, kcvc[:, g * 256 + 128:g * 256 + 256])
        psum = jnp.sum(p[0:NSA_HPG], axis=0, keepdims=True)
        imp = _mm3(jnp.broadcast_to(psum, (8, n_chunks)), ovl_ref[...])
        blk = lax.broadcasted_iota(jnp.int32, imp.shape, 1)
        n_blk = n_chunks * CMP_STRIDE // SEL_BLOCK
        forced = (blk == 0) | (blk == n_blk - 1)
        score = jnp.where(blk < n_blk, jnp.where(forced, SEL_BIG, imp), REMOVED)
        _, picks = _topk_select(score, N_SEL - 1)
        idx_ref[g] = picks.astype(jnp.int32)
        kv = swa[:, g * 128:(g + 1) * 128]
        sw = _mm_nt(q, kv) * scale + bw_ref[g][:, 0:WINDOW]
        qf = q.astype(F32)
        s_new = (jnp.sum(qf * new[:, g * 128:(g + 1) * 128], axis=-1, keepdims=True) * scale
                 + bw_ref[g][:, WINDOW:WINDOW + 1])
        m = jnp.maximum(jnp.max(sw, axis=-1, keepdims=True), s_new)
        e = jnp.exp(sw - m)
        e_new = jnp.exp(s_new - m)
        den = jnp.sum(e, axis=-1, keepdims=True) + e_new
        ow_ref[g] = (_mm(e, kv) + e_new * new[:, g * 128:(g + 1) * 128]) / den


def _nsa_decode1(page_table, ab_pool, qc, cache_swa, kvw_new, bw, pe8, w1c, w2, ovl):
    b, n_pages = page_table.shape
    full = lambda a: pl.BlockSpec(a.shape, lambda bi, pt: (0,) * a.ndim)
    out4 = lambda: pl.BlockSpec((None, NSA_KV, 8, LANE), lambda bi, pt: (bi, 0, 0, 0))
    grid_spec = pltpu.PrefetchScalarGridSpec(
        num_scalar_prefetch=1,
        grid=(b,),
        in_specs=[pl.BlockSpec(memory_space=pl.ANY),
                  pl.BlockSpec((None, NSA_KV, 8, LANE), lambda bi, pt: (bi, 0, 0, 0)),
                  pl.BlockSpec((None, WINDOW, 256), lambda bi, pt: (bi, 0, 0)),
                  pl.BlockSpec((None, 1, 256), lambda bi, pt: (bi, 0, 0)),
                  full(bw), full(pe8), full(w1c), full(w2), full(ovl)],
        out_specs=[out4(), out4(), out4()],
        scratch_shapes=[pltpu.VMEM((n_pages, PAGE // CMP_STRIDE, 512), F32), pltpu.SemaphoreType.DMA(())])
    return pl.pallas_call(
        functools.partial(_nsa_decode1_kernel, n_pages=n_pages),
        grid_spec=grid_spec,
        out_shape=[jax.ShapeDtypeStruct((b, NSA_KV, 8, LANE), F32),
                   jax.ShapeDtypeStruct((b, NSA_KV, 8, LANE), F32),
                   jax.ShapeDtypeStruct((b, NSA_KV, 8, LANE), jnp.int32)],
        compiler_params=_cparams(("arbitrary",)),
        name="nsa_decode_cmp_win",
    )(page_table, ab_pool, qc, cache_swa, kvw_new, bw, pe8, w1c, w2, ovl)


def _nsa_decode2_kernel(pt_ref, idx_ref, blk_ref, q_ref, new_ref, bs_ref, o_ref, m_ref, l_ref, a_ref, *, n_blk):
    b = pl.program_id(0)
    g = pl.program_id(1)
    i = pl.program_id(2)
    scale = NSA_HD ** -0.5

    @pl.when(i == 0)
    def _():
        m_ref[...] = jnp.full(m_ref.shape, NEG, F32)
        l_ref[...] = jnp.zeros(l_ref.shape, F32)
        a_ref[...] = jnp.zeros(a_ref.shape, F32)

    q = q_ref[...]
    kv = blk_ref[...]
    near = idx_ref[b, g, i] == n_blk - 1
    s = _mm_nt(q, kv) * scale + jnp.where(near, bs_ref[:, 0:SEL_BLOCK], 0.0)
    _online_update(s, None, kv, m_ref, l_ref, a_ref)

    @pl.when(i == pl.num_programs(2) - 1)
    def _():
        new = new_ref[...]
        s_new = (jnp.sum(q.astype(F32) * new, axis=-1, keepdims=True) * scale
                 + bs_ref[:, SEL_BLOCK:SEL_BLOCK + 1])
        m_o = m_ref[...]
        m_n = jnp.maximum(m_o, s_new)
        p_new = jnp.exp(s_new - m_n)
        al = jnp.exp(m_o - m_n)
        o_ref[...] = (al * a_ref[...] + p_new * new) / (al * l_ref[...] + p_new)


def _nsa_decode2(page_table, idx, cache_nsa_half, qs, nsa_new, bs):
    b = page_table.shape[0]
    n_pick = idx.shape[2]

    def blk_map(bi, g, i, pt, ix):
        blk = ix[bi, g, i]
        return (pt[bi, lax.shift_right_logical(blk, 1)] * 2 + (blk & 1), 0, g)

    grid_spec = pltpu.PrefetchScalarGridSpec(
        num_scalar_prefetch=2,
        grid=(b, NSA_KV, n_pick),
        in_specs=[pl.BlockSpec((None, SEL_BLOCK, 256), blk_map),
                  pl.BlockSpec((None, None, 8, 256), lambda bi, g, i, pt, ix: (bi, g, 0, 0)),
                  pl.BlockSpec((None, 1, 256), lambda bi, g, i, pt, ix: (bi, 0, g)),
                  pl.BlockSpec((None, 8, LANE), lambda bi, g, i, pt, ix: (g, 0, 0))],
        out_specs=pl.BlockSpec((None, None, 8, 256), lambda bi, g, i, pt, ix: (bi, g, 0, 0)),
        scratch_shapes=[pltpu.VMEM((8, 1), F32), pltpu.VMEM((8, 1), F32), pltpu.VMEM((8, 256), F32)])
    return pl.pallas_call(
        functools.partial(_nsa_decode2_kernel, n_blk=page_table.shape[1] * PAGE // SEL_BLOCK),
        grid_spec=grid_spec,
        out_shape=jax.ShapeDtypeStruct((b, NSA_KV, 8, 256), F32),
        compiler_params=_cparams(("arbitrary", "arbitrary", "arbitrary")),
        name="nsa_decode_sel",
    )(page_table, idx, cache_nsa_half, qs, nsa_new, bs)


def _prompt_mixers(pr, b, s, gd_a, gd_b, da_lambda, w1, w2, pe8, w1c):
    sh = lambda a: a.reshape(b, s, a.shape[-1])
    o_a = _da_prompt(sh(pr["qa", BF16]), sh(pr["da", BF16]), gd_a, da_lambda)
    n_chunks = s // CMP_STRIDE
    ab = _cmp_stage1(pr["nsa", F32].reshape(b * n_chunks, CMP_STRIDE * 512), w1)
    kcvc = _cmp_stage2_prompt(ab.reshape(b, n_chunks, 512), pe8, w1c, w2)
    o_c, sel = _nsa_cmp_prompt(sh(pr["qb", BF16]), kcvc, _ovl_t(n_chunks))
    o_s, o_w = _nsa_sw_prompt(sh(pr["qb", BF16]), sh(pr["ksel", BF16]), sh(pr["kwin", BF16]), sel, gd_b)
    flat = lambda a: a.reshape(b * s, a.shape[-1])
    return flat(o_a), flat(o_c), flat(o_s), flat(o_w)


def _sample_mixers(pr, page_table, cache_da, cache_nsa, cache_swa, gd_a, gd_b, da_lambda, w1, w2, pe8, w1c):
    b, n_pages = page_table.shape
    past = n_pages * PAGE
    qa = pr["qa", BF16].reshape(b, DA_HEADS, 2, DA_HD)
    q8 = jnp.zeros((b, DA_HEADS, 2, DA_HEADS, 256), BF16)
    for h in range(DA_HEADS):
        for c in range(2):
            q8 = q8.at[:, h, c, h, c * DA_HD:(c + 1) * DA_HD].set(qa[:, h, c])
    q8 = q8.reshape(b, 8, 1024)
    gda8 = jnp.repeat(gd_a, 2, axis=0)
    bl = gda8[:, jnp.clip(PAGE - jnp.arange(PAGE), 0, gd_a.shape[1] - 1)]
    b0 = jnp.broadcast_to(gda8[:, 0:1], (8, LANE))
    o_a = _da_decode(page_table, cache_da.reshape(-1, PAGE, 1024), q8, pr["da", F32].reshape(b, 1, 1024),
                     bl, b0, da_lambda).reshape(b, 512)
    n_pool = cache_nsa.shape[0]
    ab_pool = _cmp_stage1(cache_nsa.reshape(n_pool * (PAGE // CMP_STRIDE), CMP_STRIDE * 512), w1)
    ab_pool = ab_pool.reshape(n_pool, PAGE // CMP_STRIDE, 512)
    qb = pr["qb", BF16].reshape(b, NSA_KV, NSA_HPG, NSA_HD)
    qc = jnp.zeros((b, NSA_KV, 8, LANE), BF16).at[:, :, :NSA_HPG, :NSA_HD].set(qb)
    gdb = gd_b.reshape(NSA_KV, NSA_HPG, -1)
    bw = jnp.zeros((NSA_KV, 8, WINDOW + LANE), F32)
    bw = bw.at[:, :NSA_HPG, :WINDOW].set(gdb[:, :, WINDOW - jnp.arange(WINDOW)])
    bw = bw.at[:, :NSA_HPG, WINDOW].set(gdb[:, :, 0])
    o_c, o_w, idx = _nsa_decode1(page_table, ab_pool, qc, cache_swa.reshape(b, WINDOW, 256),
                                 pr["kvw", F32].reshape(b, 1, 256), bw, pe8, w1c, w2, _ovl_t(past // CMP_STRIDE))
    idx = idx[:, :, 0, :N_SEL - 1]
    qs = jnp.zeros((b, NSA_KV, 8, 256), BF16).at[:, :, :NSA_HPG, 128:128 + NSA_HD].set(qb)
    bs = jnp.zeros((NSA_KV, 8, LANE), F32)
    bs = bs.at[:, :NSA_HPG, :SEL_BLOCK].set(gdb[:, :, SEL_BLOCK - jnp.arange(SEL_BLOCK)])
    bs = bs.at[:, :NSA_HPG, SEL_BLOCK].set(gdb[:, :, 0])
    o_s = _nsa_decode2(page_table, idx, cache_nsa.reshape(n_pool * 2, SEL_BLOCK, 512), qs,
                       pr["nsa", F32].reshape(b, 1, 512), bs)
    o_c = o_c[:, :, :NSA_HPG, :NSA_HD].reshape(b, 512)
    o_w = o_w[:, :, :NSA_HPG, NSA_HD:].reshape(b, 512)
    o_s = o_s[:, :, :NSA_HPG, 192:].reshape(b, 512)
    return o_a, o_c, o_s, o_w


def kernel(x_prompt, x_sample, cache_da_kv, cache_nsa_kv, cache_swa_kv, page_table, c_prompt, c_sample, rel_bias, w_ada, b_ada, w_in, da_lambda, da_subln, cmp_pe, cmp_w1, cmp_w2, w_br_a, w_br_b, w_out, ln1_g, ln1_b, w_up, w_down, ln2_g, ln2_b):
    bp, s, _ = x_prompt.shape
    bs_ = x_sample.shape[0]
    w_perm = _perm_w_in(w_in[0])
    w1, w2 = _cmp_weights(cmp_w1[0], cmp_w2[0])
    pe8 = jnp.broadcast_to(cmp_pe[0].reshape(2, 1, CMP_LEN * NSA_HD), (2, 8, CMP_LEN * NSA_HD))
    w1c = cmp_w1[0].reshape(2, CMP_LEN * NSA_HD, CMP_HID)
    gd_a = _dist_bias(rel_bias[:, :DA_HEADS], 1024)
    gd_b = _dist_bias(rel_bias[:, DA_HEADS:], 1024)
    wa, wb, wo = w_br_a[0].astype(BF16), w_br_b[0].astype(BF16), w_out[0].astype(BF16)
    wu, wd = w_up[0].astype(BF16), w_down[0].astype(BF16)
    sub = da_subln[0].reshape(1, DA_VD)
    eg = _gate_expand()
    g1, b1 = ln1_g[0].reshape(1, D_MODEL), ln1_b[0].reshape(1, D_MODEL)
    g2, b2 = ln2_g[0].reshape(1, D_MODEL), ln2_b[0].reshape(1, D_MODEL)
    lam = da_lambda[0]

    n_c = bp + bs_
    c_all = jnp.pad(jnp.concatenate([c_prompt, c_sample], 0), ((0, (-n_c) % 8), (0, 0)))
    ada = _ada(c_all, w_ada[0], b_ada[0])[:n_c].reshape(n_c, 6, D_MODEL)
    ada_p = jnp.transpose(ada[:bp], (1, 0, 2)).reshape(6, bp, 1, D_MODEL)
    ada_s = jnp.transpose(ada[bp:], (1, 0, 2)).reshape(6, 1, bs_, D_MODEL)

    def tail(mix, pr, x2d, ada_x, tm, tpg):
        o_a, o_c, o_s, o_w = mix
        x1 = _tail1(o_a, o_c, o_s, o_w, pr["gate", F32], pr["ma", F32], pr["mb", F32], x2d, ada_x,
                    wa, wb, wo, sub, eg, g1, b1, tm, tpg)
        return _tail2(x1, ada_x, wu, wd, g2, b2, tm, tpg)

    xp = x_prompt.reshape(bp * s, D_MODEL)
    tm_p = 256
    pr_p = _proj(xp, ada_p, w_perm, tm_p, s // tm_p)
    mix_p = _prompt_mixers(pr_p, bp, s, gd_a, gd_b, lam, w1, w2, pe8, w1c)
    tm_t = 512
    y_p = tail(mix_p, pr_p, xp, ada_p, tm_t, s // tm_t).reshape(bp, s, D_MODEL)
    xs = x_sample.reshape(bs_, D_MODEL)
    pr_s = _proj(xs, ada_s, w_perm, bs_, 1)
    mix_s = _sample_mixers(pr_s, page_table, cache_da_kv[0], cache_nsa_kv[0], cache_swa_kv[0],
                           gd_a, gd_b, lam, w1, w2, pe8, w1c)
    y_s = tail(mix_s, pr_s, xs, ada_s, bs_, 1).reshape(bs_, 1, D_MODEL)

    win = min(WINDOW, s)
    new_da_p = pr_p["da", F32].reshape(1, bp, s, DA_HEADS, 4 * DA_HD)
    new_nsa_p = pr_p["nsa", F32].reshape(1, bp, s, NSA_KV, 4 * NSA_HD)
    new_swa_p = pr_p["kvw", F32].reshape(bp, s, NSA_KV, 2 * NSA_HD)[None, :, s - win:]
    new_da_s = pr_s["da", F32].reshape(1, bs_, 1, DA_HEADS, 4 * DA_HD)
    new_nsa_s = pr_s["nsa", F32].reshape(1, bs_, 1, NSA_KV, 4 * NSA_HD)
    new_swa_s = jnp.concatenate([cache_swa_kv[0][:, 1:], pr_s["kvw", F32].reshape(bs_, 1, NSA_KV, 2 * NSA_HD)],
                                axis=1)[None]
    return (y_p, y_s, new_da_p, new_nsa_p, new_swa_p, new_da_s, new_nsa_s, new_swa_s)
```

```python
import functools
import math

import numpy as np
import jax
import jax.numpy as jnp
from jax import lax
from jax.experimental import pallas as pl
from jax.experimental.pallas import tpu as pltpu

F32 = jnp.float32
BF16 = jnp.bfloat16

D_MODEL = 1024
PAGE = 128
DA_HEADS = 4
DA_HD = 64
DA_VD = 128
NSA_HEADS = 8
NSA_KV = 2
NSA_HPG = 4
NSA_HD = 64
CMP_STRIDE = 16
CMP_LEN = 32
CMP_HID = 64
SEL_BLOCK = 64
N_SEL = 16
WINDOW = 512
D_FF = 4096
N_BUCKETS = 32
MAX_DIST = 128
DEPTH = 1
ALPHA = (2 * DEPTH) ** 0.25
LN_EPS = 1e-5
RMS_EPS = 1e-5
NEG = -1e30
SEL_BIG = 1e9
LAM_INIT = 0.8 - 0.6 * math.exp(-0.3 * 0)
SPLIT_SIZES = (512, 512, 512, 512, 256, 256, 256, 24, 1024, 1024)

LANE = 128
VMEM_LIMIT = 56 * 1024 * 1024
TQ = 128
TK = 512
TDA = 512
SEL_LANES = 128
REMOVED = -3e38
PP = 8


def _cparams(sem):
    return pltpu.CompilerParams(dimension_semantics=sem, vmem_limit_bytes=VMEM_LIMIT)


def _mm(a, b):
    return jnp.dot(a.astype(BF16), b.astype(BF16), preferred_element_type=F32)


def _mm_nt(a, b):
    return lax.dot_general(a.astype(BF16), b.astype(BF16), (((1,), (1,)), ((), ())),
                           preferred_element_type=F32)


def _mm3(x, w):
    hi = x.astype(BF16)
    r = x - hi.astype(F32)
    mid = r.astype(BF16)
    lo = (r - mid.astype(F32)).astype(BF16)
    return (jnp.dot(hi, w, preferred_element_type=F32) + jnp.dot(mid, w, preferred_element_type=F32)
            + jnp.dot(lo, w, preferred_element_type=F32))


def _masked_softmax(s, valid):
    l = jnp.where(valid, s, NEG)
    m = jnp.max(l, axis=-1, keepdims=True)
    e = jnp.where(valid, jnp.exp(l - m), 0.0)
    return e / jnp.maximum(jnp.sum(e, axis=-1, keepdims=True), 1e-30)


def _online_update(s, valid, v, m_ref, l_ref, a_ref):
    if valid is not None:
        s = jnp.where(valid, s, NEG)
    m_old = m_ref[...]
    m_new = jnp.maximum(m_old, jnp.max(s, axis=-1, keepdims=True))
    p = jnp.exp(s - m_new)
    if valid is not None:
        p = jnp.where(valid, p, 0.0)
    alpha = jnp.exp(m_old - m_new)
    l_ref[...] = alpha * l_ref[...] + jnp.sum(p, axis=-1, keepdims=True)
    a_ref[...] = alpha * a_ref[...] + _mm(p, v)
    m_ref[...] = m_new


def _t5_bucket(dist):
    n = jnp.maximum(dist, 0)
    max_exact = N_BUCKETS // 2
    nf = jnp.maximum(n, 1).astype(F32)
    large = max_exact + (jnp.log(nf / max_exact) / math.log(MAX_DIST / max_exact)
                         * (N_BUCKETS - max_exact)).astype(jnp.int32)
    return jnp.where(n < max_exact, n, jnp.minimum(large, N_BUCKETS - 1))


def _dist_bias(tbl, n):
    d = jnp.arange(n, dtype=jnp.int32)
    g = tbl[_t5_bucket(d)] - tbl[N_BUCKETS - 1][None, :]
    return jnp.transpose(g)


def _toeplitz(gd, offset, rows, cols, below=0.0):
    heads, n = gd.shape
    length = rows + cols - 1
    assert offset + rows <= n
    lo = offset - cols + 1
    hvec = gd[:, max(lo, 0):offset + rows]
    if lo < 0:
        hvec = jnp.concatenate([jnp.full((heads, -lo), below, gd.dtype), hvec], axis=1)
    rev = jnp.concatenate([hvec[:, ::-1], jnp.zeros((heads, 1), gd.dtype)], axis=1)
    flat = jnp.tile(rev, (1, rows))[:, :rows * length].reshape(heads, rows, length)
    return flat[:, :, rows - 1:rows - 1 + cols]


def _diff_lambda(lam_ref):
    l = lam_ref[...]
    a = jnp.sum(l[0:1, :] * l[1:2, :], axis=-1, keepdims=True)
    b = jnp.sum(l[2:3, :] * l[3:4, :], axis=-1, keepdims=True)
    return jnp.exp(a) - jnp.exp(b) + LAM_INIT


def _ada_kernel(c_ref, w_ref, b_ref, o_ref):
    c = c_ref[...]
    o_ref[...] = _mm(c * jax.nn.sigmoid(c), w_ref[...]) + b_ref[...]


def _ada(c, w_ada, b_ada):
    m = c.shape[0]
    n = w_ada.shape[1]
    tn = 512
    return pl.pallas_call(
        _ada_kernel,
        grid=(n // tn,),
        in_specs=[pl.BlockSpec((m, D_MODEL), lambda j: (0, 0)),
                  pl.BlockSpec((D_MODEL, tn), lambda j: (0, j)),
                  pl.BlockSpec((1, tn), lambda j: (0, j))],
        out_specs=pl.BlockSpec((m, tn), lambda j: (0, j)),
        out_shape=jax.ShapeDtypeStruct((m, n), F32),
        compiler_params=_cparams(("arbitrary",)),
        name="ada",
    )(c, w_ada, b_ada.reshape(1, n))


PROJ_GROUPS = (("qa", 512, False, True), ("da", 1024, True, True), ("qb", 512, False, True),
               ("nsa", 512, True, False), ("kvw", 256, True, False), ("gate", 128, True, False),
               ("ma", 1024, True, False), ("mb", 1024, True, False),
               ("ksel", 512, False, True), ("kwin", 512, False, True))
PROJ_OUTS = tuple((n, w, dt) for n, w, f, b in PROJ_GROUPS for dt, on in ((F32, f), (BF16, b)) if on)
PROJ_W = sum(w for _, w, _, _ in PROJ_GROUPS)


def _perm_w_in(w_in):
    parts = jnp.split(w_in, np.cumsum(SPLIT_SIZES)[:-1].tolist(), axis=1)
    qa, ka, va, qn, kvc, kvs, kvw, gb, ma, mb = parts
    da = jnp.concatenate([jnp.concatenate([ka[:, h * 128:(h + 1) * 128], va[:, h * 128:(h + 1) * 128]], 1)
                          for h in range(DA_HEADS)], 1)
    nsa = jnp.concatenate([jnp.concatenate([kvc[:, g * 128:(g + 1) * 128], kvs[:, g * 128:(g + 1) * 128]], 1)
                           for g in range(NSA_KV)], 1)
    gate = jnp.pad(gb, ((0, 0), (0, LANE - gb.shape[1])))

    def dup(kv):
        out = []
        for g in range(NSA_KV):
            k = kv[:, g * 128:g * 128 + 64]
            v = kv[:, g * 128 + 64:g * 128 + 128]
            out += [k, k, v, v]
        return jnp.concatenate(out, 1)

    cols = dict(qa=qa, da=da, qb=qn, nsa=nsa, kvw=kvw, gate=gate, ma=ma, mb=mb, ksel=dup(kvs), kwin=dup(kvw))
    return jnp.concatenate([cols[n] for n, _, _, _ in PROJ_GROUPS], 1).astype(BF16)


def _proj_kernel(x_ref, ada_ref, w_ref, *o_refs):
    h = (x_ref[...] * (1.0 + ada_ref[1]) + ada_ref[0]).astype(BF16)
    off = 0
    k = 0
    for _, width, f32_on, bf_on in PROJ_GROUPS:
        acc = jnp.dot(h, w_ref[:, off:off + width], preferred_element_type=F32)
        if f32_on:
            o_refs[k][...] = acc
            k += 1
        if bf_on:
            o_refs[k][...] = acc.astype(BF16)
            k += 1
        off += width


def _ada_spec(ada, tm, tiles_per_group):
    r = ada.shape[2]
    return pl.BlockSpec((6, None, r, D_MODEL), lambda i, *_: (0, i // tiles_per_group, 0, 0))


def _proj(x2d, ada, w_perm, tm, tiles_per_group):
    m = x2d.shape[0]
    outs = pl.pallas_call(
        _proj_kernel,
        grid=(m // tm,),
        in_specs=[pl.BlockSpec((tm, D_MODEL), lambda i: (i, 0)),
                  _ada_spec(ada, tm, tiles_per_group),
                  pl.BlockSpec((D_MODEL, PROJ_W), lambda i: (0, 0))],
        out_specs=[pl.BlockSpec((tm, w), lambda i: (i, 0)) for _, w, _ in PROJ_OUTS],
        out_shape=[jax.ShapeDtypeStruct((m, w), dt) for _, w, dt in PROJ_OUTS],
        compiler_params=_cparams(("arbitrary",)),
        name="proj",
    )(x2d, ada, w_perm)
    return {(n, dt): o for (n, _, dt), o in zip(PROJ_OUTS, outs)}


def _da_kernel(qi_tab, kj_tab, q_ref, kv_ref, bd_ref, bs_ref, lam_ref, o_ref,
               m1, l1, a1, m2, l2, a2):
    s_id = pl.program_id(2)
    qi = qi_tab[s_id]
    kj = kj_tab[s_id]
    scale = DA_HD ** -0.5

    @pl.when(kj == 0)
    def _():
        for m_ref, l_ref, a_ref in ((m1, l1, a1), (m2, l2, a2)):
            m_ref[...] = jnp.full(m_ref.shape, NEG, F32)
            l_ref[...] = jnp.zeros(l_ref.shape, F32)
            a_ref[...] = jnp.zeros(a_ref.shape, F32)

    def step(bias, causal):
        q = q_ref[...]
        lane = lax.broadcasted_iota(jnp.int32, q.shape, 1)
        zero = jnp.zeros_like(q)
        kk = kv_ref[:, 0:128]
        v = kv_ref[:, 128:256]
        valid = None
        if causal:
            row = lax.broadcasted_iota(jnp.int32, (TDA, TDA), 0)
            col = lax.broadcasted_iota(jnp.int32, (TDA, TDA), 1)
            valid = col <= row
        for qm, m_ref, l_ref, a_ref in ((jnp.where(lane < DA_HD, q, zero), m1, l1, a1),
                                        (jnp.where(lane >= DA_HD, q, zero), m2, l2, a2)):
            s = _mm_nt(qm, kk) * scale
            if bias is not None:
                s = s + bias
            _online_update(s, valid, v, m_ref, l_ref, a_ref)

    @pl.when(kj == qi)
    def _():
        step(bd_ref[...], True)
        lam = _diff_lambda(lam_ref)
        o_ref[...] = a1[...] / l1[...] - lam * (a2[...] / l2[...])

    @pl.when(kj == qi - 1)
    def _():
        step(bs_ref[...], False)

    @pl.when(kj < qi - 1)
    def _():
        step(None, False)


def _da_prompt(qa, da_bf, gd_a, da_lambda):
    b, s, _ = qa.shape
    nq = s // TDA
    steps = [(qi, kj) for qi in range(nq) for kj in range(qi + 1)]
    qi_tab = jnp.asarray(np.array([p[0] for p in steps], np.int32))
    kj_tab = jnp.asarray(np.array([p[1] for p in steps], np.int32))
    bd = _toeplitz(gd_a, 0, TDA, TDA)
    bs = _toeplitz(gd_a, TDA, TDA, TDA)
    grid_spec = pltpu.PrefetchScalarGridSpec(
        num_scalar_prefetch=2,
        grid=(b, DA_HEADS, len(steps)),
        in_specs=[pl.BlockSpec((None, TDA, 128), lambda bi, h, t, qt, kt: (bi, qt[t], h)),
                  pl.BlockSpec((None, TDA, 256), lambda bi, h, t, qt, kt: (bi, kt[t], h)),
                  pl.BlockSpec((None, TDA, TDA), lambda bi, h, t, qt, kt: (h, 0, 0)),
                  pl.BlockSpec((None, TDA, TDA), lambda bi, h, t, qt, kt: (h, 0, 0)),
                  pl.BlockSpec((4, DA_HD), lambda bi, h, t, qt, kt: (0, 0))],
        out_specs=pl.BlockSpec((None, TDA, 128), lambda bi, h, t, qt, kt: (bi, qt[t], h)),
        scratch_shapes=[pltpu.VMEM((TDA, 1), F32), pltpu.VMEM((TDA, 1), F32), pltpu.VMEM((TDA, 128), F32),
                        pltpu.VMEM((TDA, 1), F32), pltpu.VMEM((TDA, 1), F32), pltpu.VMEM((TDA, 128), F32)])
    return pl.pallas_call(
        _da_kernel,
        grid_spec=grid_spec,
        out_shape=jax.ShapeDtypeStruct((b, s, DA_HEADS * DA_VD), F32),
        compiler_params=_cparams(("arbitrary", "arbitrary", "arbitrary")),
        name="da_prompt",
    )(qi_tab, kj_tab, qa, da_bf, bd, bs, da_lambda)


def _cmp_weights(cmp_w1, cmp_w2):
    w1 = jnp.zeros((CMP_STRIDE, 2, NSA_HD, 2, 2, CMP_HID), F32)
    for half in range(2):
        for c in range(2):
            blk = cmp_w1[c, half * CMP_STRIDE:(half + 1) * CMP_STRIDE]
            w1 = w1.at[:, c, :, half, c, :].set(blk)
    w1 = w1.reshape(CMP_STRIDE * 2 * NSA_HD, 2 * 2 * CMP_HID).astype(BF16)
    w2 = jnp.zeros((2, CMP_HID, 2, 2, NSA_HD), F32)
    for c in range(2):
        for rep in range(2):
            w2 = w2.at[c, :, c, rep, :].set(cmp_w2[c])
    w2 = w2.reshape(2 * CMP_HID, 2 * 2 * NSA_HD).astype(BF16)
    return w1, w2


def _cmp1_kernel(x_ref, w_ref, o_ref, *, pos_lanes, grp_lanes):
    for g in range(NSA_KV):
        xg = jnp.concatenate([x_ref[:, l * pos_lanes + g * grp_lanes:l * pos_lanes + g * grp_lanes + LANE]
                              for l in range(CMP_STRIDE)], axis=1)
        o_ref[:, g * 256:(g + 1) * 256] = _mm(xg, w_ref[...])


def _cmp_stage1(x2d, w1, grp_lanes):
    r, width = x2d.shape
    tm = math.gcd(r, 256)
    return pl.pallas_call(
        functools.partial(_cmp1_kernel, pos_lanes=width // CMP_STRIDE, grp_lanes=grp_lanes),
        grid=(r // tm,),
        in_specs=[pl.BlockSpec((tm, width), lambda i: (i, 0)),
                  pl.BlockSpec(w1.shape, lambda i: (0, 0))],
        out_specs=pl.BlockSpec((tm, 512), lambda i: (i, 0)),
        out_shape=jax.ShapeDtypeStruct((r, 512), F32),
        compiler_params=_cparams(("arbitrary",)),
        name="cmp_stage1",
    )(x2d, w1)


def _cmp_stage2(ab, pe_ref, w1c_ref, w2_ref):
    n = ab.shape[0]
    cst = jnp.concatenate([_mm(pe_ref[c], w1c_ref[c])[0:1, :] for c in range(2)], axis=1)
    outs = []
    for g in range(NSA_KV):
        a = ab[:, g * 256:g * 256 + 128]
        bn = pltpu.roll(ab[:, g * 256 + 128:g * 256 + 256], n - 1, 0)
        hid = jax.nn.gelu(a + bn + cst)
        outs.append(_mm(hid, w2_ref[...]))
    return jnp.concatenate(outs, axis=1)


def _cmp2_kernel(ab_ref, pe_ref, w1c_ref, w2_ref, o_ref):
    o_ref[...] = _cmp_stage2(ab_ref[...], pe_ref, w1c_ref, w2_ref).astype(BF16)


def _cmp_stage2_prompt(ab, pe8, w1c, w2):
    b, n, _ = ab.shape
    return pl.pallas_call(
        _cmp2_kernel,
        grid=(b,),
        in_specs=[pl.BlockSpec((None, n, 512), lambda i: (i, 0, 0)),
                  pl.BlockSpec(pe8.shape, lambda i: (0, 0, 0)),
                  pl.BlockSpec(w1c.shape, lambda i: (0, 0, 0)),
                  pl.BlockSpec(w2.shape, lambda i: (0, 0))],
        out_specs=pl.BlockSpec((None, n, 512), lambda i: (i, 0, 0)),
        out_shape=jax.ShapeDtypeStruct((b, n, 512), BF16),
        compiler_params=_cparams(("arbitrary",)),
        name="cmp_stage2",
    )(ab, pe8, w1c, w2)


def _ovl_t(n_chunks):
    n = np.arange(n_chunks)[:, None]
    j = np.arange(SEL_LANES)[None, :]
    ovl = (n * CMP_STRIDE < j * SEL_BLOCK + SEL_BLOCK) & (j * SEL_BLOCK < n * CMP_STRIDE + CMP_LEN)
    ovl &= n < n_chunks - 1
    return jnp.asarray(ovl.astype(np.float32)).astype(BF16)


def _topk_select(score, n_pick):
    lane = lax.broadcasted_iota(jnp.int32, score.shape, 1).astype(F32)
    sel = jnp.zeros(score.shape, F32)
    picks = jnp.zeros(score.shape, F32)
    sc = score
    for it in range(n_pick):
        m = jnp.max(sc, axis=-1, keepdims=True)
        first = jnp.min(jnp.where(sc == m, lane, float(SEL_LANES)), axis=-1, keepdims=True)
        hit = lane == first
        sel = jnp.where(hit, 1.0, sel)
        picks = jnp.where(lane == float(it), first, picks)
        sc = jnp.where(hit, REMOVED, sc)
    return sel, picks


def _stack_heads(q):
    lane = lax.broadcasted_iota(jnp.int32, (q.shape[0], LANE), 1)
    zero = jnp.zeros((q.shape[0], LANE), q.dtype)
    parts = []
    for hp in range(NSA_HPG):
        blk = q[:, (hp // 2) * LANE:(hp // 2 + 1) * LANE]
        keep = (lane < NSA_HD) if hp % 2 == 0 else (lane >= NSA_HD)
        parts.append(jnp.where(keep, blk, zero))
    return jnp.concatenate(parts, axis=0)


def _unstack_heads(o, tq):
    lane = lax.broadcasted_iota(jnp.int32, (tq, LANE), 1)
    pairs = [jnp.where(lane < NSA_HD, o[(2 * m) * tq:(2 * m + 1) * tq], o[(2 * m + 1) * tq:(2 * m + 2) * tq])
             for m in range(2)]
    return jnp.concatenate(pairs, axis=1)


def _nsa_cmp_kernel(q_ref, kcvc_ref, ovl_ref, oc_ref, sel_ref, *, n_cmp, n_slc):
    qi = pl.program_id(2)
    n_chunks = kcvc_ref.shape[0]
    qs = _stack_heads(q_ref[...])
    kc2 = kcvc_ref[:, 0:128]
    vc2 = kcvc_ref[:, 128:256]
    s = _mm_nt(qs, kc2) * (NSA_HD ** -0.5)
    rows = NSA_HPG * TQ
    t = qi * TQ + (lax.broadcasted_iota(jnp.int32, (rows, n_chunks), 0) & (TQ - 1))
    n = lax.broadcasted_iota(jnp.int32, (rows, n_chunks), 1)
    p = _masked_softmax(s, (n * CMP_STRIDE + (CMP_LEN - 1) <= t) & (n < n_cmp))
    oc_ref[...] = _unstack_heads(_mm(p, vc2), TQ)
    psum = p[0:TQ] + p[TQ:2 * TQ] + p[2 * TQ:3 * TQ] + p[3 * TQ:4 * TQ]
    imp = _mm3(psum, ovl_ref[...])
    tq = qi * TQ + lax.broadcasted_iota(jnp.int32, (TQ, SEL_LANES), 0)
    blk = lax.broadcasted_iota(jnp.int32, (TQ, SEL_LANES), 1)
    cur = lax.shift_right_logical(tq, 6)
    forced = (blk == 0) | (blk == cur) | (blk == cur - 1)
    score = jnp.where(forced, SEL_BIG, jnp.where(blk * SEL_BLOCK <= tq, imp, -SEL_BIG))
    score = jnp.where(blk < n_slc, score, REMOVED)
    sel, _ = _topk_select(score, min(N_SEL, n_slc))
    sel_ref[...] = sel.astype(BF16)


def _nsa_cmp_prompt(qb, kcvc, ovl):
    b, s, _ = qb.shape
    n_chunks = kcvc.shape[1]
    kern = functools.partial(_nsa_cmp_kernel, n_cmp=n_chunks - 1, n_slc=s // SEL_BLOCK)
    return pl.pallas_call(
        kern,
        grid=(b, NSA_KV, s // TQ),
        in_specs=[pl.BlockSpec((None, TQ, 256), lambda bi, g, qi: (bi, qi, g)),
                  pl.BlockSpec((None, n_chunks, 256), lambda bi, g, qi: (bi, 0, g)),
                  pl.BlockSpec(ovl.shape, lambda bi, g, qi: (0, 0))],
        out_specs=[pl.BlockSpec((None, TQ, 256), lambda bi, g, qi: (bi, qi, g)),
                   pl.BlockSpec((None, None, TQ, SEL_LANES), lambda bi, g, qi: (bi, g, qi, 0))],
        out_shape=[jax.ShapeDtypeStruct((b, s, 512), F32),
                   jax.ShapeDtypeStruct((b, NSA_KV, s, SEL_LANES), BF16)],
        compiler_params=_cparams(("arbitrary", "arbitrary", "arbitrary")),
        name="nsa_cmp",
    )(qb, kcvc, ovl)


def _nsa_sw_kernel(qi_tab, kj_tab, var_tab, kw_tab, q_ref, ks_ref, kw_ref, sel_ref, e_ref, nb_ref,
                   os_ref, ow_ref, ms, ls, as_, mw, lw, aw):
    s_id = pl.program_id(2)
    qi = qi_tab[s_id]
    kj = kj_tab[s_id]
    kjmax = lax.shift_right_logical(qi, 2)
    r = qi & 3
    scale = NSA_HD ** -0.5
    rows = NSA_HPG * TQ

    @pl.when(kj == 0)
    def _():
        for m_ref, l_ref, a_ref in ((ms, ls, as_), (mw, lw, aw)):
            m_ref[...] = jnp.full(m_ref.shape, NEG, F32)
            l_ref[...] = jnp.zeros(l_ref.shape, F32)
            a_ref[...] = jnp.zeros(a_ref.shape, F32)

    def step(use_bias, near, window):
        qs = _stack_heads(q_ref[...])
        mexp = jnp.dot(sel_ref[...], e_ref[...], preferred_element_type=F32)
        chosen = jnp.concatenate([mexp] * NSA_HPG, axis=0) > 0.5
        i = lax.broadcasted_iota(jnp.int32, (rows, TK), 0) & (TQ - 1)
        j = lax.broadcasted_iota(jnp.int32, (rows, TK), 1)
        bias = nb_ref[...].reshape(rows, TK) if use_bias else None
        off = r * TQ if near else TK + r * TQ
        d = off + i - j
        s = _mm_nt(qs, ks_ref[:, 0:128]) * scale
        if bias is not None:
            s = s + bias
        valid = (chosen & (d >= 0)) if near else chosen
        _online_update(s, valid, ks_ref[:, 128:256], ms, ls, as_)
        if window:
            sw = _mm_nt(qs, kw_ref[:, 0:128]) * scale
            if bias is not None:
                sw = sw + bias
            _online_update(sw, (d >= 0) & (d <= WINDOW), kw_ref[:, 128:256], mw, lw, aw)

    @pl.when(kj == kjmax)
    def _():
        step(True, True, True)
        os_ref[...] = _unstack_heads(as_[...] / ls[...], TQ)
        ow_ref[...] = _unstack_heads(aw[...] / lw[...], TQ)

    @pl.when((kj == kjmax - 1) & (r == 0))
    def _():
        step(True, False, True)

    @pl.when((kj == kjmax - 1) & (r != 0))
    def _():
        step(False, False, True)

    @pl.when(kj < kjmax - 1)
    def _():
        step(False, False, False)


def _nsa_sw_prompt(qb, ksel, kwin, sel, gd_b):
    b, s, _ = qb.shape
    nq = s // TQ
    nk = s // TK
    qi_l, kj_l, var_l, kw_l = [], [], [], []
    for qi in range(nq):
        kjmax = qi // 4
        r = qi % 4
        for kj in range(kjmax + 1):
            qi_l.append(qi)
            kj_l.append(kj)
            var_l.append(r if (r != 0 or kj == kjmax) else 4)
            kw_l.append(max(kj, max(kjmax - 1, 0)))
    tabs = [jnp.asarray(np.array(a, np.int32)) for a in (qi_l, kj_l, var_l, kw_l)]
    nb = jnp.stack([_toeplitz(gd_b, o, TQ, TK) for o in (0, 128, 256, 384, 512)], 0)
    nb = nb.reshape(5, NSA_KV, NSA_HPG, TQ, TK).transpose(1, 0, 2, 3, 4)
    e = np.zeros((nk, SEL_LANES, TK), np.float32)
    for kj in range(nk):
        for k in range(TK):
            e[kj, kj * (TK // SEL_BLOCK) + k // SEL_BLOCK, k] = 1.0
    e = jnp.asarray(e).astype(BF16)
    im = lambda f: (lambda bi, g, t, qt, kt, vt, wt: f(bi, g, t, qt, kt, vt, wt))
    grid_spec = pltpu.PrefetchScalarGridSpec(
        num_scalar_prefetch=4,
        grid=(b, NSA_KV, len(qi_l)),
        in_specs=[pl.BlockSpec((None, TQ, 256), im(lambda bi, g, t, qt, kt, vt, wt: (bi, qt[t], g))),
                  pl.BlockSpec((None, TK, 256), im(lambda bi, g, t, qt, kt, vt, wt: (bi, kt[t], g))),
                  pl.BlockSpec((None, TK, 256), im(lambda bi, g, t, qt, kt, vt, wt: (bi, wt[t], g))),
                  pl.BlockSpec((None, None, TQ, SEL_LANES), im(lambda bi, g, t, qt, kt, vt, wt: (bi, g, qt[t], 0))),
                  pl.BlockSpec((None, SEL_LANES, TK), im(lambda bi, g, t, qt, kt, vt, wt: (kt[t], 0, 0))),
                  pl.BlockSpec((None, None, NSA_HPG, TQ, TK),
                               im(lambda bi, g, t, qt, kt, vt, wt: (g, vt[t], 0, 0, 0)))],
        out_specs=[pl.BlockSpec((None, TQ, 256), im(lambda bi, g, t, qt, kt, vt, wt: (bi, qt[t], g))),
                   pl.BlockSpec((None, TQ, 256), im(lambda bi, g, t, qt, kt, vt, wt: (bi, qt[t], g)))],
        scratch_shapes=[pltpu.VMEM((NSA_HPG * TQ, 1), F32), pltpu.VMEM((NSA_HPG * TQ, 1), F32),
                        pltpu.VMEM((NSA_HPG * TQ, 128), F32),
                        pltpu.VMEM((NSA_HPG * TQ, 1), F32), pltpu.VMEM((NSA_HPG * TQ, 1), F32),
                        pltpu.VMEM((NSA_HPG * TQ, 128), F32)])
    return pl.pallas_call(
        _nsa_sw_kernel,
        grid_spec=grid_spec,
        out_shape=[jax.ShapeDtypeStruct((b, s, 512), F32), jax.ShapeDtypeStruct((b, s, 512), F32)],
        compiler_params=_cparams(("arbitrary", "arbitrary", "arbitrary")),
        name="nsa_sel_win",
    )(*tabs, qb, ksel, kwin, sel, e, nb)


def _layer_norm(x, g, b):
    mu = jnp.mean(x, axis=-1, keepdims=True)
    xc = x - mu
    var = jnp.mean(xc * xc, axis=-1, keepdims=True)
    return xc * lax.rsqrt(var + LN_EPS) * g + b


def _gate_expand():
    e = np.zeros((3, LANE, NSA_HEADS * NSA_HD), np.float32)
    for h in range(NSA_HEADS):
        for j in range(3):
            e[j, h * 3 + j, h * NSA_HD:(h + 1) * NSA_HD] = 1.0
    return jnp.asarray(e).astype(BF16)


def _tail1_kernel(oa_ref, oc_ref, os_ref, ow_ref, gate_ref, ma_ref, mb_ref, x_ref, ada_ref,
                  wa_ref, wb_ref, wo_ref, sub_ref, eg_ref, g1_ref, b1_ref, o_ref):
    oa = oa_ref[...]
    parts = []
    for h in range(DA_HEADS):
        of = oa[:, h * DA_VD:(h + 1) * DA_VD]
        rr = lax.rsqrt(jnp.mean(of * of, axis=-1, keepdims=True) + RMS_EPS)
        parts.append(of * rr * sub_ref[...] * (1.0 - LAM_INIT))
    oan = jnp.concatenate(parts, axis=1)
    sg = jax.nn.sigmoid(gate_ref[...])
    ob = (_mm3(sg, eg_ref[0]) * oc_ref[...] + _mm3(sg, eg_ref[1]) * os_ref[...]
          + _mm3(sg, eg_ref[2]) * ow_ref[...])
    y = (jax.nn.sigmoid(ma_ref[...]) * _mm(oan, wa_ref[...])
         + jax.nn.sigmoid(mb_ref[...]) * _mm(ob, wb_ref[...]))
    z = ALPHA * x_ref[...] + ada_ref[2] * _mm(y, wo_ref[...])
    o_ref[...] = _layer_norm(z, g1_ref[...], b1_ref[...])


def _tail1(oa, oc, os_, ow, gate, ma, mb, x2d, ada, wa, wb, wo, sub, eg, g1, b1, tm, tiles_per_group):
    m = x2d.shape[0]
    row = lambda w: pl.BlockSpec((tm, w), lambda i: (i, 0))
    full = lambda a: pl.BlockSpec(a.shape, lambda i: (0,) * a.ndim)
    return pl.pallas_call(
        _tail1_kernel,
        grid=(m // tm,),
        in_specs=[row(512), row(512), row(512), row(512), row(128), row(1024), row(1024), row(1024),
                  _ada_spec(ada, tm, tiles_per_group),
                  full(wa), full(wb), full(wo), full(sub), full(eg), full(g1), full(b1)],
        out_specs=row(1024),
        out_shape=jax.ShapeDtypeStruct((m, D_MODEL), F32),
        compiler_params=_cparams(("arbitrary",)),
        name="tail_merge",
    )(oa, oc, os_, ow, gate, ma, mb, x2d, ada, wa, wb, wo, sub, eg, g1, b1)


def _tail2_kernel(x_ref, ada_ref, wu_ref, wd_ref, g2_ref, b2_ref, o_ref, h_scr, acc):
    f = pl.program_id(1)

    @pl.when(f == 0)
    def _():
        h_scr[...] = (x_ref[...] * (1.0 + ada_ref[4]) + ada_ref[3]).astype(BF16)
        acc[...] = jnp.zeros(acc.shape, F32)

    u = jnp.maximum(jnp.dot(h_scr[...], wu_ref[...], preferred_element_type=F32), 0.0)
    acc[...] += _mm(u * u, wd_ref[...])

    @pl.when(f == pl.num_programs(1) - 1)
    def _():
        z = ALPHA * x_ref[...] + ada_ref[5] * acc[...]
        o_ref[...] = _layer_norm(z, g2_ref[...], b2_ref[...])


def _tail2(x1, ada, wu, wd, g2, b2, tm, tiles_per_group):
    m = x1.shape[0]
    tf = 1024
    return pl.pallas_call(
        _tail2_kernel,
        grid=(m // tm, D_FF // tf),
        in_specs=[pl.BlockSpec((tm, D_MODEL), lambda i, f: (i, 0)),
                  _ada_spec(ada, tm, tiles_per_group),
                  pl.BlockSpec((D_MODEL, tf), lambda i, f: (0, f)),
                  pl.BlockSpec((tf, D_MODEL), lambda i, f: (f, 0)),
                  pl.BlockSpec((1, D_MODEL), lambda i, f: (0, 0)),
                  pl.BlockSpec((1, D_MODEL), lambda i, f: (0, 0))],
        out_specs=pl.BlockSpec((tm, D_MODEL), lambda i, f: (i, 0)),
        out_shape=jax.ShapeDtypeStruct((m, D_MODEL), F32),
        scratch_shapes=[pltpu.VMEM((tm, D_MODEL), BF16), pltpu.VMEM((tm, D_MODEL), F32)],
        compiler_params=_cparams(("arbitrary", "arbitrary")),
        name="tail_mlp",
    )(x1, ada, wu, wd, g2, b2)


def _da_decode_kernel(pt_ref, *refs):
    pages = refs[:PP]
    q_ref, kn_ref, vn_ref, bl_ref, b0_ref, lam_ref, o_ref, m_ref, l_ref, a_ref = refs[PP:]
    j = pl.program_id(1)
    last = j == pl.num_programs(1) - 1
    scale = DA_HD ** -0.5
    rows_pg = PAGE * 2 * DA_HEADS

    @pl.when(j == 0)
    def _():
        m_ref[...] = jnp.full(m_ref.shape, NEG, F32)
        l_ref[...] = jnp.zeros(l_ref.shape, F32)
        a_ref[...] = jnp.zeros(a_ref.shape, F32)

    q = q_ref[...]
    row = lax.broadcasted_iota(jnp.int32, (8, rows_pg), 0)
    col = lax.broadcasted_iota(jnp.int32, (8, rows_pg), 1)
    cmask = jnp.where((col & 7) == lax.shift_right_logical(row, 1), 0.0, NEG)
    scores = []
    xs = []
    for k in range(PP):
        x = pages[k][...].astype(BF16)
        sc = _mm_nt(q, x) * scale + cmask
        if k == PP - 1:
            sc = sc + jnp.where(last, bl_ref[...], 0.0)
        scores.append(sc)
        xs.append(x)
    s = jnp.concatenate(scores, axis=1)
    m_old = m_ref[...]
    m_new = jnp.maximum(m_old, jnp.max(s, axis=-1, keepdims=True))
    p = jnp.exp(s - m_new)
    alpha = jnp.exp(m_old - m_new)
    l_ref[...] = alpha * l_ref[...] + jnp.sum(p, axis=-1, keepdims=True)
    acc = alpha * a_ref[...]
    for k in range(PP):
        pv = pltpu.roll(p[:, k * rows_pg:(k + 1) * rows_pg], DA_HEADS, 1)
        acc = acc + _mm(pv, xs[k])
    a_ref[...] = acc
    m_ref[...] = m_new

    @pl.when(last)
    def _():
        s_new = jnp.sum(q.astype(F32) * kn_ref[...], axis=-1, keepdims=True) * scale + b0_ref[:, 0:1]
        m_o = m_ref[...]
        m_n = jnp.maximum(m_o, s_new)
        p_new = jnp.exp(s_new - m_n)
        al = jnp.exp(m_o - m_n)
        raw = (al * a_ref[...] + p_new * vn_ref[...]) / (al * l_ref[...] + p_new)
        o_ref[...] = raw - _diff_lambda(lam_ref) * pltpu.roll(raw, 7, 0)


def _da_decode(page_table, cache_rows, q8, k_new, v_new, bl, b0, da_lambda):
    b, n_pages = page_table.shape
    rows_pg = PAGE * 2 * DA_HEADS
    page_spec = lambda k: pl.BlockSpec((rows_pg, LANE), lambda bi, j, pt: (pt[bi, j * PP + k], 0))
    per_row = lambda: pl.BlockSpec((None, 8, LANE), lambda bi, j, pt: (bi, 0, 0))
    grid_spec = pltpu.PrefetchScalarGridSpec(
        num_scalar_prefetch=1,
        grid=(b, n_pages // PP),
        in_specs=[page_spec(k) for k in range(PP)] + [
            per_row(), per_row(), per_row(),
            pl.BlockSpec((8, rows_pg), lambda bi, j, pt: (0, 0)),
            pl.BlockSpec((8, LANE), lambda bi, j, pt: (0, 0)),
            pl.BlockSpec((4, DA_HD), lambda bi, j, pt: (0, 0))],
        out_specs=per_row(),
        scratch_shapes=[pltpu.VMEM((8, 1), F32), pltpu.VMEM((8, 1), F32), pltpu.VMEM((8, LANE), F32)])
    return pl.pallas_call(
        _da_decode_kernel,
        grid_spec=grid_spec,
        out_shape=jax.ShapeDtypeStruct((b, 8, LANE), F32),
        compiler_params=_cparams(("arbitrary", "arbitrary")),
        name="da_decode",
    )(page_table, *([cache_rows] * PP), q8, k_new, v_new, bl, b0, da_lambda)


def _nsa_decode1_kernel(pt_ref, ab_hbm, q_ref, swa_ref, new_ref, bw_ref, pe_ref, w1c_ref, w2_ref, ovl_ref,
                        oc_ref, ow_ref, idx_ref, abuf, sem, *, n_pages):
    b = pl.program_id(0)
    copies = [pltpu.make_async_copy(ab_hbm.at[pt_ref[b, p]], abuf.at[p], sem) for p in range(n_pages)]
    for c in copies:
        c.start()
    for c in copies:
        c.wait()
    n_chunks = n_pages * (PAGE // CMP_STRIDE)
    kcvc = _cmp_stage2(abuf[...].reshape(n_chunks, 512), pe_ref, w1c_ref, w2_ref)
    scale = NSA_HD ** -0.5
    swa = swa_ref[...].astype(BF16)
    n_win = swa.shape[0]
    for g in range(NSA_KV):
        q = q_ref[g]
        s = _mm_nt(q, kcvc[:, g * 256:g * 256 + 128]) * scale
        n = lax.broadcasted_iota(jnp.int32, s.shape, 1)
        p = _masked_softmax(s, n < n_chunks - 1)
        oc_ref[g] = _mm(p, kcvc[:, g * 256 + 128:g * 256 + 256])
        psum = jnp.sum(p[0:NSA_HPG], axis=0, keepdims=True)
        imp = _mm3(jnp.broadcast_to(psum, (8, n_chunks)), ovl_ref[...])
        blk = lax.broadcasted_iota(jnp.int32, imp.shape, 1)
        n_blk = n_chunks * CMP_STRIDE // SEL_BLOCK
        forced = (blk == 0) | (blk == n_blk - 1)
        score = jnp.where(blk < n_blk, jnp.where(forced, SEL_BIG, imp), REMOVED)
        _, picks = _topk_select(score, N_SEL - 1)
        idx_ref[g] = picks.astype(jnp.int32)
        new = new_ref[g:g + 1, :]
        sw = _mm_nt(q, swa) * scale + bw_ref[g][:, 0:n_win]
        s_new = (jnp.sum(q.astype(F32) * new, axis=-1, keepdims=True) * scale
                 + bw_ref[g][:, n_win:n_win + 1])
        m = jnp.maximum(jnp.max(sw, axis=-1, keepdims=True), s_new)
        e = jnp.exp(sw - m)
        e_new = jnp.exp(s_new - m)
        den = jnp.sum(e, axis=-1, keepdims=True) + e_new
        ow_ref[g] = (_mm(e, swa) + e_new * new) / den


def _nsa_decode1(page_table, ab_pool, qc, cache_swa, kvw_new, bw, pe8, w1c, w2, ovl):
    b, n_pages = page_table.shape
    full = lambda a: pl.BlockSpec(a.shape, lambda bi, pt: (0,) * a.ndim)
    out4 = lambda: pl.BlockSpec((None, NSA_KV, 8, LANE), lambda bi, pt: (bi, 0, 0, 0))
    grid_spec = pltpu.PrefetchScalarGridSpec(
        num_scalar_prefetch=1,
        grid=(b,),
        in_specs=[pl.BlockSpec(memory_space=pl.ANY),
                  pl.BlockSpec((None, NSA_KV, 8, LANE), lambda bi, pt: (bi, 0, 0, 0)),
                  pl.BlockSpec((None, cache_swa.shape[1], LANE), lambda bi, pt: (bi, 0, 0)),
                  pl.BlockSpec((None, NSA_KV, LANE), lambda bi, pt: (bi, 0, 0)),
                  full(bw), full(pe8), full(w1c), full(w2), full(ovl)],
        out_specs=[out4(), out4(), out4()],
        scratch_shapes=[pltpu.VMEM((n_pages, PAGE // CMP_STRIDE, 512), F32), pltpu.SemaphoreType.DMA(())])
    return pl.pallas_call(
        functools.partial(_nsa_decode1_kernel, n_pages=n_pages),
        grid_spec=grid_spec,
        out_shape=[jax.ShapeDtypeStruct((b, NSA_KV, 8, LANE), F32),
                   jax.ShapeDtypeStruct((b, NSA_KV, 8, LANE), F32),
                   jax.ShapeDtypeStruct((b, NSA_KV, 8, LANE), jnp.int32)],
        compiler_params=_cparams(("arbitrary",)),
        name="nsa_decode_cmp_win",
    )(page_table, ab_pool, qc, cache_swa, kvw_new, bw, pe8, w1c, w2, ovl)


def _nsa_decode2_kernel(pt_ref, idx_ref, blk_ref, q_ref, new_ref, bs_ref, o_ref, m_ref, l_ref, a_ref, *, n_blk):
    b = pl.program_id(0)
    g = pl.program_id(1)
    i = pl.program_id(2)
    scale = NSA_HD ** -0.5

    @pl.when(i == 0)
    def _():
        m_ref[...] = jnp.full(m_ref.shape, NEG, F32)
        l_ref[...] = jnp.zeros(l_ref.shape, F32)
        a_ref[...] = jnp.zeros(a_ref.shape, F32)

    q = q_ref[...]
    x = blk_ref[...].astype(BF16)
    near = idx_ref[b, g, i] == n_blk - 1
    cols = x.shape[0]
    s = _mm_nt(q, x) * scale + bs_ref[:, 0:cols] + jnp.where(near, bs_ref[:, cols:2 * cols], 0.0)
    _online_update(s, None, x, m_ref, l_ref, a_ref)

    @pl.when(i == pl.num_programs(2) - 1)
    def _():
        new = new_ref[...]
        s_new = (jnp.sum(q.astype(F32) * new, axis=-1, keepdims=True) * scale
                 + bs_ref[:, 2 * cols:2 * cols + 1])
        m_o = m_ref[...]
        m_n = jnp.maximum(m_o, s_new)
        p_new = jnp.exp(s_new - m_n)
        al = jnp.exp(m_o - m_n)
        o_ref[...] = (al * a_ref[...] + p_new * new) / (al * l_ref[...] + p_new)


def _nsa_decode2(page_table, idx, cache_rows, qc, nsa_new, bs):
    b = page_table.shape[0]
    n_pick = idx.shape[2]
    rows_blk = SEL_BLOCK * 2 * NSA_KV

    def blk_map(bi, g, i, pt, ix):
        blk = ix[bi, g, i]
        return (pt[bi, lax.shift_right_logical(blk, 1)] * 2 + (blk & 1), 0)

    grid_spec = pltpu.PrefetchScalarGridSpec(
        num_scalar_prefetch=2,
        grid=(b, NSA_KV, n_pick),
        in_specs=[pl.BlockSpec((rows_blk, LANE), blk_map),
                  pl.BlockSpec((None, None, 8, LANE), lambda bi, g, i, pt, ix: (bi, g, 0, 0)),
                  pl.BlockSpec((None, None, 1, LANE), lambda bi, g, i, pt, ix: (bi, g, 0, 0)),
                  pl.BlockSpec((None, 8, bs.shape[2]), lambda bi, g, i, pt, ix: (g, 0, 0))],
        out_specs=pl.BlockSpec((None, None, 8, LANE), lambda bi, g, i, pt, ix: (bi, g, 0, 0)),
        scratch_shapes=[pltpu.VMEM((8, 1), F32), pltpu.VMEM((8, 1), F32), pltpu.VMEM((8, LANE), F32)])
    return pl.pallas_call(
        functools.partial(_nsa_decode2_kernel, n_blk=page_table.shape[1] * PAGE // SEL_BLOCK),
        grid_spec=grid_spec,
        out_shape=jax.ShapeDtypeStruct((b, NSA_KV, 8, LANE), F32),
        compiler_params=_cparams(("arbitrary", "arbitrary", "arbitrary")),
        name="nsa_decode_sel",
    )(page_table, idx, cache_rows, qc, nsa_new, bs)


def _prompt_mixers(pr, b, s, gd_a, gd_b, da_lambda, w1, w2, pe8, w1c):
    sh = lambda a: a.reshape(b, s, a.shape[-1])
    o_a = _da_prompt(sh(pr["qa", BF16]), sh(pr["da", BF16]), gd_a, da_lambda)
    n_chunks = s // CMP_STRIDE
    ab = _cmp_stage1(pr["nsa", F32].reshape(b * n_chunks, CMP_STRIDE * 512), w1, 2 * LANE)
    kcvc = _cmp_stage2_prompt(ab.reshape(b, n_chunks, 512), pe8, w1c, w2)
    o_c, sel = _nsa_cmp_prompt(sh(pr["qb", BF16]), kcvc, _ovl_t(n_chunks))
    o_s, o_w = _nsa_sw_prompt(sh(pr["qb", BF16]), sh(pr["ksel", BF16]), sh(pr["kwin", BF16]), sel, gd_b)
    flat = lambda a: a.reshape(b * s, a.shape[-1])
    return flat(o_a), flat(o_c), flat(o_s), flat(o_w)


def _sample_mixers(pr, page_table, cache_da, cache_nsa, cache_swa, gd_a, gd_b, da_lambda, w1, w2, pe8, w1c):
    b, n_pages = page_table.shape
    past = n_pages * PAGE
    n_pool = cache_da.shape[0]
    da_rows = cache_da.reshape(n_pool, PAGE, DA_HEADS, 2, LANE).transpose(0, 1, 3, 2, 4).reshape(-1, LANE)
    nsa_rows = cache_nsa.reshape(n_pool, PAGE, NSA_KV, 2, LANE).transpose(0, 1, 3, 2, 4).reshape(-1, LANE)
    qa = pr["qa", BF16].reshape(b, DA_HEADS, 2, DA_HD)
    q8 = jnp.zeros((b, DA_HEADS, 2, 2, DA_HD), BF16)
    for c in range(2):
        q8 = q8.at[:, :, c, c, :].set(qa[:, :, c])
    q8 = q8.reshape(b, 8, LANE)
    da_new = pr["da", F32].reshape(b, DA_HEADS, 2, LANE)
    k_new = jnp.repeat(da_new[:, :, 0], 2, axis=1)
    v_new = jnp.repeat(da_new[:, :, 1], 2, axis=1)
    gda8 = jnp.repeat(gd_a, 2, axis=0)
    bl = jnp.repeat(gda8[:, PAGE - jnp.arange(PAGE)], 2 * DA_HEADS, axis=1)
    b0 = jnp.broadcast_to(gda8[:, 0:1], (8, LANE))
    o_a = _da_decode(page_table, da_rows, q8, k_new, v_new, bl, b0, da_lambda)
    o_a = o_a[:, 0::2, :].reshape(b, 512)
    chunks = PAGE // CMP_STRIDE
    ab_pool = _cmp_stage1(cache_nsa[:, :, :, :LANE].reshape(n_pool * chunks, CMP_STRIDE * NSA_KV * LANE), w1, LANE)
    ab_pool = ab_pool.reshape(n_pool, chunks, 512)
    qb = pr["qb", BF16].reshape(b, NSA_KV, NSA_HPG, NSA_HD)
    qc = jnp.zeros((b, NSA_KV, 8, LANE), BF16).at[:, :, :NSA_HPG, :NSA_HD].set(qb)
    gdb = jnp.pad(gd_b.reshape(NSA_KV, NSA_HPG, -1), ((0, 0), (0, 8 - NSA_HPG), (0, 0)))
    own = jnp.arange(NSA_KV)[:, None, None]
    n_win = cache_swa.shape[1]
    grp_w = jnp.arange(n_win * NSA_KV)[None, None, :] % NSA_KV
    bw = jnp.where(grp_w == own, jnp.repeat(gdb[:, :, n_win - jnp.arange(n_win)], NSA_KV, axis=2), NEG)
    bw = jnp.concatenate([bw, jnp.broadcast_to(gdb[:, :, 0:1], (NSA_KV, 8, LANE))], axis=2)
    o_c, o_w, idx = _nsa_decode1(page_table, ab_pool, qc, cache_swa.reshape(b, n_win * NSA_KV, LANE),
                                 pr["kvw", F32].reshape(b, NSA_KV, LANE), bw, pe8, w1c, w2,
                                 _ovl_t(past // CMP_STRIDE))
    idx = idx[:, :, 0, :N_SEL - 1]
    rows_blk = SEL_BLOCK * 2 * NSA_KV
    kind = jnp.arange(rows_blk)[None, None, :] % (2 * NSA_KV)
    keep = kind == NSA_KV + own
    bs = jnp.concatenate([
        jnp.where(keep, 0.0, NEG) * jnp.ones((1, 8, 1), F32),
        jnp.repeat(gdb[:, :, SEL_BLOCK - jnp.arange(SEL_BLOCK)], 2 * NSA_KV, axis=2),
        jnp.broadcast_to(gdb[:, :, 0:1], (NSA_KV, 8, LANE))], axis=2)
    sel_new = pr["nsa", F32].reshape(b, NSA_KV, 2, LANE)[:, :, 1:2, :]
    o_s = _nsa_decode2(page_table, idx, nsa_rows, qc, sel_new, bs)
    o_c = o_c[:, :, :NSA_HPG, :NSA_HD].reshape(b, 512)
    o_w = o_w[:, :, :NSA_HPG, NSA_HD:].reshape(b, 512)
    o_s = o_s[:, :, :NSA_HPG, NSA_HD:].reshape(b, 512)
    return o_a, o_c, o_s, o_w


def kernel(x_prompt, x_sample, cache_da_kv, cache_nsa_kv, cache_swa_kv, page_table, c_prompt, c_sample, rel_bias, w_ada, b_ada, w_in, da_lambda, da_subln, cmp_pe, cmp_w1, cmp_w2, w_br_a, w_br_b, w_out, ln1_g, ln1_b, w_up, w_down, ln2_g, ln2_b):
    bp, s, _ = x_prompt.shape
    bs_ = x_sample.shape[0]
    w_perm = _perm_w_in(w_in[0])
    w1, w2 = _cmp_weights(cmp_w1[0], cmp_w2[0])
    pe8 = jnp.broadcast_to(cmp_pe[0].reshape(2, 1, CMP_LEN * NSA_HD), (2, 8, CMP_LEN * NSA_HD))
    w1c = cmp_w1[0].reshape(2, CMP_LEN * NSA_HD, CMP_HID)
    gd_a = _dist_bias(rel_bias[:, :DA_HEADS], 1024)
    gd_b = _dist_bias(rel_bias[:, DA_HEADS:], 1024)
    wa, wb, wo = w_br_a[0].astype(BF16), w_br_b[0].astype(BF16), w_out[0].astype(BF16)
    wu, wd = w_up[0].astype(BF16), w_down[0].astype(BF16)
    sub = da_subln[0].reshape(1, DA_VD)
    eg = _gate_expand()
    g1, b1 = ln1_g[0].reshape(1, D_MODEL), ln1_b[0].reshape(1, D_MODEL)
    g2, b2 = ln2_g[0].reshape(1, D_MODEL), ln2_b[0].reshape(1, D_MODEL)
    lam = da_lambda[0]

    n_c = bp + bs_
    c_all = jnp.pad(jnp.concatenate([c_prompt, c_sample], 0), ((0, (-n_c) % 8), (0, 0)))
    ada = _ada(c_all, w_ada[0], b_ada[0])[:n_c].reshape(n_c, 6, D_MODEL)
    ada_p = jnp.transpose(ada[:bp], (1, 0, 2)).reshape(6, bp, 1, D_MODEL)
    ada_s = jnp.transpose(ada[bp:], (1, 0, 2)).reshape(6, 1, bs_, D_MODEL)

    def tail(mix, pr, x2d, ada_x, tm, tpg):
        o_a, o_c, o_s, o_w = mix
        x1 = _tail1(o_a, o_c, o_s, o_w, pr["gate", F32], pr["ma", F32], pr["mb", F32], x2d, ada_x,
                    wa, wb, wo, sub, eg, g1, b1, tm, tpg)
        return _tail2(x1, ada_x, wu, wd, g2, b2, tm, tpg)

    xp = x_prompt.reshape(bp * s, D_MODEL)
    tm_p = 256
    pr_p = _proj(xp, ada_p, w_perm, tm_p, s // tm_p)
    mix_p = _prompt_mixers(pr_p, bp, s, gd_a, gd_b, lam, w1, w2, pe8, w1c)
    tm_t = 512
    y_p = tail(mix_p, pr_p, xp, ada_p, tm_t, s // tm_t).reshape(bp, s, D_MODEL)
    xs = x_sample.reshape(bs_, D_MODEL)
    pr_s = _proj(xs, ada_s, w_perm, bs_, 1)
    mix_s = _sample_mixers(pr_s, page_table, cache_da_kv[0], cache_nsa_kv[0], cache_swa_kv[0],
                           gd_a, gd_b, lam, w1, w2, pe8, w1c)
    y_s = tail(mix_s, pr_s, xs, ada_s, bs_, 1).reshape(bs_, 1, D_MODEL)

    win = min(WINDOW, s)
    new_da_p = pr_p["da", F32].reshape(1, bp, s, DA_HEADS, 4 * DA_HD)
    new_nsa_p = pr_p["nsa", F32].reshape(1, bp, s, NSA_KV, 4 * NSA_HD)
    new_swa_p = pr_p["kvw", F32].reshape(bp, s, NSA_KV, 2 * NSA_HD)[None, :, s - win:]
    new_da_s = pr_s["da", F32].reshape(1, bs_, 1, DA_HEADS, 4 * DA_HD)
    new_nsa_s = pr_s["nsa", F32].reshape(1, bs_, 1, NSA_KV, 4 * NSA_HD)
    new_swa_s = jnp.concatenate([cache_swa_kv[0][:, 1:], pr_s["kvw", F32].reshape(bs_, 1, NSA_KV, 2 * NSA_HD)],
                                axis=1)[None]
    return (y_p, y_s, new_da_p, new_nsa_p, new_swa_p, new_da_s, new_nsa_s, new_swa_s)
```

```python
import functools
import math

import numpy as np
import jax
import jax.numpy as jnp
from jax import lax
from jax.experimental import pallas as pl
from jax.experimental.pallas import tpu as pltpu

F32 = jnp.float32
BF16 = jnp.bfloat16

D_MODEL = 1024
PAGE = 128
DA_HEADS = 4
DA_HD = 64
DA_VD = 128
NSA_HEADS = 8
NSA_KV = 2
NSA_HPG = 4
NSA_HD = 64
CMP_STRIDE = 16
CMP_LEN = 32
CMP_HID = 64
SEL_BLOCK = 64
N_SEL = 16
WINDOW = 512
D_FF = 4096
N_BUCKETS = 32
MAX_DIST = 128
DEPTH = 1
ALPHA = (2 * DEPTH) ** 0.25
LN_EPS = 1e-5
RMS_EPS = 1e-5
NEG = -1e30
SEL_BIG = 1e9
LAM_INIT = 0.8 - 0.6 * math.exp(-0.3 * 0)
SPLIT_SIZES = (512, 512, 512, 512, 256, 256, 256, 24, 1024, 1024)

LANE = 128
VMEM_LIMIT = 56 * 1024 * 1024
TQ = 128
TKI = 256
TDA = 512
SEL_LANES = 128
REMOVED = -3e38
PP = 8


def _cparams(sem):
    return pltpu.CompilerParams(dimension_semantics=sem, vmem_limit_bytes=VMEM_LIMIT)


def _mm(a, b):
    return jnp.dot(a.astype(BF16), b.astype(BF16), preferred_element_type=F32)


def _mm_nt(a, b):
    return lax.dot_general(a.astype(BF16), b.astype(BF16), (((1,), (1,)), ((), ())),
                           preferred_element_type=F32)


def _mm3(x, w):
    hi = x.astype(BF16)
    r = x - hi.astype(F32)
    mid = r.astype(BF16)
    lo = (r - mid.astype(F32)).astype(BF16)
    return (jnp.dot(hi, w, preferred_element_type=F32) + jnp.dot(mid, w, preferred_element_type=F32)
            + jnp.dot(lo, w, preferred_element_type=F32))


def _masked_softmax(s, valid):
    l = jnp.where(valid, s, NEG)
    m = jnp.max(l, axis=-1, keepdims=True)
    e = jnp.where(valid, jnp.exp(l - m), 0.0)
    return e / jnp.maximum(jnp.sum(e, axis=-1, keepdims=True), 1e-30)


def _online_update(s, valid, v, m_ref, l_ref, a_ref):
    if valid is not None:
        s = jnp.where(valid, s, NEG)
    m_old = m_ref[...]
    m_new = jnp.maximum(m_old, jnp.max(s, axis=-1, keepdims=True))
    p = jnp.exp(s - m_new)
    if valid is not None:
        p = jnp.where(valid, p, 0.0)
    alpha = jnp.exp(m_old - m_new)
    l_ref[...] = alpha * l_ref[...] + jnp.sum(p, axis=-1, keepdims=True)
    a_ref[...] = alpha * a_ref[...] + _mm(p, v)
    m_ref[...] = m_new


def _t5_bucket(dist):
    n = jnp.maximum(dist, 0)
    max_exact = N_BUCKETS // 2
    nf = jnp.maximum(n, 1).astype(F32)
    large = max_exact + (jnp.log(nf / max_exact) / math.log(MAX_DIST / max_exact)
                         * (N_BUCKETS - max_exact)).astype(jnp.int32)
    return jnp.where(n < max_exact, n, jnp.minimum(large, N_BUCKETS - 1))


def _dist_bias(tbl, n):
    d = jnp.arange(n, dtype=jnp.int32)
    g = tbl[_t5_bucket(d)] - tbl[N_BUCKETS - 1][None, :]
    return jnp.transpose(g)


def _toeplitz(gd, offset, rows, cols, below=0.0):
    heads, n = gd.shape
    length = rows + cols - 1
    assert offset + rows <= n
    lo = offset - cols + 1
    hvec = gd[:, max(lo, 0):offset + rows]
    if lo < 0:
        hvec = jnp.concatenate([jnp.full((heads, -lo), below, gd.dtype), hvec], axis=1)
    rev = jnp.concatenate([hvec[:, ::-1], jnp.zeros((heads, 1), gd.dtype)], axis=1)
    flat = jnp.tile(rev, (1, rows))[:, :rows * length].reshape(heads, rows, length)
    return flat[:, :, rows - 1:rows - 1 + cols]


def _diff_lambda(lam_ref):
    l = lam_ref[...]
    a = jnp.sum(l[0:1, :] * l[1:2, :], axis=-1, keepdims=True)
    b = jnp.sum(l[2:3, :] * l[3:4, :], axis=-1, keepdims=True)
    return jnp.exp(a) - jnp.exp(b) + LAM_INIT


def _ada_kernel(c_ref, w_ref, b_ref, o_ref):
    c = c_ref[...]
    o_ref[...] = _mm(c * jax.nn.sigmoid(c), w_ref[...]) + b_ref[...]


def _ada(c, w_ada, b_ada):
    m = c.shape[0]
    n = w_ada.shape[1]
    tn = 512
    return pl.pallas_call(
        _ada_kernel,
        grid=(n // tn,),
        in_specs=[pl.BlockSpec((m, D_MODEL), lambda j: (0, 0)),
                  pl.BlockSpec((D_MODEL, tn), lambda j: (0, j)),
                  pl.BlockSpec((1, tn), lambda j: (0, j))],
        out_specs=pl.BlockSpec((m, tn), lambda j: (0, j)),
        out_shape=jax.ShapeDtypeStruct((m, n), F32),
        compiler_params=_cparams(("arbitrary",)),
        name="ada",
    )(c, w_ada, b_ada.reshape(1, n))


PROJ_GROUPS = (("qa", 512, False, True), ("da", 1024, True, True), ("qb", 512, False, True),
               ("nsa", 512, True, False), ("kvw", 256, True, False), ("gate", 128, True, False),
               ("ma", 1024, True, False), ("mb", 1024, True, False),
               ("ksel", 512, False, True), ("kwin", 512, False, True))
PROJ_OUTS = tuple((n, w, dt) for n, w, f, b in PROJ_GROUPS for dt, on in ((F32, f), (BF16, b)) if on)
PROJ_W = sum(w for _, w, _, _ in PROJ_GROUPS)


def _perm_w_in(w_in):
    parts = jnp.split(w_in, np.cumsum(SPLIT_SIZES)[:-1].tolist(), axis=1)
    qa, ka, va, qn, kvc, kvs, kvw, gb, ma, mb = parts
    da = jnp.concatenate([jnp.concatenate([ka[:, h * 128:(h + 1) * 128], va[:, h * 128:(h + 1) * 128]], 1)
                          for h in range(DA_HEADS)], 1)
    nsa = jnp.concatenate([jnp.concatenate([kvc[:, g * 128:(g + 1) * 128], kvs[:, g * 128:(g + 1) * 128]], 1)
                           for g in range(NSA_KV)], 1)
    gate = jnp.pad(gb, ((0, 0), (0, LANE - gb.shape[1])))

    def dup(kv):
        out = []
        for g in range(NSA_KV):
            k = kv[:, g * 128:g * 128 + 64]
            v = kv[:, g * 128 + 64:g * 128 + 128]
            out += [k, k, v, v]
        return jnp.concatenate(out, 1)

    cols = dict(qa=qa, da=da, qb=qn, nsa=nsa, kvw=kvw, gate=gate, ma=ma, mb=mb, ksel=dup(kvs), kwin=dup(kvw))
    return jnp.concatenate([cols[n] for n, _, _, _ in PROJ_GROUPS], 1).astype(BF16)


def _proj_kernel(x_ref, ada_ref, w_ref, *o_refs):
    h = (x_ref[...] * (1.0 + ada_ref[1]) + ada_ref[0]).astype(BF16)
    off = 0
    k = 0
    for _, width, f32_on, bf_on in PROJ_GROUPS:
        acc = jnp.dot(h, w_ref[:, off:off + width], preferred_element_type=F32)
        if f32_on:
            o_refs[k][...] = acc
            k += 1
        if bf_on:
            o_refs[k][...] = acc.astype(BF16)
            k += 1
        off += width


def _ada_spec(ada, tm, tiles_per_group):
    r = ada.shape[2]
    return pl.BlockSpec((6, None, r, D_MODEL), lambda i, *_: (0, i // tiles_per_group, 0, 0))


def _proj(x2d, ada, w_perm, tm, tiles_per_group):
    m = x2d.shape[0]
    outs = pl.pallas_call(
        _proj_kernel,
        grid=(m // tm,),
        in_specs=[pl.BlockSpec((tm, D_MODEL), lambda i: (i, 0)),
                  _ada_spec(ada, tm, tiles_per_group),
                  pl.BlockSpec((D_MODEL, PROJ_W), lambda i: (0, 0))],
        out_specs=[pl.BlockSpec((tm, w), lambda i: (i, 0)) for _, w, _ in PROJ_OUTS],
        out_shape=[jax.ShapeDtypeStruct((m, w), dt) for _, w, dt in PROJ_OUTS],
        compiler_params=_cparams(("arbitrary",)),
        name="proj",
    )(x2d, ada, w_perm)
    return {(n, dt): o for (n, _, dt), o in zip(PROJ_OUTS, outs)}


def _flash_tile(s, vext, m_ref, acc_ref, valid=None):
    m_cur = jnp.max(s, axis=-1, keepdims=True)
    if valid is not None:
        m_cur = jnp.where(valid, m_cur, NEG)
    m_old = m_ref[...]
    m_new = jnp.maximum(m_old, m_cur)
    p = jnp.exp(s - jnp.concatenate([m_new] * (s.shape[1] // LANE), axis=1))
    alpha = jnp.exp(m_old - m_new)
    acc_ref[...] = (acc_ref[...] * jnp.concatenate([alpha] * (acc_ref.shape[1] // LANE), axis=1)
                    + jnp.dot(p.astype(BF16), vext, preferred_element_type=F32))
    m_ref[...] = m_new


def _flash_init(*refs):
    for m_ref, a_ref in zip(refs[0::2], refs[1::2]):
        m_ref[...] = jnp.full(m_ref.shape, NEG, F32)
        a_ref[...] = jnp.zeros(a_ref.shape, F32)


def _key_rows(ref, j):
    return pl.ds(pl.multiple_of(j * TKI, TKI), TKI)


def _da_kernel(q_ref, kv_ref, tb_ref, lam_ref, o_ref, s_a, s_b, m1, a1, m2, a2):
    qi = pl.program_id(2)
    _flash_init(m1, a1, m2, a2)
    q = q_ref[...] * (DA_HD ** -0.5)
    lane = lax.broadcasted_iota(jnp.int32, q.shape, 1)
    zero = jnp.zeros_like(q)
    q_maps = (jnp.where(lane < DA_HD, q, zero), jnp.where(lane >= DA_HD, q, zero))
    state = ((m1, a1), (m2, a2))
    ones = jnp.ones((TKI, LANE), BF16)

    def scores(j, dst):
        kk = kv_ref[_key_rows(kv_ref, j), 0:128]
        for c in range(2):
            dst[c] = _mm_nt(q_maps[c], kk)

    def consume(src, j, bias=None, valid=None, pen=None):
        vext = jnp.concatenate([kv_ref[_key_rows(kv_ref, j), 128:256], ones], axis=1)
        if valid is not None:
            vext = jnp.where(valid, vext, jnp.zeros_like(vext))
        for c in range(2):
            s = src[c]
            if bias is not None:
                s = s + bias
            if pen is not None:
                s = s + pen
            _flash_tile(s, vext, state[c][0], state[c][1], valid)

    n_far = jnp.maximum(2 * qi - 1, 0)
    scores(0, s_a)

    def pair(i, carry):
        a = 2 * i
        b = jnp.minimum(a + 1, n_far - 1)
        scores(b, s_b)
        consume(s_a, a)
        scores(jnp.minimum(a + 2, n_far - 1), s_a)
        consume(s_b, b, valid=a + 1 < n_far)
        return carry

    lax.fori_loop(0, lax.shift_right_logical(n_far + 1, 1), pair, 0)
    j_d = 2 * qi
    j_s = jnp.maximum(2 * qi - 1, 0)
    scores(j_d, s_a)
    scores(j_s, s_b)
    consume(s_a, j_d, bias=tb_ref[1])
    scores(j_d + 1, s_a)
    consume(s_b, j_s, bias=tb_ref[0], pen=jnp.where(qi > 0, 0.0, NEG))
    consume(s_a, j_d + 1, bias=tb_ref[2])
    lam = _diff_lambda(lam_ref)
    o_ref[...] = (a1[:, 0:DA_VD] / a1[:, DA_VD:2 * DA_VD]
                  - lam * (a2[:, 0:DA_VD] / a2[:, DA_VD:2 * DA_VD]))


def _da_prompt(qa, da_bf, gd_a, da_lambda):
    b, s, _ = qa.shape
    tb = jnp.stack([_toeplitz(gd_a, TKI, TDA, TKI), _toeplitz(gd_a, 0, TDA, TKI, NEG),
                    _toeplitz(gd_a, -TKI, TDA, TKI, NEG)], axis=1)
    return pl.pallas_call(
        _da_kernel,
        grid=(b, DA_HEADS, s // TDA),
        in_specs=[pl.BlockSpec((None, TDA, 128), lambda bi, h, qi: (bi, qi, h)),
                  pl.BlockSpec((None, s, 256), lambda bi, h, qi: (bi, 0, h)),
                  pl.BlockSpec((None, 3, TDA, TKI), lambda bi, h, qi: (h, 0, 0, 0)),
                  pl.BlockSpec((4, DA_HD), lambda bi, h, qi: (0, 0))],
        out_specs=pl.BlockSpec((None, TDA, 128), lambda bi, h, qi: (bi, qi, h)),
        out_shape=jax.ShapeDtypeStruct((b, s, DA_HEADS * DA_VD), F32),
        scratch_shapes=[pltpu.VMEM((2, TDA, TKI), F32), pltpu.VMEM((2, TDA, TKI), F32),
                        pltpu.VMEM((TDA, LANE), F32), pltpu.VMEM((TDA, 2 * DA_VD), F32),
                        pltpu.VMEM((TDA, LANE), F32), pltpu.VMEM((TDA, 2 * DA_VD), F32)],
        compiler_params=_cparams(("arbitrary", "arbitrary", "arbitrary")),
        name="da_prompt",
    )(qa, da_bf, tb, da_lambda)


def _cmp_weights(cmp_w1, cmp_w2):
    w1 = jnp.zeros((CMP_STRIDE, 2, NSA_HD, 2, 2, CMP_HID), F32)
    for half in range(2):
        for c in range(2):
            blk = cmp_w1[c, half * CMP_STRIDE:(half + 1) * CMP_STRIDE]
            w1 = w1.at[:, c, :, half, c, :].set(blk)
    w1 = w1.reshape(CMP_STRIDE * 2 * NSA_HD, 2 * 2 * CMP_HID).astype(BF16)
    w2 = jnp.zeros((2, CMP_HID, 2, 2, NSA_HD), F32)
    for c in range(2):
        for rep in range(2):
            w2 = w2.at[c, :, c, rep, :].set(cmp_w2[c])
    w2 = w2.reshape(2 * CMP_HID, 2 * 2 * NSA_HD).astype(BF16)
    return w1, w2


def _cmp1_kernel(x_ref, w_ref, o_ref, *, pos_lanes, grp_lanes):
    for g in range(NSA_KV):
        xg = jnp.concatenate([x_ref[:, l * pos_lanes + g * grp_lanes:l * pos_lanes + g * grp_lanes + LANE]
                              for l in range(CMP_STRIDE)], axis=1)
        o_ref[:, g * 256:(g + 1) * 256] = _mm(xg, w_ref[...])


def _cmp_stage1(x2d, w1, grp_lanes):
    r, width = x2d.shape
    tm = math.gcd(r, 256)
    return pl.pallas_call(
        functools.partial(_cmp1_kernel, pos_lanes=width // CMP_STRIDE, grp_lanes=grp_lanes),
        grid=(r // tm,),
        in_specs=[pl.BlockSpec((tm, width), lambda i: (i, 0)),
                  pl.BlockSpec(w1.shape, lambda i: (0, 0))],
        out_specs=pl.BlockSpec((tm, 512), lambda i: (i, 0)),
        out_shape=jax.ShapeDtypeStruct((r, 512), F32),
        compiler_params=_cparams(("arbitrary",)),
        name="cmp_stage1",
    )(x2d, w1)


def _cmp_stage2(ab, pe_ref, w1c_ref, w2_ref):
    n = ab.shape[0]
    cst = jnp.concatenate([_mm(pe_ref[c], w1c_ref[c])[0:1, :] for c in range(2)], axis=1)
    outs = []
    for g in range(NSA_KV):
        a = ab[:, g * 256:g * 256 + 128]
        bn = pltpu.roll(ab[:, g * 256 + 128:g * 256 + 256], n - 1, 0)
        hid = jax.nn.gelu(a + bn + cst)
        outs.append(_mm(hid, w2_ref[...]))
    return jnp.concatenate(outs, axis=1)


def _cmp2_kernel(ab_ref, pe_ref, w1c_ref, w2_ref, o_ref):
    o_ref[...] = _cmp_stage2(ab_ref[...], pe_ref, w1c_ref, w2_ref).astype(BF16)


def _cmp_stage2_prompt(ab, pe8, w1c, w2):
    b, n, _ = ab.shape
    return pl.pallas_call(
        _cmp2_kernel,
        grid=(b,),
        in_specs=[pl.BlockSpec((None, n, 512), lambda i: (i, 0, 0)),
                  pl.BlockSpec(pe8.shape, lambda i: (0, 0, 0)),
                  pl.BlockSpec(w1c.shape, lambda i: (0, 0, 0)),
                  pl.BlockSpec(w2.shape, lambda i: (0, 0))],
        out_specs=pl.BlockSpec((None, n, 512), lambda i: (i, 0, 0)),
        out_shape=jax.ShapeDtypeStruct((b, n, 512), BF16),
        compiler_params=_cparams(("arbitrary",)),
        name="cmp_stage2",
    )(ab, pe8, w1c, w2)


def _ovl_t(n_chunks):
    n = np.arange(n_chunks)[:, None]
    j = np.arange(SEL_LANES)[None, :]
    ovl = (n * CMP_STRIDE < j * SEL_BLOCK + SEL_BLOCK) & (j * SEL_BLOCK < n * CMP_STRIDE + CMP_LEN)
    ovl &= n < n_chunks - 1
    return jnp.asarray(ovl.astype(np.float32)).astype(BF16)


def _topk_select(score, n_pick):
    lane = lax.broadcasted_iota(jnp.int32, score.shape, 1).astype(F32)
    sel = jnp.zeros(score.shape, F32)
    picks = jnp.zeros(score.shape, F32)
    sc = score
    for it in range(n_pick):
        m = jnp.max(sc, axis=-1, keepdims=True)
        first = jnp.min(jnp.where(sc == m, lane, float(SEL_LANES)), axis=-1, keepdims=True)
        hit = lane == first
        sel = jnp.where(hit, 1.0, sel)
        picks = jnp.where(lane == float(it), first, picks)
        sc = jnp.where(hit, REMOVED, sc)
    return sel, picks


def _stack_heads(q):
    lane = lax.broadcasted_iota(jnp.int32, (q.shape[0], LANE), 1)
    zero = jnp.zeros((q.shape[0], LANE), q.dtype)
    parts = []
    for hp in range(NSA_HPG):
        blk = q[:, (hp // 2) * LANE:(hp // 2 + 1) * LANE]
        keep = (lane < NSA_HD) if hp % 2 == 0 else (lane >= NSA_HD)
        parts.append(jnp.where(keep, blk, zero))
    return jnp.concatenate(parts, axis=0)


def _unstack_heads(o, tq):
    lane = lax.broadcasted_iota(jnp.int32, (tq, LANE), 1)
    pairs = [jnp.where(lane < NSA_HD, o[(2 * m) * tq:(2 * m + 1) * tq], o[(2 * m + 1) * tq:(2 * m + 2) * tq])
             for m in range(2)]
    return jnp.concatenate(pairs, axis=1)


def _nsa_cmp_kernel(q_ref, kcvc_ref, ovl_ref, oc_ref, sel_ref, *, n_cmp, n_slc):
    qi = pl.program_id(2)
    n_chunks = kcvc_ref.shape[0]
    qs = _stack_heads(q_ref[...])
    kc2 = kcvc_ref[:, 0:128]
    vc2 = kcvc_ref[:, 128:256]
    s = _mm_nt(qs, kc2) * (NSA_HD ** -0.5)
    rows = NSA_HPG * TQ
    t = qi * TQ + (lax.broadcasted_iota(jnp.int32, (rows, n_chunks), 0) & (TQ - 1))
    n = lax.broadcasted_iota(jnp.int32, (rows, n_chunks), 1)
    p = _masked_softmax(s, (n * CMP_STRIDE + (CMP_LEN - 1) <= t) & (n < n_cmp))
    oc_ref[...] = _unstack_heads(_mm(p, vc2), TQ)
    psum = p[0:TQ] + p[TQ:2 * TQ] + p[2 * TQ:3 * TQ] + p[3 * TQ:4 * TQ]
    imp = _mm3(psum, ovl_ref[...])
    tq = qi * TQ + lax.broadcasted_iota(jnp.int32, (TQ, SEL_LANES), 0)
    blk = lax.broadcasted_iota(jnp.int32, (TQ, SEL_LANES), 1)
    cur = lax.shift_right_logical(tq, 6)
    forced = (blk == 0) | (blk == cur) | (blk == cur - 1)
    score = jnp.where(forced, SEL_BIG, jnp.where(blk * SEL_BLOCK <= tq, imp, -SEL_BIG))
    score = jnp.where(blk < n_slc, score, REMOVED)
    sel, _ = _topk_select(score, min(N_SEL, n_slc))
    sel_ref[...] = jnp.where(sel > 0.5, 0.0, NEG).astype(BF16)


def _nsa_cmp_prompt(qb, kcvc, ovl):
    b, s, _ = qb.shape
    n_chunks = kcvc.shape[1]
    kern = functools.partial(_nsa_cmp_kernel, n_cmp=n_chunks - 1, n_slc=s // SEL_BLOCK)
    return pl.pallas_call(
        kern,
        grid=(b, NSA_KV, s // TQ),
        in_specs=[pl.BlockSpec((None, TQ, 256), lambda bi, g, qi: (bi, qi, g)),
                  pl.BlockSpec((None, n_chunks, 256), lambda bi, g, qi: (bi, 0, g)),
                  pl.BlockSpec(ovl.shape, lambda bi, g, qi: (0, 0))],
        out_specs=[pl.BlockSpec((None, TQ, 256), lambda bi, g, qi: (bi, qi, g)),
                   pl.BlockSpec((None, None, TQ, SEL_LANES), lambda bi, g, qi: (bi, g, qi, 0))],
        out_shape=[jax.ShapeDtypeStruct((b, s, 512), F32),
                   jax.ShapeDtypeStruct((b, NSA_KV, s, SEL_LANES), BF16)],
        compiler_params=_cparams(("arbitrary", "arbitrary", "arbitrary")),
        name="nsa_cmp",
    )(qb, kcvc, ovl)


def _nsa_sw_kernel(q_ref, ks_ref, kw_ref, sel_ref, e_ref, tb_ref, os_ref, ow_ref, s_a, s_b, ms, as_, mw, aw):
    qi = pl.program_id(2)
    m = lax.shift_right_logical(qi, 1)
    par = qi & 1
    rows = NSA_HPG * TQ
    _flash_init(ms, as_, mw, aw)
    qs = _stack_heads(q_ref[...] * (NSA_HD ** -0.5))
    selneg = sel_ref[...]
    ones = jnp.ones((TKI, LANE), BF16)

    def scores(ref, j, dst):
        dst[...] = _mm_nt(qs, ref[_key_rows(ref, j), 0:128])

    def sel_bias(j):
        mb = jnp.dot(selneg, e_ref[j], preferred_element_type=F32)
        return jnp.concatenate([mb] * NSA_HPG, axis=0)

    def consume(ref, src, j, m_ref, a_ref, bias, valid=None, pen=None):
        vext = jnp.concatenate([ref[_key_rows(ref, j), 128:256], ones], axis=1)
        if valid is not None:
            vext = jnp.where(valid, vext, jnp.zeros_like(vext))
        s = src[...] + bias
        if pen is not None:
            s = s + pen
        _flash_tile(s, vext, m_ref, a_ref, valid)

    n_far = jnp.maximum(m - 1, 0)
    scores(ks_ref, 0, s_a)

    def pair(i, carry):
        a = 2 * i
        b = jnp.minimum(a + 1, n_far - 1)
        scores(ks_ref, b, s_b)
        consume(ks_ref, s_a, a, ms, as_, sel_bias(a))
        scores(ks_ref, jnp.minimum(a + 2, n_far - 1), s_a)
        consume(ks_ref, s_b, b, ms, as_, sel_bias(b), valid=a + 1 < n_far)
        return carry

    lax.fori_loop(0, lax.shift_right_logical(n_far + 1, 1), pair, 0)
    j1 = jnp.maximum(m - 1, 0)
    j2 = jnp.maximum(m - 2, 0)
    pen1 = jnp.where(m >= 1, 0.0, NEG)
    pen2 = jnp.where(m >= 2, 0.0, NEG)
    t0 = tb_ref[par, 0].reshape(rows, TKI)
    t1 = tb_ref[par, 1].reshape(rows, TKI)
    t2 = tb_ref[par, 2].reshape(rows, TKI)
    scores(ks_ref, m, s_a)
    scores(kw_ref, m, s_b)
    consume(ks_ref, s_a, m, ms, as_, t0 + sel_bias(m))
    scores(ks_ref, j1, s_a)
    consume(kw_ref, s_b, m, mw, aw, t0)
    scores(kw_ref, j1, s_b)
    consume(ks_ref, s_a, j1, ms, as_, t1 + sel_bias(j1), pen=pen1)
    scores(kw_ref, j2, s_a)
    consume(kw_ref, s_b, j1, mw, aw, t1, pen=pen1)
    consume(kw_ref, s_a, j2, mw, aw, t2, pen=pen2)
    os_ref[...] = _unstack_heads(as_[:, 0:LANE] / as_[:, LANE:2 * LANE], TQ)
    ow_ref[...] = _unstack_heads(aw[:, 0:LANE] / aw[:, LANE:2 * LANE], TQ)


def _nsa_sw_prompt(qb, ksel, kwin, sel, gd_b):
    b, s, _ = qb.shape
    rows = NSA_HPG * TQ
    i = np.arange(TQ)[:, None]
    j = np.arange(TKI)[None, :]
    edge = lambda shift: jnp.asarray(np.broadcast_to(np.where(j >= i + shift, 0.0, NEG).astype(np.float32),
                                                     (NSA_HEADS, TQ, TKI)))
    zeros = jnp.zeros((NSA_HEADS, TQ, TKI), F32)
    tb = jnp.stack([jnp.stack([_toeplitz(gd_b, 0, TQ, TKI, NEG), _toeplitz(gd_b, TKI, TQ, TKI), edge(0)]),
                    jnp.stack([_toeplitz(gd_b, TQ, TQ, TKI, NEG), zeros, edge(TQ)])])
    tb = tb.reshape(2, 3, NSA_KV, NSA_HPG, TQ, TKI).transpose(2, 0, 1, 3, 4, 5)
    nk = s // TKI
    e = np.zeros((nk, SEL_LANES, TKI), np.float32)
    for kj in range(nk):
        for k in range(TKI):
            e[kj, kj * (TKI // SEL_BLOCK) + k // SEL_BLOCK, k] = 1.0
    e = jnp.asarray(e).astype(BF16)
    return pl.pallas_call(
        _nsa_sw_kernel,
        grid=(b, NSA_KV, s // TQ),
        in_specs=[pl.BlockSpec((None, TQ, 256), lambda bi, g, qi: (bi, qi, g)),
                  pl.BlockSpec((None, s, 256), lambda bi, g, qi: (bi, 0, g)),
                  pl.BlockSpec((None, s, 256), lambda bi, g, qi: (bi, 0, g)),
                  pl.BlockSpec((None, None, TQ, SEL_LANES), lambda bi, g, qi: (bi, g, qi, 0)),
                  pl.BlockSpec(e.shape, lambda bi, g, qi: (0, 0, 0)),
                  pl.BlockSpec((None, 2, 3, NSA_HPG, TQ, TKI), lambda bi, g, qi: (g, 0, 0, 0, 0, 0))],
        out_specs=[pl.BlockSpec((None, TQ, 256), lambda bi, g, qi: (bi, qi, g)),
                   pl.BlockSpec((None, TQ, 256), lambda bi, g, qi: (bi, qi, g))],
        out_shape=[jax.ShapeDtypeStruct((b, s, 512), F32), jax.ShapeDtypeStruct((b, s, 512), F32)],
        scratch_shapes=[pltpu.VMEM((rows, TKI), F32), pltpu.VMEM((rows, TKI), F32),
                        pltpu.VMEM((rows, LANE), F32), pltpu.VMEM((rows, 2 * LANE), F32),
                        pltpu.VMEM((rows, LANE), F32), pltpu.VMEM((rows, 2 * LANE), F32)],
        compiler_params=_cparams(("arbitrary", "arbitrary", "arbitrary")),
        name="nsa_sel_win",
    )(qb, ksel, kwin, sel, e, tb)


def _layer_norm(x, g, b):
    mu = jnp.mean(x, axis=-1, keepdims=True)
    xc = x - mu
    var = jnp.mean(xc * xc, axis=-1, keepdims=True)
    return xc * lax.rsqrt(var + LN_EPS) * g + b


def _gate_expand():
    e = np.zeros((3, LANE, NSA_HEADS * NSA_HD), np.float32)
    for h in range(NSA_HEADS):
        for j in range(3):
            e[j, h * 3 + j, h * NSA_HD:(h + 1) * NSA_HD] = 1.0
    return jnp.asarray(e).astype(BF16)


def _tail1_kernel(oa_ref, oc_ref, os_ref, ow_ref, gate_ref, ma_ref, mb_ref, x_ref, ada_ref,
                  wa_ref, wb_ref, wo_ref, sub_ref, eg_ref, g1_ref, b1_ref, o_ref):
    oa = oa_ref[...]
    parts = []
    for h in range(DA_HEADS):
        of = oa[:, h * DA_VD:(h + 1) * DA_VD]
        rr = lax.rsqrt(jnp.mean(of * of, axis=-1, keepdims=True) + RMS_EPS)
        parts.append(of * rr * sub_ref[...] * (1.0 - LAM_INIT))
    oan = jnp.concatenate(parts, axis=1)
    sg = jax.nn.sigmoid(gate_ref[...])
    ob = (_mm3(sg, eg_ref[0]) * oc_ref[...] + _mm3(sg, eg_ref[1]) * os_ref[...]
          + _mm3(sg, eg_ref[2]) * ow_ref[...])
    y = (jax.nn.sigmoid(ma_ref[...]) * _mm(oan, wa_ref[...])
         + jax.nn.sigmoid(mb_ref[...]) * _mm(ob, wb_ref[...]))
    z = ALPHA * x_ref[...] + ada_ref[2] * _mm(y, wo_ref[...])
    o_ref[...] = _layer_norm(z, g1_ref[...], b1_ref[...])


def _tail1(oa, oc, os_, ow, gate, ma, mb, x2d, ada, wa, wb, wo, sub, eg, g1, b1, tm, tiles_per_group):
    m = x2d.shape[0]
    row = lambda w: pl.BlockSpec((tm, w), lambda i: (i, 0))
    full = lambda a: pl.BlockSpec(a.shape, lambda i: (0,) * a.ndim)
    return pl.pallas_call(
        _tail1_kernel,
        grid=(m // tm,),
        in_specs=[row(512), row(512), row(512), row(512), row(128), row(1024), row(1024), row(1024),
                  _ada_spec(ada, tm, tiles_per_group),
                  full(wa), full(wb), full(wo), full(sub), full(eg), full(g1), full(b1)],
        out_specs=row(1024),
        out_shape=jax.ShapeDtypeStruct((m, D_MODEL), F32),
        compiler_params=_cparams(("arbitrary",)),
        name="tail_merge",
    )(oa, oc, os_, ow, gate, ma, mb, x2d, ada, wa, wb, wo, sub, eg, g1, b1)


def _tail2_kernel(x_ref, ada_ref, wu_ref, wd_ref, g2_ref, b2_ref, o_ref, h_scr, acc):
    f = pl.program_id(1)

    @pl.when(f == 0)
    def _():
        h_scr[...] = (x_ref[...] * (1.0 + ada_ref[4]) + ada_ref[3]).astype(BF16)
        acc[...] = jnp.zeros(acc.shape, F32)

    u = jnp.maximum(jnp.dot(h_scr[...], wu_ref[...], preferred_element_type=F32), 0.0)
    acc[...] += _mm(u * u, wd_ref[...])

    @pl.when(f == pl.num_programs(1) - 1)
    def _():
        z = ALPHA * x_ref[...] + ada_ref[5] * acc[...]
        o_ref[...] = _layer_norm(z, g2_ref[...], b2_ref[...])


def _tail2(x1, ada, wu, wd, g2, b2, tm, tiles_per_group):
    m = x1.shape[0]
    tf = 1024
    return pl.pallas_call(
        _tail2_kernel,
        grid=(m // tm, D_FF // tf),
        in_specs=[pl.BlockSpec((tm, D_MODEL), lambda i, f: (i, 0)),
                  _ada_spec(ada, tm, tiles_per_group),
                  pl.BlockSpec((D_MODEL, tf), lambda i, f: (0, f)),
                  pl.BlockSpec((tf, D_MODEL), lambda i, f: (f, 0)),
                  pl.BlockSpec((1, D_MODEL), lambda i, f: (0, 0)),
                  pl.BlockSpec((1, D_MODEL), lambda i, f: (0, 0))],
        out_specs=pl.BlockSpec((tm, D_MODEL), lambda i, f: (i, 0)),
        out_shape=jax.ShapeDtypeStruct((m, D_MODEL), F32),
        scratch_shapes=[pltpu.VMEM((tm, D_MODEL), BF16), pltpu.VMEM((tm, D_MODEL), F32)],
        compiler_params=_cparams(("arbitrary", "arbitrary")),
        name="tail_mlp",
    )(x1, ada, wu, wd, g2, b2)


def _da_decode_kernel(pt_ref, *refs):
    pages = refs[:PP]
    q_ref, kn_ref, vn_ref, bl_ref, b0_ref, lam_ref, o_ref, m_ref, l_ref, a_ref = refs[PP:]
    j = pl.program_id(1)
    last = j == pl.num_programs(1) - 1
    scale = DA_HD ** -0.5
    rows_pg = PAGE * 2 * DA_HEADS

    @pl.when(j == 0)
    def _():
        m_ref[...] = jnp.full(m_ref.shape, NEG, F32)
        l_ref[...] = jnp.zeros(l_ref.shape, F32)
        a_ref[...] = jnp.zeros(a_ref.shape, F32)

    q = q_ref[...]
    row = lax.broadcasted_iota(jnp.int32, (8, rows_pg), 0)
    col = lax.broadcasted_iota(jnp.int32, (8, rows_pg), 1)
    cmask = jnp.where((col & 7) == lax.shift_right_logical(row, 1), 0.0, NEG)
    scores = []
    xs = []
    for k in range(PP):
        x = pages[k][...].astype(BF16)
        sc = _mm_nt(q, x) * scale + cmask
        if k == PP - 1:
            sc = sc + jnp.where(last, bl_ref[...], 0.0)
        scores.append(sc)
        xs.append(x)
    s = jnp.concatenate(scores, axis=1)
    m_old = m_ref[...]
    m_new = jnp.maximum(m_old, jnp.max(s, axis=-1, keepdims=True))
    p = jnp.exp(s - m_new)
    alpha = jnp.exp(m_old - m_new)
    l_ref[...] = alpha * l_ref[...] + jnp.sum(p, axis=-1, keepdims=True)
    acc = alpha * a_ref[...]
    for k in range(PP):
        pv = pltpu.roll(p[:, k * rows_pg:(k + 1) * rows_pg], DA_HEADS, 1)
        acc = acc + _mm(pv, xs[k])
    a_ref[...] = acc
    m_ref[...] = m_new

    @pl.when(last)
    def _():
        s_new = jnp.sum(q.astype(F32) * kn_ref[...], axis=-1, keepdims=True) * scale + b0_ref[:, 0:1]
        m_o = m_ref[...]
        m_n = jnp.maximum(m_o, s_new)
        p_new = jnp.exp(s_new - m_n)
        al = jnp.exp(m_o - m_n)
        raw = (al * a_ref[...] + p_new * vn_ref[...]) / (al * l_ref[...] + p_new)
        o_ref[...] = raw - _diff_lambda(lam_ref) * pltpu.roll(raw, 7, 0)


def _da_decode(page_table, cache_rows, q8, k_new, v_new, bl, b0, da_lambda):
    b, n_pages = page_table.shape
    rows_pg = PAGE * 2 * DA_HEADS
    page_spec = lambda k: pl.BlockSpec((rows_pg, LANE), lambda bi, j, pt: (pt[bi, j * PP + k], 0))
    per_row = lambda: pl.BlockSpec((None, 8, LANE), lambda bi, j, pt: (bi, 0, 0))
    grid_spec = pltpu.PrefetchScalarGridSpec(
        num_scalar_prefetch=1,
        grid=(b, n_pages // PP),
        in_specs=[page_spec(k) for k in range(PP)] + [
            per_row(), per_row(), per_row(),
            pl.BlockSpec((8, rows_pg), lambda bi, j, pt: (0, 0)),
            pl.BlockSpec((8, LANE), lambda bi, j, pt: (0, 0)),
            pl.BlockSpec((4, DA_HD), lambda bi, j, pt: (0, 0))],
        out_specs=per_row(),
        scratch_shapes=[pltpu.VMEM((8, 1), F32), pltpu.VMEM((8, 1), F32), pltpu.VMEM((8, LANE), F32)])
    return pl.pallas_call(
        _da_decode_kernel,
        grid_spec=grid_spec,
        out_shape=jax.ShapeDtypeStruct((b, 8, LANE), F32),
        compiler_params=_cparams(("arbitrary", "arbitrary")),
        name="da_decode",
    )(page_table, *([cache_rows] * PP), q8, k_new, v_new, bl, b0, da_lambda)


def _nsa_decode1_kernel(pt_ref, ab_hbm, q_ref, swa_ref, new_ref, bw_ref, pe_ref, w1c_ref, w2_ref, ovl_ref,
                        oc_ref, ow_ref, idx_ref, abuf, sem, *, n_pages):
    b = pl.program_id(0)
    copies = [pltpu.make_async_copy(ab_hbm.at[pt_ref[b, p]], abuf.at[p], sem) for p in range(n_pages)]
    for c in copies:
        c.start()
    for c in copies:
        c.wait()
    n_chunks = n_pages * (PAGE // CMP_STRIDE)
    kcvc = _cmp_stage2(abuf[...].reshape(n_chunks, 512), pe_ref, w1c_ref, w2_ref)
    scale = NSA_HD ** -0.5
    swa = swa_ref[...].astype(BF16)
    n_win = swa.shape[0]
    for g in range(NSA_KV):
        q = q_ref[g]
        s = _mm_nt(q, kcvc[:, g * 256:g * 256 + 128]) * scale
        n = lax.broadcasted_iota(jnp.int32, s.shape, 1)
        p = _masked_softmax(s, n < n_chunks - 1)
        oc_ref[g] = _mm(p, kcvc[:, g * 256 + 128:g * 256 + 256])
        psum = jnp.sum(p[0:NSA_HPG], axis=0, keepdims=True)
        imp = _mm3(jnp.broadcast_to(psum, (8, n_chunks)), ovl_ref[...])
        blk = lax.broadcasted_iota(jnp.int32, imp.shape, 1)
        n_blk = n_chunks * CMP_STRIDE // SEL_BLOCK
        forced = (blk == 0) | (blk == n_blk - 1)
        score = jnp.where(blk < n_blk, jnp.where(forced, SEL_BIG, imp), REMOVED)
        _, picks = _topk_select(score, N_SEL - 1)
        idx_ref[g] = picks.astype(jnp.int32)
        new = new_ref[g:g + 1, :]
        sw = _mm_nt(q, swa) * scale + bw_ref[g][:, 0:n_win]
        s_new = (jnp.sum(q.astype(F32) * new, axis=-1, keepdims=True) * scale
                 + bw_ref[g][:, n_win:n_win + 1])
        m = jnp.maximum(jnp.max(sw, axis=-1, keepdims=True), s_new)
        e = jnp.exp(sw - m)
        e_new = jnp.exp(s_new - m)
        den = jnp.sum(e, axis=-1, keepdims=True) + e_new
        ow_ref[g] = (_mm(e, swa) + e_new * new) / den


def _nsa_decode1(page_table, ab_pool, qc, cache_swa, kvw_new, bw, pe8, w1c, w2, ovl):
    b, n_pages = page_table.shape
    full = lambda a: pl.BlockSpec(a.shape, lambda bi, pt: (0,) * a.ndim)
    out4 = lambda: pl.BlockSpec((None, NSA_KV, 8, LANE), lambda bi, pt: (bi, 0, 0, 0))
    grid_spec = pltpu.PrefetchScalarGridSpec(
        num_scalar_prefetch=1,
        grid=(b,),
        in_specs=[pl.BlockSpec(memory_space=pl.ANY),
                  pl.BlockSpec((None, NSA_KV, 8, LANE), lambda bi, pt: (bi, 0, 0, 0)),
                  pl.BlockSpec((None, cache_swa.shape[1], LANE), lambda bi, pt: (bi, 0, 0)),
                  pl.BlockSpec((None, NSA_KV, LANE), lambda bi, pt: (bi, 0, 0)),
                  full(bw), full(pe8), full(w1c), full(w2), full(ovl)],
        out_specs=[out4(), out4(), out4()],
        scratch_shapes=[pltpu.VMEM((n_pages, PAGE // CMP_STRIDE, 512), F32), pltpu.SemaphoreType.DMA(())])
    return pl.pallas_call(
        functools.partial(_nsa_decode1_kernel, n_pages=n_pages),
        grid_spec=grid_spec,
        out_shape=[jax.ShapeDtypeStruct((b, NSA_KV, 8, LANE), F32),
                   jax.ShapeDtypeStruct((b, NSA_KV, 8, LANE), F32),
                   jax.ShapeDtypeStruct((b, NSA_KV, 8, LANE), jnp.int32)],
        compiler_params=_cparams(("arbitrary",)),
        name="nsa_decode_cmp_win",
    )(page_table, ab_pool, qc, cache_swa, kvw_new, bw, pe8, w1c, w2, ovl)


def _nsa_decode2_kernel(pt_ref, idx_ref, blk_ref, q_ref, new_ref, bs_ref, o_ref, m_ref, l_ref, a_ref, *, n_blk):
    b = pl.program_id(0)
    g = pl.program_id(1)
    i = pl.program_id(2)
    scale = NSA_HD ** -0.5

    @pl.when(i == 0)
    def _():
        m_ref[...] = jnp.full(m_ref.shape, NEG, F32)
        l_ref[...] = jnp.zeros(l_ref.shape, F32)
        a_ref[...] = jnp.zeros(a_ref.shape, F32)

    q = q_ref[...]
    x = blk_ref[...].astype(BF16)
    near = idx_ref[b, g, i] == n_blk - 1
    cols = x.shape[0]
    s = _mm_nt(q, x) * scale + bs_ref[:, 0:cols] + jnp.where(near, bs_ref[:, cols:2 * cols], 0.0)
    _online_update(s, None, x, m_ref, l_ref, a_ref)

    @pl.when(i == pl.num_programs(2) - 1)
    def _():
        new = new_ref[...]
        s_new = (jnp.sum(q.astype(F32) * new, axis=-1, keepdims=True) * scale
                 + bs_ref[:, 2 * cols:2 * cols + 1])
        m_o = m_ref[...]
        m_n = jnp.maximum(m_o, s_new)
        p_new = jnp.exp(s_new - m_n)
        al = jnp.exp(m_o - m_n)
        o_ref[...] = (al * a_ref[...] + p_new * new) / (al * l_ref[...] + p_new)


def _nsa_decode2(page_table, idx, cache_rows, qc, nsa_new, bs):
    b = page_table.shape[0]
    n_pick = idx.shape[2]
    rows_blk = SEL_BLOCK * 2 * NSA_KV

    def blk_map(bi, g, i, pt, ix):
        blk = ix[bi, g, i]
        return (pt[bi, lax.shift_right_logical(blk, 1)] * 2 + (blk & 1), 0)

    grid_spec = pltpu.PrefetchScalarGridSpec(
        num_scalar_prefetch=2,
        grid=(b, NSA_KV, n_pick),
        in_specs=[pl.BlockSpec((rows_blk, LANE), blk_map),
                  pl.BlockSpec((None, None, 8, LANE), lambda bi, g, i, pt, ix: (bi, g, 0, 0)),
                  pl.BlockSpec((None, None, 1, LANE), lambda bi, g, i, pt, ix: (bi, g, 0, 0)),
                  pl.BlockSpec((None, 8, bs.shape[2]), lambda bi, g, i, pt, ix: (g, 0, 0))],
        out_specs=pl.BlockSpec((None, None, 8, LANE), lambda bi, g, i, pt, ix: (bi, g, 0, 0)),
        scratch_shapes=[pltpu.VMEM((8, 1), F32), pltpu.VMEM((8, 1), F32), pltpu.VMEM((8, LANE), F32)])
    return pl.pallas_call(
        functools.partial(_nsa_decode2_kernel, n_blk=page_table.shape[1] * PAGE // SEL_BLOCK),
        grid_spec=grid_spec,
        out_shape=jax.ShapeDtypeStruct((b, NSA_KV, 8, LANE), F32),
        compiler_params=_cparams(("arbitrary", "arbitrary", "arbitrary")),
        name="nsa_decode_sel",
    )(page_table, idx, cache_rows, qc, nsa_new, bs)


def _prompt_mixers(pr, b, s, gd_a, gd_b, da_lambda, w1, w2, pe8, w1c):
    sh = lambda a: a.reshape(b, s, a.shape[-1])
    o_a = _da_prompt(sh(pr["qa", BF16]), sh(pr["da", BF16]), gd_a, da_lambda)
    n_chunks = s // CMP_STRIDE
    ab = _cmp_stage1(pr["nsa", F32].reshape(b * n_chunks, CMP_STRIDE * 512), w1, 2 * LANE)
    kcvc = _cmp_stage2_prompt(ab.reshape(b, n_chunks, 512), pe8, w1c, w2)
    o_c, sel = _nsa_cmp_prompt(sh(pr["qb", BF16]), kcvc, _ovl_t(n_chunks))
    o_s, o_w = _nsa_sw_prompt(sh(pr["qb", BF16]), sh(pr["ksel", BF16]), sh(pr["kwin", BF16]), sel, gd_b)
    flat = lambda a: a.reshape(b * s, a.shape[-1])
    return flat(o_a), flat(o_c), flat(o_s), flat(o_w)


def _sample_mixers(pr, page_table, cache_da, cache_nsa, cache_swa, gd_a, gd_b, da_lambda, w1, w2, pe8, w1c):
    b, n_pages = page_table.shape
    past = n_pages * PAGE
    n_pool = cache_da.shape[0]
    da_rows = cache_da.reshape(n_pool, PAGE, DA_HEADS, 2, LANE).transpose(0, 1, 3, 2, 4).reshape(-1, LANE)
    nsa_rows = cache_nsa.reshape(n_pool, PAGE, NSA_KV, 2, LANE).transpose(0, 1, 3, 2, 4).reshape(-1, LANE)
    qa = pr["qa", BF16].reshape(b, DA_HEADS, 2, DA_HD)
    q8 = jnp.zeros((b, DA_HEADS, 2, 2, DA_HD), BF16)
    for c in range(2):
        q8 = q8.at[:, :, c, c, :].set(qa[:, :, c])
    q8 = q8.reshape(b, 8, LANE)
    da_new = pr["da", F32].reshape(b, DA_HEADS, 2, LANE)
    k_new = jnp.repeat(da_new[:, :, 0], 2, axis=1)
    v_new = jnp.repeat(da_new[:, :, 1], 2, axis=1)
    gda8 = jnp.repeat(gd_a, 2, axis=0)
    bl = jnp.repeat(gda8[:, PAGE - jnp.arange(PAGE)], 2 * DA_HEADS, axis=1)
    b0 = jnp.broadcast_to(gda8[:, 0:1], (8, LANE))
    o_a = _da_decode(page_table, da_rows, q8, k_new, v_new, bl, b0, da_lambda)
    o_a = o_a[:, 0::2, :].reshape(b, 512)
    chunks = PAGE // CMP_STRIDE
    ab_pool = _cmp_stage1(cache_nsa[:, :, :, :LANE].reshape(n_pool * chunks, CMP_STRIDE * NSA_KV * LANE), w1, LANE)
    ab_pool = ab_pool.reshape(n_pool, chunks, 512)
    qb = pr["qb", BF16].reshape(b, NSA_KV, NSA_HPG, NSA_HD)
    qc = jnp.zeros((b, NSA_KV, 8, LANE), BF16).at[:, :, :NSA_HPG, :NSA_HD].set(qb)
    gdb = jnp.pad(gd_b.reshape(NSA_KV, NSA_HPG, -1), ((0, 0), (0, 8 - NSA_HPG), (0, 0)))
    own = jnp.arange(NSA_KV)[:, None, None]
    n_win = cache_swa.shape[1]
    grp_w = jnp.arange(n_win * NSA_KV)[None, None, :] % NSA_KV
    bw = jnp.where(grp_w == own, jnp.repeat(gdb[:, :, n_win - jnp.arange(n_win)], NSA_KV, axis=2), NEG)
    bw = jnp.concatenate([bw, jnp.broadcast_to(gdb[:, :, 0:1], (NSA_KV, 8, LANE))], axis=2)
    o_c, o_w, idx = _nsa_decode1(page_table, ab_pool, qc, cache_swa.reshape(b, n_win * NSA_KV, LANE),
                                 pr["kvw", F32].reshape(b, NSA_KV, LANE), bw, pe8, w1c, w2,
                                 _ovl_t(past // CMP_STRIDE))
    idx = idx[:, :, 0, :N_SEL - 1]
    rows_blk = SEL_BLOCK * 2 * NSA_KV
    kind = jnp.arange(rows_blk)[None, None, :] % (2 * NSA_KV)
    keep = kind == NSA_KV + own
    bs = jnp.concatenate([
        jnp.where(keep, 0.0, NEG) * jnp.ones((1, 8, 1), F32),
        jnp.repeat(gdb[:, :, SEL_BLOCK - jnp.arange(SEL_BLOCK)], 2 * NSA_KV, axis=2),
        jnp.broadcast_to(gdb[:, :, 0:1], (NSA_KV, 8, LANE))], axis=2)
    sel_new = pr["nsa", F32].reshape(b, NSA_KV, 2, LANE)[:, :, 1:2, :]
    o_s = _nsa_decode2(page_table, idx, nsa_rows, qc, sel_new, bs)
    o_c = o_c[:, :, :NSA_HPG, :NSA_HD].reshape(b, 512)
    o_w = o_w[:, :, :NSA_HPG, NSA_HD:].reshape(b, 512)
    o_s = o_s[:, :, :NSA_HPG, NSA_HD:].reshape(b, 512)
    return o_a, o_c, o_s, o_w


def kernel(x_prompt, x_sample, cache_da_kv, cache_nsa_kv, cache_swa_kv, page_table, c_prompt, c_sample, rel_bias, w_ada, b_ada, w_in, da_lambda, da_subln, cmp_pe, cmp_w1, cmp_w2, w_br_a, w_br_b, w_out, ln1_g, ln1_b, w_up, w_down, ln2_g, ln2_b):
    bp, s, _ = x_prompt.shape
    bs_ = x_sample.shape[0]
    w_perm = _perm_w_in(w_in[0])
    w1, w2 = _cmp_weights(cmp_w1[0], cmp_w2[0])
    pe8 = jnp.broadcast_to(cmp_pe[0].reshape(2, 1, CMP_LEN * NSA_HD), (2, 8, CMP_LEN * NSA_HD))
    w1c = cmp_w1[0].reshape(2, CMP_LEN * NSA_HD, CMP_HID)
    gd_a = _dist_bias(rel_bias[:, :DA_HEADS], 1024)
    gd_b = _dist_bias(rel_bias[:, DA_HEADS:], 1024)
    wa, wb, wo = w_br_a[0].astype(BF16), w_br_b[0].astype(BF16), w_out[0].astype(BF16)
    wu, wd = w_up[0].astype(BF16), w_down[0].astype(BF16)
    sub = da_subln[0].reshape(1, DA_VD)
    eg = _gate_expand()
    g1, b1 = ln1_g[0].reshape(1, D_MODEL), ln1_b[0].reshape(1, D_MODEL)
    g2, b2 = ln2_g[0].reshape(1, D_MODEL), ln2_b[0].reshape(1, D_MODEL)
    lam = da_lambda[0]

    n_c = bp + bs_
    c_all = jnp.pad(jnp.concatenate([c_prompt, c_sample], 0), ((0, (-n_c) % 8), (0, 0)))
    ada = _ada(c_all, w_ada[0], b_ada[0])[:n_c].reshape(n_c, 6, D_MODEL)
    ada_p = jnp.transpose(ada[:bp], (1, 0, 2)).reshape(6, bp, 1, D_MODEL)
    ada_s = jnp.transpose(ada[bp:], (1, 0, 2)).reshape(6, 1, bs_, D_MODEL)

    def tail(mix, pr, x2d, ada_x, tm, tpg):
        o_a, o_c, o_s, o_w = mix
        x1 = _tail1(o_a, o_c, o_s, o_w, pr["gate", F32], pr["ma", F32], pr["mb", F32], x2d, ada_x,
                    wa, wb, wo, sub, eg, g1, b1, tm, tpg)
        return _tail2(x1, ada_x, wu, wd, g2, b2, tm, tpg)

    xp = x_prompt.reshape(bp * s, D_MODEL)
    tm_p = 256
    pr_p = _proj(xp, ada_p, w_perm, tm_p, s // tm_p)
    mix_p = _prompt_mixers(pr_p, bp, s, gd_a, gd_b, lam, w1, w2, pe8, w1c)
    tm_t = 512
    y_p = tail(mix_p, pr_p, xp, ada_p, tm_t, s // tm_t).reshape(bp, s, D_MODEL)
    xs = x_sample.reshape(bs_, D_MODEL)
    pr_s = _proj(xs, ada_s, w_perm, bs_, 1)
    mix_s = _sample_mixers(pr_s, page_table, cache_da_kv[0], cache_nsa_kv[0], cache_swa_kv[0],
                           gd_a, gd_b, lam, w1, w2, pe8, w1c)
    y_s = tail(mix_s, pr_s, xs, ada_s, bs_, 1).reshape(bs_, 1, D_MODEL)

    win = min(WINDOW, s)
    new_da_p = pr_p["da", F32].reshape(1, bp, s, DA_HEADS, 4 * DA_HD)
    new_nsa_p = pr_p["nsa", F32].reshape(1, bp, s, NSA_KV, 4 * NSA_HD)
    new_swa_p = pr_p["kvw", F32].reshape(bp, s, NSA_KV, 2 * NSA_HD)[None, :, s - win:]
    new_da_s = pr_s["da", F32].reshape(1, bs_, 1, DA_HEADS, 4 * DA_HD)
    new_nsa_s = pr_s["nsa", F32].reshape(1, bs_, 1, NSA_KV, 4 * NSA_HD)
    new_swa_s = jnp.concatenate([cache_swa_kv[0][:, 1:], pr_s["kvw", F32].reshape(bs_, 1, NSA_KV, 2 * NSA_HD)],
                                axis=1)[None]
    return (y_p, y_s, new_da_p, new_nsa_p, new_swa_p, new_da_s, new_nsa_s, new_swa_s)
```

```python
import functools
import math

import numpy as np
import jax
import jax.numpy as jnp
from jax import lax
from jax.experimental import pallas as pl
from jax.experimental.pallas import tpu as pltpu

F32 = jnp.float32
BF16 = jnp.bfloat16

D_MODEL = 1024
PAGE = 128
DA_HEADS = 4
DA_HD = 64
DA_VD = 128
NSA_HEADS = 8
NSA_KV = 2
NSA_HPG = 4
NSA_HD = 64
CMP_STRIDE = 16
CMP_LEN = 32
CMP_HID = 64
SEL_BLOCK = 64
N_SEL = 16
WINDOW = 512
D_FF = 4096
N_BUCKETS = 32
MAX_DIST = 128
DEPTH = 1
ALPHA = (2 * DEPTH) ** 0.25
LN_EPS = 1e-5
RMS_EPS = 1e-5
NEG = -1e30
SEL_BIG = 1e9
LAM_INIT = 0.8 - 0.6 * math.exp(-0.3 * 0)
SPLIT_SIZES = (512, 512, 512, 512, 256, 256, 256, 24, 1024, 1024)

LANE = 128
VMEM_LIMIT = 56 * 1024 * 1024
TQ = 128
TKI = 256
ONES_ROWS = 16
TDA = 512
SEL_LANES = 128
REMOVED = -3e38
PP = 8


def _cparams(sem):
    return pltpu.CompilerParams(dimension_semantics=sem, vmem_limit_bytes=VMEM_LIMIT)


def _mm(a, b):
    return jnp.dot(a.astype(BF16), b.astype(BF16), preferred_element_type=F32)


def _mm_nt(a, b):
    return lax.dot_general(a.astype(BF16), b.astype(BF16), (((1,), (1,)), ((), ())),
                           preferred_element_type=F32)


def _mm3(x, w):
    hi = x.astype(BF16)
    r = x - hi.astype(F32)
    mid = r.astype(BF16)
    lo = (r - mid.astype(F32)).astype(BF16)
    return (jnp.dot(hi, w, preferred_element_type=F32) + jnp.dot(mid, w, preferred_element_type=F32)
            + jnp.dot(lo, w, preferred_element_type=F32))


def _masked_softmax(s, valid):
    l = jnp.where(valid, s, NEG)
    m = jnp.max(l, axis=-1, keepdims=True)
    e = jnp.where(valid, jnp.exp(l - m), 0.0)
    return e / jnp.maximum(jnp.sum(e, axis=-1, keepdims=True), 1e-30)


def _t5_bucket(dist):
    n = jnp.maximum(dist, 0)
    max_exact = N_BUCKETS // 2
    nf = jnp.maximum(n, 1).astype(F32)
    large = max_exact + (jnp.log(nf / max_exact) / math.log(MAX_DIST / max_exact)
                         * (N_BUCKETS - max_exact)).astype(jnp.int32)
    return jnp.where(n < max_exact, n, jnp.minimum(large, N_BUCKETS - 1))


def _dist_bias(tbl, n):
    d = jnp.arange(n, dtype=jnp.int32)
    g = tbl[_t5_bucket(d)] - tbl[N_BUCKETS - 1][None, :]
    return jnp.transpose(g)


def _toeplitz(gd, offset, rows, cols, below=0.0):
    heads, n = gd.shape
    length = rows + cols - 1
    assert offset + rows <= n
    lo = offset - cols + 1
    hvec = gd[:, max(lo, 0):offset + rows]
    if lo < 0:
        hvec = jnp.concatenate([jnp.full((heads, -lo), below, gd.dtype), hvec], axis=1)
    rev = jnp.concatenate([hvec[:, ::-1], jnp.zeros((heads, 1), gd.dtype)], axis=1)
    flat = jnp.tile(rev, (1, rows))[:, :rows * length].reshape(heads, rows, length)
    return flat[:, :, rows - 1:rows - 1 + cols]


def _diff_lambda(lam_ref):
    l = lam_ref[...]
    a = jnp.sum(l[0:1, :] * l[1:2, :], axis=-1, keepdims=True)
    b = jnp.sum(l[2:3, :] * l[3:4, :], axis=-1, keepdims=True)
    return jnp.exp(a) - jnp.exp(b) + LAM_INIT


def _ada_kernel(c_ref, w_ref, b_ref, o_ref):
    c = c_ref[...]
    o_ref[...] = _mm(c * jax.nn.sigmoid(c), w_ref[...]) + b_ref[...]


def _ada(c, w_ada, b_ada):
    m = c.shape[0]
    n = w_ada.shape[1]
    tn = 512
    return pl.pallas_call(
        _ada_kernel,
        grid=(n // tn,),
        in_specs=[pl.BlockSpec((m, D_MODEL), lambda j: (0, 0)),
                  pl.BlockSpec((D_MODEL, tn), lambda j: (0, j)),
                  pl.BlockSpec((1, tn), lambda j: (0, j))],
        out_specs=pl.BlockSpec((m, tn), lambda j: (0, j)),
        out_shape=jax.ShapeDtypeStruct((m, n), F32),
        compiler_params=_cparams(("arbitrary",)),
        name="ada",
    )(c, w_ada, b_ada.reshape(1, n))


PROJ_GROUPS = (("qa", 512, BF16), ("da", 1024, F32), ("qb", 512, BF16), ("nsa", 512, F32), ("kvw", 256, F32),
               ("gate", 128, F32), ("ma", 1024, F32), ("mb", 1024, F32))
PROJ_KEY_GROUPS = (("ksel", 256), ("kwin", 256))
PROJ_W = sum(w for _, w, _ in PROJ_GROUPS)
PROJ_W_KV = PROJ_W + sum(w for _, w in PROJ_KEY_GROUPS)
PROJ_VT = (("davt", DA_HEADS * DA_VD), ("selvt", NSA_KV * NSA_HD), ("winvt", NSA_KV * NSA_HD))


def _perm_w_in(w_in):
    parts = jnp.split(w_in, np.cumsum(SPLIT_SIZES)[:-1].tolist(), axis=1)
    qa, ka, va, qn, kvc, kvs, kvw, gb, ma, mb = parts
    da = jnp.concatenate([jnp.concatenate([ka[:, h * 128:(h + 1) * 128], va[:, h * 128:(h + 1) * 128]], 1)
                          for h in range(DA_HEADS)], 1)
    nsa = jnp.concatenate([jnp.concatenate([kvc[:, g * 128:(g + 1) * 128], kvs[:, g * 128:(g + 1) * 128]], 1)
                           for g in range(NSA_KV)], 1)
    gate = jnp.pad(gb, ((0, 0), (0, LANE - gb.shape[1])))
    keys2 = lambda kv: jnp.concatenate([kv[:, g * 128:g * 128 + 64] for g in range(NSA_KV) for _ in range(2)], 1)
    vals = lambda kv: jnp.concatenate([kv[:, g * 128 + 64:(g + 1) * 128] for g in range(NSA_KV)], 1)
    cols = dict(qa=qa, da=da, qb=qn, nsa=nsa, kvw=kvw, gate=gate, ma=ma, mb=mb, ksel=keys2(kvs), kwin=keys2(kvw))
    names = [n for n, _, _ in PROJ_GROUPS] + [n for n, _ in PROJ_KEY_GROUPS]
    w_perm = jnp.concatenate([cols[n] for n in names], 1).astype(BF16)
    w_vt = jnp.transpose(jnp.concatenate([va, vals(kvs), vals(kvw)], 1)).astype(BF16)
    return w_perm, w_vt


def _proj_kernel(x_ref, ada_ref, w_ref, *refs, with_kv):
    h = (x_ref[...] * (1.0 + ada_ref[1]) + ada_ref[0]).astype(BF16)
    o_refs = refs[1:] if with_kv else refs
    off = 0
    k = 0
    for name, width, dt in PROJ_GROUPS:
        acc = jnp.dot(h, w_ref[:, off:off + width], preferred_element_type=F32)
        o_refs[k][...] = acc.astype(dt)
        k += 1
        if with_kv and name == "da":
            o_refs[k][...] = jnp.concatenate([acc[:, hd * 256:hd * 256 + 128] for hd in range(DA_HEADS)],
                                             axis=1).astype(BF16)
            k += 1
        off += width
    if with_kv:
        for _, width in PROJ_KEY_GROUPS:
            o_refs[k][...] = jnp.dot(h, w_ref[:, off:off + width], preferred_element_type=F32).astype(BF16)
            k += 1
            off += width
        vt = _mm_nt(refs[0][...], h)
        r0 = 0
        for _, rows in PROJ_VT:
            o_refs[k][...] = vt[r0:r0 + rows].astype(BF16)
            k += 1
            r0 += rows


def _ada_spec(ada, tm, tiles_per_group):
    r = ada.shape[2]
    return pl.BlockSpec((6, None, r, D_MODEL), lambda i, *_: (0, i // tiles_per_group, 0, 0))


def _proj(x2d, ada, w_perm, w_vt, tm, tiles_per_group):
    m = x2d.shape[0]
    with_kv = w_vt is not None
    outs = []
    for name, w, dt in PROJ_GROUPS:
        outs.append((name, (m, w), (tm, w), dt))
        if with_kv and name == "da":
            outs.append(("dak", (m, 512), (tm, 512), BF16))
    in_specs = [pl.BlockSpec((tm, D_MODEL), lambda i: (i, 0)),
                _ada_spec(ada, tm, tiles_per_group),
                pl.BlockSpec((D_MODEL, PROJ_W_KV if with_kv else PROJ_W), lambda i: (0, 0))]
    args = [x2d, ada, w_perm]
    if with_kv:
        assert tm == TKI
        outs += [(n, (m, w), (tm, w), BF16) for n, w in PROJ_KEY_GROUPS]
        outs += [(n, (m // tm, r, tm), (None, r, tm), BF16) for n, r in PROJ_VT]
        in_specs.append(pl.BlockSpec(w_vt.shape, lambda i: (0, 0)))
        args.append(w_vt)
    res = pl.pallas_call(
        functools.partial(_proj_kernel, with_kv=with_kv),
        grid=(m // tm,),
        in_specs=in_specs,
        out_specs=[pl.BlockSpec(blk, (lambda i: (i, 0)) if len(blk) == 2 else (lambda i: (i, 0, 0)))
                   for _, _, blk, _ in outs],
        out_shape=[jax.ShapeDtypeStruct(shape, dt) for _, shape, _, dt in outs],
        compiler_params=_cparams(("arbitrary",)),
        name="proj",
    )(*args)
    return {n: o for (n, _, _, _), o in zip(outs, res)}


class _Chain:
    def __init__(self, s_bufs, p_bufs, al_bufs, m_ref, acc_ref, score_fn, vext_fn, bias_fn):
        self.s, self.p, self.al = s_bufs, p_bufs, al_bufs
        self.m, self.acc = m_ref, acc_ref
        self.score_fn, self.vext_fn, self.bias_fn = score_fn, vext_fn, bias_fn
        self.maps = m_ref.shape[0]
        self.cur = 0

    def prime(self, j):
        self.cur = 0
        self.m[...] = jnp.full(self.m.shape, NEG, F32)
        self.acc[...] = jnp.zeros(self.acc.shape, F32)
        self.p[1][...] = jnp.zeros(self.p[1].shape, BF16)
        self.al[1][...] = jnp.ones(self.al[1].shape, F32)
        self.score_fn(j, self.s[0])

    def _finish(self, k, buf, pv):
        self.acc[k] = self.acc[k] * self.al[buf][k, 0:1, :] + pv

    def step(self, j_prev, j_next, table=None, valid=None, pen=None):
        c, o = self.cur, 1 - self.cur
        vext = self.vext_fn(j_prev)
        pv = [jnp.dot(vext, self.p[o][k], preferred_element_type=F32) for k in range(self.maps)]
        if j_next is not None:
            self.score_fn(j_next, self.s[o])
        for k in range(self.maps):
            for l0 in range(0, self.m.shape[2], LANE):
                cols = slice(l0, l0 + LANE)
                s = self.s[c][k, :, cols]
                if table is not None:
                    s = s + self.bias_fn(table, l0)
                if pen is not None:
                    s = s + pen
                m_cur = jnp.max(s, axis=0, keepdims=True)
                if valid is not None:
                    m_cur = jnp.where(valid, m_cur, NEG)
                m_old = self.m[k, 0:1, cols]
                m_new = jnp.maximum(m_old, m_cur)
                p = jnp.exp(s - m_new)
                if valid is not None:
                    p = jnp.where(valid, p, 0.0)
                self.p[c][k, :, cols] = p.astype(BF16)
                self.al[c][k, :, cols] = jnp.broadcast_to(jnp.exp(m_old - m_new), (8, LANE))
                self.m[k, :, cols] = jnp.broadcast_to(m_new, (8, LANE))
            self._finish(k, o, pv[k])
        self.cur = o

    def step_eager(self, j_cur, j_next, table=None, valid=None, pen=None):
        c, o = self.cur, 1 - self.cur
        if j_next is not None:
            self.score_fn(j_next, self.s[o])
        vext = self.vext_fn(j_cur)
        for k in range(self.maps):
            ps, alphas = [], []
            for l0 in range(0, self.m.shape[2], LANE):
                cols = slice(l0, l0 + LANE)
                s = self.s[c][k, :, cols]
                if table is not None:
                    s = s + self.bias_fn(table, l0)
                if pen is not None:
                    s = s + pen
                m_cur = jnp.max(s, axis=0, keepdims=True)
                if valid is not None:
                    m_cur = jnp.where(valid, m_cur, NEG)
                m_old = self.m[k, 0:1, cols]
                m_new = jnp.maximum(m_old, m_cur)
                p = jnp.exp(s - m_new)
                if valid is not None:
                    p = jnp.where(valid, p, 0.0)
                ps.append(p.astype(BF16))
                alphas.append(jnp.exp(m_old - m_new))
                self.m[k, :, cols] = jnp.broadcast_to(m_new, (8, LANE))
            self.acc[k] = (self.acc[k] * jnp.concatenate(alphas, axis=1)
                           + jnp.dot(vext, jnp.concatenate(ps, axis=1), preferred_element_type=F32))
        self.cur = o

    def flush(self, j_prev):
        o = 1 - self.cur
        vext = self.vext_fn(j_prev)
        for k in range(self.maps):
            self._finish(k, o, jnp.dot(vext, self.p[o][k], preferred_element_type=F32))


def _chain_scratch(maps, acc_rows, queries):
    return [pltpu.VMEM((maps, TKI, queries), F32), pltpu.VMEM((maps, TKI, queries), F32),
            pltpu.VMEM((maps, TKI, queries), BF16), pltpu.VMEM((maps, TKI, queries), BF16),
            pltpu.VMEM((maps, 8, queries), F32), pltpu.VMEM((maps, 8, queries), F32),
            pltpu.VMEM((maps, 8, queries), F32), pltpu.VMEM((maps, acc_rows, queries), F32)]


def _key_rows(ref, j):
    return pl.ds(pl.multiple_of(j * TKI, TKI), TKI)


def _da_kernel(q_ref, k_ref, vt_ref, tb_ref, lam_ref, o_ref, s_a, s_b, p_a, p_b, al_a, al_b, m_ref, acc_ref):
    qi = pl.program_id(2)
    q = q_ref[...] * (DA_HD ** -0.5)
    lane = lax.broadcasted_iota(jnp.int32, q.shape, 1)
    zero = jnp.zeros_like(q)
    q_maps = (jnp.where(lane < DA_HD, q, zero), jnp.where(lane >= DA_HD, q, zero))
    ones = jnp.ones((ONES_ROWS, TKI), BF16)

    def scores(j, dst):
        kk = k_ref[_key_rows(k_ref, j), :]
        for c in range(2):
            dst[c] = _mm_nt(kk, q_maps[c])

    chain = _Chain((s_a, s_b), (p_a, p_b), (al_a, al_b), m_ref, acc_ref, scores,
                   lambda j: jnp.concatenate([vt_ref[j], ones], axis=0),
                   lambda table, l0: tb_ref[table, :, l0:l0 + LANE])
    n_far = jnp.maximum(2 * qi - 1, 0)
    j_d = 2 * qi
    j_s = jnp.maximum(2 * qi - 1, 0)
    chain.prime(jnp.where(n_far > 0, 0, j_d))

    def pair(i, carry):
        a = 2 * i
        chain.step_eager(a, jnp.where(a + 1 < n_far, a + 1, j_d))
        chain.step_eager(jnp.minimum(a + 1, n_far - 1), jnp.where(a + 2 < n_far, a + 2, j_d), valid=a + 1 < n_far)
        return carry

    lax.fori_loop(0, lax.shift_right_logical(n_far + 1, 1), pair, 0)
    chain.step_eager(j_d, j_s, table=1)
    chain.step_eager(j_s, j_d + 1, table=0, pen=jnp.where(qi > 0, 0.0, NEG))
    chain.step_eager(j_d + 1, None, table=2)
    lam = _diff_lambda(lam_ref)
    o_t = (acc_ref[0, 0:DA_VD, :] / acc_ref[0, DA_VD:DA_VD + 1, :]
           - lam * (acc_ref[1, 0:DA_VD, :] / acc_ref[1, DA_VD:DA_VD + 1, :]))
    o_ref[...] = o_t.T


def _da_prompt(qa, dak, davt, gd_a, da_lambda):
    b, s, _ = qa.shape
    tb = jnp.stack([_toeplitz(gd_a, TKI, TDA, TKI), _toeplitz(gd_a, 0, TDA, TKI, NEG),
                    _toeplitz(gd_a, -TKI, TDA, TKI, NEG)], axis=1)
    tb = jnp.swapaxes(tb, 2, 3)
    acc_rows = DA_VD + ONES_ROWS
    return pl.pallas_call(
        _da_kernel,
        grid=(b, DA_HEADS, s // TDA),
        in_specs=[pl.BlockSpec((None, TDA, 128), lambda bi, h, qi: (bi, qi, h)),
                  pl.BlockSpec((None, s, 128), lambda bi, h, qi: (bi, 0, h)),
                  pl.BlockSpec((None, s // TKI, DA_VD, TKI), lambda bi, h, qi: (bi, 0, h, 0)),
                  pl.BlockSpec((None, 3, TKI, TDA), lambda bi, h, qi: (h, 0, 0, 0)),
                  pl.BlockSpec((4, DA_HD), lambda bi, h, qi: (0, 0))],
        out_specs=pl.BlockSpec((None, TDA, 128), lambda bi, h, qi: (bi, qi, h)),
        out_shape=jax.ShapeDtypeStruct((b, s, DA_HEADS * DA_VD), F32),
        scratch_shapes=_chain_scratch(2, acc_rows, TDA),
        compiler_params=_cparams(("arbitrary", "arbitrary", "arbitrary")),
        name="da_prompt",
    )(qa, dak, davt, tb, da_lambda)


def _cmp_weights(cmp_w1, cmp_w2):
    w1 = jnp.zeros((CMP_STRIDE, 2, NSA_HD, 2, 2, CMP_HID), F32)
    for half in range(2):
        for c in range(2):
            blk = cmp_w1[c, half * CMP_STRIDE:(half + 1) * CMP_STRIDE]
            w1 = w1.at[:, c, :, half, c, :].set(blk)
    w1 = w1.reshape(CMP_STRIDE * 2 * NSA_HD, 2 * 2 * CMP_HID).astype(BF16)
    w2 = jnp.zeros((2, CMP_HID, 2, 2, NSA_HD), F32)
    for c in range(2):
        for rep in range(2):
            w2 = w2.at[c, :, c, rep, :].set(cmp_w2[c])
    w2 = w2.reshape(2 * CMP_HID, 2 * 2 * NSA_HD).astype(BF16)
    return w1, w2


def _cmp1_kernel(x_ref, w_ref, o_ref, *, pos_lanes, grp_lanes):
    for g in range(NSA_KV):
        xg = jnp.concatenate([x_ref[:, l * pos_lanes + g * grp_lanes:l * pos_lanes + g * grp_lanes + LANE]
                              for l in range(CMP_STRIDE)], axis=1)
        o_ref[:, g * 256:(g + 1) * 256] = _mm(xg, w_ref[...])


def _cmp_stage1(x2d, w1, grp_lanes):
    r, width = x2d.shape
    tm = math.gcd(r, 256)
    return pl.pallas_call(
        functools.partial(_cmp1_kernel, pos_lanes=width // CMP_STRIDE, grp_lanes=grp_lanes),
        grid=(r // tm,),
        in_specs=[pl.BlockSpec((tm, width), lambda i: (i, 0)),
                  pl.BlockSpec(w1.shape, lambda i: (0, 0))],
        out_specs=pl.BlockSpec((tm, 512), lambda i: (i, 0)),
        out_shape=jax.ShapeDtypeStruct((r, 512), F32),
        compiler_params=_cparams(("arbitrary",)),
        name="cmp_stage1",
    )(x2d, w1)


def _cmp1_rows_kernel(x_ref, w_ref, o_ref, *, tm):
    rows_pos = 2 * NSA_KV
    for g in range(NSA_KV):
        xg = jnp.concatenate([x_ref[pl.ds(l * rows_pos + g, tm, stride=CMP_STRIDE * rows_pos), :]
                              for l in range(CMP_STRIDE)], axis=1)
        o_ref[:, g * 256:(g + 1) * 256] = _mm(xg, w_ref[...])


def _cmp_stage1_rows(rows2d, w1):
    rows_chunk = CMP_STRIDE * 2 * NSA_KV
    r = rows2d.shape[0] // rows_chunk
    tm = math.gcd(r, 128)
    return pl.pallas_call(
        functools.partial(_cmp1_rows_kernel, tm=tm),
        grid=(r // tm,),
        in_specs=[pl.BlockSpec((tm * rows_chunk, LANE), lambda i: (i, 0)),
                  pl.BlockSpec(w1.shape, lambda i: (0, 0))],
        out_specs=pl.BlockSpec((tm, 512), lambda i: (i, 0)),
        out_shape=jax.ShapeDtypeStruct((r, 512), F32),
        compiler_params=_cparams(("arbitrary",)),
        name="cmp_stage1_rows",
    )(rows2d, w1)


def _cmp_stage2(ab, pe_ref, w1c_ref, w2_ref):
    n = ab.shape[0]
    cst = jnp.concatenate([_mm(pe_ref[c], w1c_ref[c])[0:1, :] for c in range(2)], axis=1)
    outs = []
    for g in range(NSA_KV):
        a = ab[:, g * 256:g * 256 + 128]
        bn = pltpu.roll(ab[:, g * 256 + 128:g * 256 + 256], n - 1, 0)
        hid = jax.nn.gelu(a + bn + cst)
        outs.append(_mm(hid, w2_ref[...]))
    return jnp.concatenate(outs, axis=1)


def _cmp2_kernel(ab_ref, pe_ref, w1c_ref, w2_ref, o_ref):
    o_ref[...] = _cmp_stage2(ab_ref[...], pe_ref, w1c_ref, w2_ref).astype(BF16)


def _cmp_stage2_prompt(ab, pe8, w1c, w2):
    b, n, _ = ab.shape
    return pl.pallas_call(
        _cmp2_kernel,
        grid=(b,),
        in_specs=[pl.BlockSpec((None, n, 512), lambda i: (i, 0, 0)),
                  pl.BlockSpec(pe8.shape, lambda i: (0, 0, 0)),
                  pl.BlockSpec(w1c.shape, lambda i: (0, 0, 0)),
                  pl.BlockSpec(w2.shape, lambda i: (0, 0))],
        out_specs=pl.BlockSpec((None, n, 512), lambda i: (i, 0, 0)),
        out_shape=jax.ShapeDtypeStruct((b, n, 512), BF16),
        compiler_params=_cparams(("arbitrary",)),
        name="cmp_stage2",
    )(ab, pe8, w1c, w2)


def _ovl_t(n_chunks):
    n = np.arange(n_chunks)[:, None]
    j = np.arange(SEL_LANES)[None, :]
    ovl = (n * CMP_STRIDE < j * SEL_BLOCK + SEL_BLOCK) & (j * SEL_BLOCK < n * CMP_STRIDE + CMP_LEN)
    ovl &= n < n_chunks - 1
    return jnp.asarray(ovl.astype(np.float32)).astype(BF16)


def _topk_select(score, n_pick):
    lane = lax.broadcasted_iota(jnp.int32, score.shape, 1).astype(F32)
    sel = jnp.zeros(score.shape, F32)
    picks = jnp.zeros(score.shape, F32)
    sc = score
    for it in range(n_pick):
        m = jnp.max(sc, axis=-1, keepdims=True)
        first = jnp.min(jnp.where(sc == m, lane, float(SEL_LANES)), axis=-1, keepdims=True)
        hit = lane == first
        sel = jnp.where(hit, 1.0, sel)
        picks = jnp.where(lane == float(it), first, picks)
        sc = jnp.where(hit, REMOVED, sc)
    return sel, picks


def _topk_mask_cols(score_t, n_pick):
    blk = lax.broadcasted_iota(jnp.int32, score_t.shape, 0).astype(F32)
    sel = jnp.zeros(score_t.shape, F32)
    sc = score_t
    for _ in range(n_pick):
        m = jnp.max(sc, axis=0, keepdims=True)
        first = jnp.min(jnp.where(sc == m, blk, float(SEL_LANES)), axis=0, keepdims=True)
        hit = blk == first
        sel = jnp.where(hit, 1.0, sel)
        sc = jnp.where(hit, REMOVED, sc)
    return sel


def _stack_heads(q):
    lane = lax.broadcasted_iota(jnp.int32, (q.shape[0], LANE), 1)
    zero = jnp.zeros((q.shape[0], LANE), q.dtype)
    parts = []
    for hp in range(NSA_HPG):
        blk = q[:, (hp // 2) * LANE:(hp // 2 + 1) * LANE]
        keep = (lane < NSA_HD) if hp % 2 == 0 else (lane >= NSA_HD)
        parts.append(jnp.where(keep, blk, zero))
    return jnp.concatenate(parts, axis=0)


def _unstack_heads(o, tq):
    lane = lax.broadcasted_iota(jnp.int32, (tq, LANE), 1)
    pairs = [jnp.where(lane < NSA_HD, o[(2 * m) * tq:(2 * m + 1) * tq], o[(2 * m + 1) * tq:(2 * m + 2) * tq])
             for m in range(2)]
    return jnp.concatenate(pairs, axis=1)


def _nsa_cmp_kernel(q_ref, kcvc_ref, ovl_ref, oc_ref, sel_ref, *, n_cmp, n_slc):
    qi = pl.program_id(2)
    n_chunks = kcvc_ref.shape[0]
    qs = _stack_heads(q_ref[...])
    kc2 = kcvc_ref[:, 0:128]
    vc2 = kcvc_ref[:, 128:256]
    s = _mm_nt(qs, kc2) * (NSA_HD ** -0.5)
    rows = NSA_HPG * TQ
    t = qi * TQ + (lax.broadcasted_iota(jnp.int32, (rows, n_chunks), 0) & (TQ - 1))
    n = lax.broadcasted_iota(jnp.int32, (rows, n_chunks), 1)
    p = _masked_softmax(s, (n * CMP_STRIDE + (CMP_LEN - 1) <= t) & (n < n_cmp))
    oc_ref[...] = _unstack_heads(_mm(p, vc2), TQ)
    psum = p[0:TQ] + p[TQ:2 * TQ] + p[2 * TQ:3 * TQ] + p[3 * TQ:4 * TQ]
    imp = _mm3(psum, ovl_ref[...])
    tq = qi * TQ + lax.broadcasted_iota(jnp.int32, (TQ, SEL_LANES), 0)
    blk = lax.broadcasted_iota(jnp.int32, (TQ, SEL_LANES), 1)
    cur = lax.shift_right_logical(tq, 6)
    forced = (blk == 0) | (blk == cur) | (blk == cur - 1)
    score = jnp.where(forced, SEL_BIG, jnp.where(blk * SEL_BLOCK <= tq, imp, -SEL_BIG))
    score = jnp.where(blk < n_slc, score, REMOVED)
    sel_t = _topk_mask_cols(score.T, min(N_SEL, n_slc))
    sel_ref[...] = jnp.where(sel_t > 0.5, 0.0, NEG).astype(BF16)


def _nsa_cmp_prompt(qb, kcvc, ovl):
    b, s, _ = qb.shape
    n_chunks = kcvc.shape[1]
    kern = functools.partial(_nsa_cmp_kernel, n_cmp=n_chunks - 1, n_slc=s // SEL_BLOCK)
    return pl.pallas_call(
        kern,
        grid=(b, NSA_KV, s // TQ),
        in_specs=[pl.BlockSpec((None, TQ, 256), lambda bi, g, qi: (bi, qi, g)),
                  pl.BlockSpec((None, n_chunks, 256), lambda bi, g, qi: (bi, 0, g)),
                  pl.BlockSpec(ovl.shape, lambda bi, g, qi: (0, 0))],
        out_specs=[pl.BlockSpec((None, TQ, 256), lambda bi, g, qi: (bi, qi, g)),
                   pl.BlockSpec((None, None, TQ, SEL_LANES), lambda bi, g, qi: (bi, g, qi, 0))],
        out_shape=[jax.ShapeDtypeStruct((b, s, 512), F32),
                   jax.ShapeDtypeStruct((b, NSA_KV, s, SEL_LANES), BF16)],
        compiler_params=_cparams(("arbitrary", "arbitrary", "arbitrary")),
        name="nsa_cmp",
    )(qb, kcvc, ovl)


def _nsa_sw_kernel(q_ref, ks_ref, kw_ref, vs_ref, vw_ref, sel_ref, e_ref, tb_ref, os_ref, ow_ref, *scratch):
    qi = pl.program_id(2)
    m = lax.shift_right_logical(qi, 1)
    par = qi & 1
    qs = _stack_heads(q_ref[...] * (NSA_HD ** -0.5))
    selneg = sel_ref[...]
    ones = jnp.ones((ONES_ROWS, TKI), BF16)

    def sel_scores(j, dst):
        mask = jnp.dot(e_ref[j], selneg, preferred_element_type=F32)
        dst[0] = _mm_nt(ks_ref[_key_rows(ks_ref, j), :], qs) + jnp.concatenate([mask] * NSA_HPG, axis=1)

    def win_scores(j, dst):
        dst[0] = _mm_nt(kw_ref[_key_rows(kw_ref, j), :], qs)

    bias = lambda table, l0: tb_ref[par, table, :, l0:l0 + LANE]
    sel = _Chain(scratch[0:2], scratch[2:4], scratch[4:6], scratch[6], scratch[7], sel_scores,
                 lambda j: jnp.concatenate([vs_ref[j], ones], axis=0), bias)
    win = _Chain(scratch[8:10], scratch[10:12], scratch[12:14], scratch[14], scratch[15], win_scores,
                 lambda j: jnp.concatenate([vw_ref[j], ones], axis=0), bias)

    def heads_out(acc_ref):
        o_t = acc_ref[0, 0:NSA_HD, :] / acc_ref[0, NSA_HD:NSA_HD + 1, :]
        pairs = [jnp.concatenate([o_t[:, (2 * k) * TQ:(2 * k + 1) * TQ], o_t[:, (2 * k + 1) * TQ:(2 * k + 2) * TQ]],
                                 axis=0).T for k in range(NSA_HPG // 2)]
        return jnp.concatenate(pairs, axis=1)

    n_far = jnp.maximum(m - 1, 0)
    j1 = jnp.maximum(m - 1, 0)
    j2 = jnp.maximum(m - 2, 0)
    pen1 = jnp.where(m >= 1, 0.0, NEG)
    pen2 = jnp.where(m >= 2, 0.0, NEG)
    sel.prime(jnp.where(n_far > 0, 0, m))
    win.prime(m)

    def pair(i, carry):
        a = 2 * i
        sel.step(jnp.maximum(a - 1, 0), jnp.where(a + 1 < n_far, a + 1, m))
        sel.step(a, jnp.where(a + 2 < n_far, a + 2, m), valid=a + 1 < n_far)
        return carry

    lax.fori_loop(0, lax.shift_right_logical(n_far + 1, 1), pair, 0)
    sel.step(jnp.maximum(n_far - 1, 0), j1, table=0)
    win.step(m, j1, table=0)
    sel.step(m, None, table=1, pen=pen1)
    win.step(m, j2, table=1, pen=pen1)
    sel.flush(j1)
    win.step(j1, None, table=2, pen=pen2)
    win.flush(j2)
    os_ref[...] = heads_out(sel.acc)
    ow_ref[...] = heads_out(win.acc)


def _nsa_sw_prompt(qb, ksel, kwin, selvt, winvt, sel, gd_b):
    b, s, _ = qb.shape
    cols = NSA_HPG * TQ
    i = np.arange(TQ)[:, None]
    j = np.arange(TKI)[None, :]
    edge = lambda shift: jnp.asarray(np.broadcast_to(np.where(j >= i + shift, 0.0, NEG).astype(np.float32),
                                                     (NSA_HEADS, TQ, TKI)))
    zeros = jnp.zeros((NSA_HEADS, TQ, TKI), F32)
    tb = jnp.stack([jnp.stack([_toeplitz(gd_b, 0, TQ, TKI, NEG), _toeplitz(gd_b, TKI, TQ, TKI), edge(0)]),
                    jnp.stack([_toeplitz(gd_b, TQ, TQ, TKI, NEG), zeros, edge(TQ)])])
    tb = tb.reshape(2, 3, NSA_KV, NSA_HPG, TQ, TKI).transpose(2, 0, 1, 5, 3, 4).reshape(NSA_KV, 2, 3, TKI, cols)
    nk = s // TKI
    e = np.zeros((nk, TKI, SEL_LANES), np.float32)
    for kj in range(nk):
        for k in range(TKI):
            e[kj, k, kj * (TKI // SEL_BLOCK) + k // SEL_BLOCK] = 1.0
    e = jnp.asarray(e).astype(BF16)
    acc_rows = NSA_HD + ONES_ROWS
    return pl.pallas_call(
        _nsa_sw_kernel,
        grid=(b, NSA_KV, s // TQ),
        in_specs=[pl.BlockSpec((None, TQ, 256), lambda bi, g, qi: (bi, qi, g)),
                  pl.BlockSpec((None, s, LANE), lambda bi, g, qi: (bi, 0, g)),
                  pl.BlockSpec((None, s, LANE), lambda bi, g, qi: (bi, 0, g)),
                  pl.BlockSpec((None, nk, NSA_HD, TKI), lambda bi, g, qi: (bi, 0, g, 0)),
                  pl.BlockSpec((None, nk, NSA_HD, TKI), lambda bi, g, qi: (bi, 0, g, 0)),
                  pl.BlockSpec((None, None, TQ, SEL_LANES), lambda bi, g, qi: (bi, g, qi, 0)),
                  pl.BlockSpec(e.shape, lambda bi, g, qi: (0, 0, 0)),
                  pl.BlockSpec((None, 2, 3, TKI, cols), lambda bi, g, qi: (g, 0, 0, 0, 0))],
        out_specs=[pl.BlockSpec((None, TQ, 256), lambda bi, g, qi: (bi, qi, g)),
                   pl.BlockSpec((None, TQ, 256), lambda bi, g, qi: (bi, qi, g))],
        out_shape=[jax.ShapeDtypeStruct((b, s, 512), F32), jax.ShapeDtypeStruct((b, s, 512), F32)],
        scratch_shapes=_chain_scratch(1, acc_rows, cols) + _chain_scratch(1, acc_rows, cols),
        compiler_params=_cparams(("arbitrary", "arbitrary", "arbitrary")),
        name="nsa_sel_win",
    )(qb, ksel, kwin, selvt, winvt, sel, e, tb)


def _layer_norm(x, g, b):
    mu = jnp.mean(x, axis=-1, keepdims=True)
    xc = x - mu
    var = jnp.mean(xc * xc, axis=-1, keepdims=True)
    return xc * lax.rsqrt(var + LN_EPS) * g + b


def _gate_expand():
    e = np.zeros((3, LANE, NSA_HEADS * NSA_HD), np.float32)
    for h in range(NSA_HEADS):
        for j in range(3):
            e[j, h * 3 + j, h * NSA_HD:(h + 1) * NSA_HD] = 1.0
    return jnp.asarray(e).astype(BF16)


def _tail1_kernel(oa_ref, oc_ref, os_ref, ow_ref, gate_ref, ma_ref, mb_ref, x_ref, ada_ref,
                  wa_ref, wb_ref, wo_ref, sub_ref, eg_ref, g1_ref, b1_ref, o_ref):
    oa = oa_ref[...]
    parts = []
    for h in range(DA_HEADS):
        of = oa[:, h * DA_VD:(h + 1) * DA_VD]
        rr = lax.rsqrt(jnp.mean(of * of, axis=-1, keepdims=True) + RMS_EPS)
        parts.append(of * rr * sub_ref[...] * (1.0 - LAM_INIT))
    oan = jnp.concatenate(parts, axis=1)
    sg = jax.nn.sigmoid(gate_ref[...])
    ob = (_mm3(sg, eg_ref[0]) * oc_ref[...] + _mm3(sg, eg_ref[1]) * os_ref[...]
          + _mm3(sg, eg_ref[2]) * ow_ref[...])
    y = (jax.nn.sigmoid(ma_ref[...]) * _mm(oan, wa_ref[...])
         + jax.nn.sigmoid(mb_ref[...]) * _mm(ob, wb_ref[...]))
    z = ALPHA * x_ref[...] + ada_ref[2] * _mm(y, wo_ref[...])
    o_ref[...] = _layer_norm(z, g1_ref[...], b1_ref[...])


def _tail1(oa, oc, os_, ow, gate, ma, mb, x2d, ada, wa, wb, wo, sub, eg, g1, b1, tm, tiles_per_group):
    m = x2d.shape[0]
    row = lambda w: pl.BlockSpec((tm, w), lambda i: (i, 0))
    full = lambda a: pl.BlockSpec(a.shape, lambda i: (0,) * a.ndim)
    return pl.pallas_call(
        _tail1_kernel,
        grid=(m // tm,),
        in_specs=[row(512), row(512), row(512), row(512), row(128), row(1024), row(1024), row(1024),
                  _ada_spec(ada, tm, tiles_per_group),
                  full(wa), full(wb), full(wo), full(sub), full(eg), full(g1), full(b1)],
        out_specs=row(1024),
        out_shape=jax.ShapeDtypeStruct((m, D_MODEL), F32),
        compiler_params=_cparams(("arbitrary",)),
        name="tail_merge",
    )(oa, oc, os_, ow, gate, ma, mb, x2d, ada, wa, wb, wo, sub, eg, g1, b1)


def _tail2_kernel(x_ref, ada_ref, wu_ref, wd_ref, g2_ref, b2_ref, o_ref, h_scr, acc):
    f = pl.program_id(1)

    @pl.when(f == 0)
    def _():
        h_scr[...] = (x_ref[...] * (1.0 + ada_ref[4]) + ada_ref[3]).astype(BF16)
        acc[...] = jnp.zeros(acc.shape, F32)

    u = jnp.maximum(jnp.dot(h_scr[...], wu_ref[...], preferred_element_type=F32), 0.0)
    acc[...] += _mm(u * u, wd_ref[...])

    @pl.when(f == pl.num_programs(1) - 1)
    def _():
        z = ALPHA * x_ref[...] + ada_ref[5] * acc[...]
        o_ref[...] = _layer_norm(z, g2_ref[...], b2_ref[...])


def _tail2(x1, ada, wu, wd, g2, b2, tm, tiles_per_group):
    m = x1.shape[0]
    tf = 1024
    return pl.pallas_call(
        _tail2_kernel,
        grid=(m // tm, D_FF // tf),
        in_specs=[pl.BlockSpec((tm, D_MODEL), lambda i, f: (i, 0)),
                  _ada_spec(ada, tm, tiles_per_group),
                  pl.BlockSpec((D_MODEL, tf), lambda i, f: (0, f)),
                  pl.BlockSpec((tf, D_MODEL), lambda i, f: (f, 0)),
                  pl.BlockSpec((1, D_MODEL), lambda i, f: (0, 0)),
                  pl.BlockSpec((1, D_MODEL), lambda i, f: (0, 0))],
        out_specs=pl.BlockSpec((tm, D_MODEL), lambda i, f: (i, 0)),
        out_shape=jax.ShapeDtypeStruct((m, D_MODEL), F32),
        scratch_shapes=[pltpu.VMEM((tm, D_MODEL), BF16), pltpu.VMEM((tm, D_MODEL), F32)],
        compiler_params=_cparams(("arbitrary", "arbitrary")),
        name="tail_mlp",
    )(x1, ada, wu, wd, g2, b2)


def _da_decode_kernel(pt_ref, *refs):
    pages = refs[:PP]
    q_ref, kn_ref, vn_ref, bl_ref, b0_ref, lam_ref, o_ref, m_ref, l_ref, a_ref = refs[PP:]
    j = pl.program_id(1)
    last = j == pl.num_programs(1) - 1
    scale = DA_HD ** -0.5
    rows_pg = PAGE * 2 * DA_HEADS

    @pl.when(j == 0)
    def _():
        m_ref[...] = jnp.full(m_ref.shape, NEG, F32)
        l_ref[...] = jnp.zeros(l_ref.shape, F32)
        a_ref[...] = jnp.zeros(a_ref.shape, F32)

    q = q_ref[...]
    row = lax.broadcasted_iota(jnp.int32, (8, rows_pg), 0)
    col = lax.broadcasted_iota(jnp.int32, (8, rows_pg), 1)
    cmask = jnp.where((col & 7) == lax.shift_right_logical(row, 1), 0.0, NEG)
    scores = []
    xs = []
    for k in range(PP):
        x = pages[k][...].astype(BF16)
        sc = _mm_nt(q, x) * scale + cmask
        if k == PP - 1:
            sc = sc + jnp.where(last, bl_ref[...], 0.0)
        scores.append(sc)
        xs.append(x)
    s = jnp.concatenate(scores, axis=1)
    m_old = m_ref[...]
    m_new = jnp.maximum(m_old, jnp.max(s, axis=-1, keepdims=True))
    p = jnp.exp(s - m_new)
    alpha = jnp.exp(m_old - m_new)
    l_ref[...] = alpha * l_ref[...] + jnp.sum(p, axis=-1, keepdims=True)
    acc = alpha * a_ref[...]
    for k in range(PP):
        pv = pltpu.roll(p[:, k * rows_pg:(k + 1) * rows_pg], DA_HEADS, 1)
        acc = acc + _mm(pv, xs[k])
    a_ref[...] = acc
    m_ref[...] = m_new

    @pl.when(last)
    def _():
        s_new = jnp.sum(q.astype(F32) * kn_ref[...], axis=-1, keepdims=True) * scale + b0_ref[:, 0:1]
        m_o = m_ref[...]
        m_n = jnp.maximum(m_o, s_new)
        p_new = jnp.exp(s_new - m_n)
        al = jnp.exp(m_o - m_n)
        raw = (al * a_ref[...] + p_new * vn_ref[...]) / (al * l_ref[...] + p_new)
        o_ref[...] = raw - _diff_lambda(lam_ref) * pltpu.roll(raw, 7, 0)


def _da_decode(page_table, cache_rows, q8, k_new, v_new, bl, b0, da_lambda):
    b, n_pages = page_table.shape
    rows_pg = PAGE * 2 * DA_HEADS
    page_spec = lambda k: pl.BlockSpec((rows_pg, LANE), lambda bi, j, pt: (pt[bi, j * PP + k], 0))
    per_row = lambda: pl.BlockSpec((None, 8, LANE), lambda bi, j, pt: (bi, 0, 0))
    grid_spec = pltpu.PrefetchScalarGridSpec(
        num_scalar_prefetch=1,
        grid=(b, n_pages // PP),
        in_specs=[page_spec(k) for k in range(PP)] + [
            per_row(), per_row(), per_row(),
            pl.BlockSpec((8, rows_pg), lambda bi, j, pt: (0, 0)),
            pl.BlockSpec((8, LANE), lambda bi, j, pt: (0, 0)),
            pl.BlockSpec((4, DA_HD), lambda bi, j, pt: (0, 0))],
        out_specs=per_row(),
        scratch_shapes=[pltpu.VMEM((8, 1), F32), pltpu.VMEM((8, 1), F32), pltpu.VMEM((8, LANE), F32)])
    return pl.pallas_call(
        _da_decode_kernel,
        grid_spec=grid_spec,
        out_shape=jax.ShapeDtypeStruct((b, 8, LANE), F32),
        compiler_params=_cparams(("arbitrary", "arbitrary")),
        name="da_decode",
    )(page_table, *([cache_rows] * PP), q8, k_new, v_new, bl, b0, da_lambda)


def _nsa_decode1_kernel(pt_ref, ab_hbm, q_ref, swa_ref, new_ref, bw_ref, pe_ref, w1c_ref, w2_ref, ovl_ref,
                        oc_ref, ow_ref, idx_ref, abuf, sem, *, n_pages):
    b = pl.program_id(0)
    copies = [pltpu.make_async_copy(ab_hbm.at[pt_ref[b, p]], abuf.at[p], sem) for p in range(n_pages)]
    for c in copies:
        c.start()
    for c in copies:
        c.wait()
    n_chunks = n_pages * (PAGE // CMP_STRIDE)
    kcvc = _cmp_stage2(abuf[...].reshape(n_chunks, 512), pe_ref, w1c_ref, w2_ref)
    scale = NSA_HD ** -0.5
    swa = swa_ref[...].astype(BF16)
    n_win = swa.shape[0]
    for g in range(NSA_KV):
        q = q_ref[g]
        s = _mm_nt(q, kcvc[:, g * 256:g * 256 + 128]) * scale
        n = lax.broadcasted_iota(jnp.int32, s.shape, 1)
        p = _masked_softmax(s, n < n_chunks - 1)
        oc_ref[g] = _mm(p, kcvc[:, g * 256 + 128:g * 256 + 256])
        psum = jnp.sum(p[0:NSA_HPG], axis=0, keepdims=True)
        imp = _mm3(jnp.broadcast_to(psum, (8, n_chunks)), ovl_ref[...])
        blk = lax.broadcasted_iota(jnp.int32, imp.shape, 1)
        n_blk = n_chunks * CMP_STRIDE // SEL_BLOCK
        forced = (blk == 0) | (blk == n_blk - 1)
        score = jnp.where(blk < n_blk, jnp.where(forced, SEL_BIG, imp), REMOVED)
        _, picks = _topk_select(score, N_SEL - 1)
        idx_ref[g] = picks.astype(jnp.int32)
        new = new_ref[g:g + 1, :]
        sw = _mm_nt(q, swa) * scale + bw_ref[g][:, 0:n_win]
        s_new = (jnp.sum(q.astype(F32) * new, axis=-1, keepdims=True) * scale
                 + bw_ref[g][:, n_win:n_win + 1])
        m = jnp.maximum(jnp.max(sw, axis=-1, keepdims=True), s_new)
        e = jnp.exp(sw - m)
        e_new = jnp.exp(s_new - m)
        den = jnp.sum(e, axis=-1, keepdims=True) + e_new
        ow_ref[g] = (_mm(e, swa) + e_new * new) / den


def _nsa_decode1(page_table, ab_pool, qc, cache_swa, kvw_new, bw, pe8, w1c, w2, ovl):
    b, n_pages = page_table.shape
    full = lambda a: pl.BlockSpec(a.shape, lambda bi, pt: (0,) * a.ndim)
    out4 = lambda: pl.BlockSpec((None, NSA_KV, 8, LANE), lambda bi, pt: (bi, 0, 0, 0))
    grid_spec = pltpu.PrefetchScalarGridSpec(
        num_scalar_prefetch=1,
        grid=(b,),
        in_specs=[pl.BlockSpec(memory_space=pl.ANY),
                  pl.BlockSpec((None, NSA_KV, 8, LANE), lambda bi, pt: (bi, 0, 0, 0)),
                  pl.BlockSpec((None, cache_swa.shape[1], LANE), lambda bi, pt: (bi, 0, 0)),
                  pl.BlockSpec((None, NSA_KV, LANE), lambda bi, pt: (bi, 0, 0)),
                  full(bw), full(pe8), full(w1c), full(w2), full(ovl)],
        out_specs=[out4(), out4(), out4()],
        scratch_shapes=[pltpu.VMEM((n_pages, PAGE // CMP_STRIDE, 512), F32), pltpu.SemaphoreType.DMA(())])
    return pl.pallas_call(
        functools.partial(_nsa_decode1_kernel, n_pages=n_pages),
        grid_spec=grid_spec,
        out_shape=[jax.ShapeDtypeStruct((b, NSA_KV, 8, LANE), F32),
                   jax.ShapeDtypeStruct((b, NSA_KV, 8, LANE), F32),
                   jax.ShapeDtypeStruct((b, NSA_KV, 8, LANE), jnp.int32)],
        compiler_params=_cparams(("arbitrary",)),
        name="nsa_decode_cmp_win",
    )(page_table, ab_pool, qc, cache_swa, kvw_new, bw, pe8, w1c, w2, ovl)


def _nsa_decode2_kernel(pt_ref, idx_ref, *refs, n_blk, n_pick):
    blks = refs[:n_pick]
    q_ref, new_ref, bs_ref, o_ref = refs[n_pick:]
    b = pl.program_id(0)
    g = pl.program_id(1)
    scale = NSA_HD ** -0.5
    q = q_ref[...]
    cols = blks[0].shape[0]
    scores = []
    xs = []
    for k in range(n_pick):
        x = blks[k][...].astype(BF16)
        blk = idx_ref[b, g, k]
        bias = (jnp.where(blk == n_blk - 1, bs_ref[:, cols:2 * cols], 0.0)
                + jnp.where(blk == n_blk - 2, bs_ref[:, 2 * cols:3 * cols], 0.0))
        scores.append(_mm_nt(q, x) * scale + bs_ref[:, 0:cols] + bias)
        xs.append(x)
    new = new_ref[...]
    s_new = jnp.sum(q.astype(F32) * new, axis=-1, keepdims=True) * scale + bs_ref[:, 3 * cols:3 * cols + 1]
    s = jnp.concatenate(scores, axis=1)
    m = jnp.maximum(jnp.max(s, axis=-1, keepdims=True), s_new)
    p = jnp.exp(s - m)
    p_new = jnp.exp(s_new - m)
    acc = p_new * new
    for k in range(n_pick):
        acc = acc + _mm(p[:, k * cols:(k + 1) * cols], xs[k])
    o_ref[...] = acc / (jnp.sum(p, axis=-1, keepdims=True) + p_new)


def _nsa_decode2(page_table, idx, cache_rows, qc, nsa_new, bs):
    b = page_table.shape[0]
    n_pick = idx.shape[2]
    rows_blk = SEL_BLOCK * 2 * NSA_KV

    def blk_spec(k):
        def blk_map(bi, g, pt, ix):
            blk = ix[bi, g, k]
            return (pt[bi, lax.shift_right_logical(blk, 1)] * 2 + (blk & 1), 0)
        return pl.BlockSpec((rows_blk, LANE), blk_map)

    grid_spec = pltpu.PrefetchScalarGridSpec(
        num_scalar_prefetch=2,
        grid=(b, NSA_KV),
        in_specs=[blk_spec(k) for k in range(n_pick)] + [
            pl.BlockSpec((None, None, 8, LANE), lambda bi, g, pt, ix: (bi, g, 0, 0)),
            pl.BlockSpec((None, None, 1, LANE), lambda bi, g, pt, ix: (bi, g, 0, 0)),
            pl.BlockSpec((None, 8, bs.shape[2]), lambda bi, g, pt, ix: (g, 0, 0))],
        out_specs=pl.BlockSpec((None, None, 8, LANE), lambda bi, g, pt, ix: (bi, g, 0, 0)))
    return pl.pallas_call(
        functools.partial(_nsa_decode2_kernel, n_blk=page_table.shape[1] * PAGE // SEL_BLOCK, n_pick=n_pick),
        grid_spec=grid_spec,
        out_shape=jax.ShapeDtypeStruct((b, NSA_KV, 8, LANE), F32),
        compiler_params=_cparams(("arbitrary", "arbitrary")),
        name="nsa_decode_sel",
    )(page_table, idx, *([cache_rows] * n_pick), qc, nsa_new, bs)


def _prompt_mixers(pr, b, s, gd_a, gd_b, da_lambda, w1, w2, pe8, w1c):
    sh = lambda a: a.reshape(b, s, a.shape[-1])
    nk = s // TKI
    vt = lambda a: a.reshape(b, nk, a.shape[1], TKI)
    o_a = _da_prompt(sh(pr["qa"]), sh(pr["dak"]), vt(pr["davt"]), gd_a, da_lambda)
    n_chunks = s // CMP_STRIDE
    ab = _cmp_stage1(pr["nsa"].reshape(b * n_chunks, CMP_STRIDE * 512), w1, 2 * LANE)
    kcvc = _cmp_stage2_prompt(ab.reshape(b, n_chunks, 512), pe8, w1c, w2)
    o_c, sel = _nsa_cmp_prompt(sh(pr["qb"]), kcvc, _ovl_t(n_chunks))
    o_s, o_w = _nsa_sw_prompt(sh(pr["qb"]), sh(pr["ksel"]), sh(pr["kwin"]), vt(pr["selvt"]), vt(pr["winvt"]),
                              sel, gd_b)
    flat = lambda a: a.reshape(b * s, a.shape[-1])
    return flat(o_a), flat(o_c), flat(o_s), flat(o_w)


def _sample_mixers(pr, page_table, cache_da, cache_nsa, cache_swa, gd_a, gd_b, da_lambda, w1, w2, pe8, w1c):
    b, n_pages = page_table.shape
    past = n_pages * PAGE
    n_pool = cache_da.shape[0]
    da_rows = cache_da.reshape(n_pool, PAGE, DA_HEADS, 2, LANE).transpose(0, 1, 3, 2, 4).reshape(-1, LANE)
    nsa_rows = cache_nsa.reshape(n_pool, PAGE, NSA_KV, 2, LANE).transpose(0, 1, 3, 2, 4).reshape(-1, LANE)
    qa = pr["qa"].reshape(b, DA_HEADS, 2, DA_HD)
    q8 = jnp.zeros((b, DA_HEADS, 2, 2, DA_HD), BF16)
    for c in range(2):
        q8 = q8.at[:, :, c, c, :].set(qa[:, :, c])
    q8 = q8.reshape(b, 8, LANE)
    da_new = pr["da"].reshape(b, DA_HEADS, 2, LANE)
    k_new = jnp.repeat(da_new[:, :, 0], 2, axis=1)
    v_new = jnp.repeat(da_new[:, :, 1], 2, axis=1)
    gda8 = jnp.repeat(gd_a, 2, axis=0)
    bl = jnp.repeat(gda8[:, PAGE - jnp.arange(PAGE)], 2 * DA_HEADS, axis=1)
    b0 = jnp.broadcast_to(gda8[:, 0:1], (8, LANE))
    o_a = _da_decode(page_table, da_rows, q8, k_new, v_new, bl, b0, da_lambda)
    o_a = o_a[:, 0::2, :].reshape(b, 512)
    chunks = PAGE // CMP_STRIDE
    ab_pool = _cmp_stage1_rows(nsa_rows, w1).reshape(n_pool, chunks, 512)
    qb = pr["qb"].reshape(b, NSA_KV, NSA_HPG, NSA_HD)
    qc = jnp.zeros((b, NSA_KV, 8, LANE), BF16).at[:, :, :NSA_HPG, :NSA_HD].set(qb)
    gdb = jnp.pad(gd_b.reshape(NSA_KV, NSA_HPG, -1), ((0, 0), (0, 8 - NSA_HPG), (0, 0)))
    own = jnp.arange(NSA_KV)[:, None, None]
    n_win = cache_swa.shape[1]
    grp_w = jnp.arange(n_win * NSA_KV)[None, None, :] % NSA_KV
    bw = jnp.where(grp_w == own, jnp.repeat(gdb[:, :, n_win - jnp.arange(n_win)], NSA_KV, axis=2), NEG)
    bw = jnp.concatenate([bw, jnp.broadcast_to(gdb[:, :, 0:1], (NSA_KV, 8, LANE))], axis=2)
    o_c, o_w, idx = _nsa_decode1(page_table, ab_pool, qc, cache_swa.reshape(b, n_win * NSA_KV, LANE),
                                 pr["kvw"].reshape(b, NSA_KV, LANE), bw, pe8, w1c, w2,
                                 _ovl_t(past // CMP_STRIDE))
    idx = idx[:, :, 0, :N_SEL - 1]
    rows_blk = SEL_BLOCK * 2 * NSA_KV
    kind = jnp.arange(rows_blk)[None, None, :] % (2 * NSA_KV)
    keep = kind == NSA_KV + own
    bs = jnp.concatenate([
        jnp.where(keep, 0.0, NEG) * jnp.ones((1, 8, 1), F32),
        jnp.repeat(gdb[:, :, SEL_BLOCK - jnp.arange(SEL_BLOCK)], 2 * NSA_KV, axis=2),
        jnp.repeat(gdb[:, :, 2 * SEL_BLOCK - jnp.arange(SEL_BLOCK)], 2 * NSA_KV, axis=2),
        jnp.broadcast_to(gdb[:, :, 0:1], (NSA_KV, 8, LANE))], axis=2)
    sel_new = pr["nsa"].reshape(b, NSA_KV, 2, LANE)[:, :, 1:2, :]
    o_s = _nsa_decode2(page_table, idx, nsa_rows, qc, sel_new, bs)
    o_c = o_c[:, :, :NSA_HPG, :NSA_HD].reshape(b, 512)
    o_w = o_w[:, :, :NSA_HPG, NSA_HD:].reshape(b, 512)
    o_s = o_s[:, :, :NSA_HPG, NSA_HD:].reshape(b, 512)
    return o_a, o_c, o_s, o_w


def kernel(x_prompt, x_sample, cache_da_kv, cache_nsa_kv, cache_swa_kv, page_table, c_prompt, c_sample, rel_bias, w_ada, b_ada, w_in, da_lambda, da_subln, cmp_pe, cmp_w1, cmp_w2, w_br_a, w_br_b, w_out, ln1_g, ln1_b, w_up, w_down, ln2_g, ln2_b):
    bp, s, _ = x_prompt.shape
    bs_ = x_sample.shape[0]
    w_perm, w_vt = _perm_w_in(w_in[0])
    w1, w2 = _cmp_weights(cmp_w1[0], cmp_w2[0])
    pe8 = jnp.broadcast_to(cmp_pe[0].reshape(2, 1, CMP_LEN * NSA_HD), (2, 8, CMP_LEN * NSA_HD))
    w1c = cmp_w1[0].reshape(2, CMP_LEN * NSA_HD, CMP_HID)
    gd_a = _dist_bias(rel_bias[:, :DA_HEADS], 1024)
    gd_b = _dist_bias(rel_bias[:, DA_HEADS:], 1024)
    wa, wb, wo = w_br_a[0].astype(BF16), w_br_b[0].astype(BF16), w_out[0].astype(BF16)
    wu, wd = w_up[0].astype(BF16), w_down[0].astype(BF16)
    sub = da_subln[0].reshape(1, DA_VD)
    eg = _gate_expand()
    g1, b1 = ln1_g[0].reshape(1, D_MODEL), ln1_b[0].reshape(1, D_MODEL)
    g2, b2 = ln2_g[0].reshape(1, D_MODEL), ln2_b[0].reshape(1, D_MODEL)
    lam = da_lambda[0]

    n_c = bp + bs_
    c_all = jnp.pad(jnp.concatenate([c_prompt, c_sample], 0), ((0, (-n_c) % 8), (0, 0)))
    ada = _ada(c_all, w_ada[0], b_ada[0])[:n_c].reshape(n_c, 6, D_MODEL)
    ada_p = jnp.transpose(ada[:bp], (1, 0, 2)).reshape(6, bp, 1, D_MODEL)
    ada_s = jnp.transpose(ada[bp:], (1, 0, 2)).reshape(6, 1, bs_, D_MODEL)

    def tail(mix, pr, x2d, ada_x, tm, tpg):
        o_a, o_c, o_s, o_w = mix
        x1 = _tail1(o_a, o_c, o_s, o_w, pr["gate"], pr["ma"], pr["mb"], x2d, ada_x,
                    wa, wb, wo, sub, eg, g1, b1, tm, tpg)
        return _tail2(x1, ada_x, wu, wd, g2, b2, tm, tpg)

    xp = x_prompt.reshape(bp * s, D_MODEL)
    pr_p = _proj(xp, ada_p, w_perm, w_vt, TKI, s // TKI)
    mix_p = _prompt_mixers(pr_p, bp, s, gd_a, gd_b, lam, w1, w2, pe8, w1c)
    tm_t = 512
    y_p = tail(mix_p, pr_p, xp, ada_p, tm_t, s // tm_t).reshape(bp, s, D_MODEL)
    xs = x_sample.reshape(bs_, D_MODEL)
    pr_s = _proj(xs, ada_s, w_perm, None, bs_, 1)
    mix_s = _sample_mixers(pr_s, page_table, cache_da_kv[0], cache_nsa_kv[0], cache_swa_kv[0],
                           gd_a, gd_b, lam, w1, w2, pe8, w1c)
    y_s = tail(mix_s, pr_s, xs, ada_s, bs_, 1).reshape(bs_, 1, D_MODEL)

    win = min(WINDOW, s)
    new_da_p = pr_p["da"].reshape(1, bp, s, DA_HEADS, 4 * DA_HD)
    new_nsa_p = pr_p["nsa"].reshape(1, bp, s, NSA_KV, 4 * NSA_HD)
    new_swa_p = pr_p["kvw"].reshape(bp, s, NSA_KV, 2 * NSA_HD)[None, :, s - win:]
    new_da_s = pr_s["da"].reshape(1, bs_, 1, DA_HEADS, 4 * DA_HD)
    new_nsa_s = pr_s["nsa"].reshape(1, bs_, 1, NSA_KV, 4 * NSA_HD)
    new_swa_s = jnp.concatenate([cache_swa_kv[0][:, 1:], pr_s["kvw"].reshape(bs_, 1, NSA_KV, 2 * NSA_HD)],
                                axis=1)[None]
    return (y_p, y_s, new_da_p, new_nsa_p, new_swa_p, new_da_s, new_nsa_s, new_swa_s)
```

```python
import functools
import math

import numpy as np
import jax
import jax.numpy as jnp
from jax import lax
from jax.experimental import pallas as pl
from jax.experimental.pallas import tpu as pltpu

F32 = jnp.float32
BF16 = jnp.bfloat16

D_MODEL = 1024
PAGE = 128
DA_HEADS = 4
DA_HD = 64
DA_VD = 128
NSA_HEADS = 8
NSA_KV = 2
NSA_HPG = 4
NSA_HD = 64
CMP_STRIDE = 16
CMP_LEN = 32
CMP_HID = 64
SEL_BLOCK = 64
N_SEL = 16
WINDOW = 512
D_FF = 4096
N_BUCKETS = 32
MAX_DIST = 128
DEPTH = 1
ALPHA = (2 * DEPTH) ** 0.25
LN_EPS = 1e-5
RMS_EPS = 1e-5
NEG = -1e30
SEL_BIG = 1e9
LAM_INIT = 0.8 - 0.6 * math.exp(-0.3 * 0)
SPLIT_SIZES = (512, 512, 512, 512, 256, 256, 256, 24, 1024, 1024)

LANE = 128
VMEM_LIMIT = 56 * 1024 * 1024
TQ = 128
TKI = 256
TQS = 256
ONES_ROWS = 16
TDA = 512
SEL_LANES = 128
REMOVED = -3e38
PP = 16


def _cparams(sem):
    return pltpu.CompilerParams(dimension_semantics=sem, vmem_limit_bytes=VMEM_LIMIT)


def _mm(a, b):
    return jnp.dot(a.astype(BF16), b.astype(BF16), preferred_element_type=F32)


def _mm_nt(a, b):
    return lax.dot_general(a.astype(BF16), b.astype(BF16), (((1,), (1,)), ((), ())),
                           preferred_element_type=F32)


def _mm3(x, w):
    hi = x.astype(BF16)
    r = x - hi.astype(F32)
    mid = r.astype(BF16)
    lo = (r - mid.astype(F32)).astype(BF16)
    return (jnp.dot(hi, w, preferred_element_type=F32) + jnp.dot(mid, w, preferred_element_type=F32)
            + jnp.dot(lo, w, preferred_element_type=F32))


def _masked_softmax(s, valid):
    l = jnp.where(valid, s, NEG)
    m = jnp.max(l, axis=-1, keepdims=True)
    e = jnp.where(valid, jnp.exp(l - m), 0.0)
    return e / jnp.maximum(jnp.sum(e, axis=-1, keepdims=True), 1e-30)


def _t5_bucket(dist):
    n = jnp.maximum(dist, 0)
    max_exact = N_BUCKETS // 2
    nf = jnp.maximum(n, 1).astype(F32)
    large = max_exact + (jnp.log(nf / max_exact) / math.log(MAX_DIST / max_exact)
                         * (N_BUCKETS - max_exact)).astype(jnp.int32)
    return jnp.where(n < max_exact, n, jnp.minimum(large, N_BUCKETS - 1))


def _dist_bias(tbl, n):
    d = jnp.arange(n, dtype=jnp.int32)
    g = tbl[_t5_bucket(d)] - tbl[N_BUCKETS - 1][None, :]
    return jnp.transpose(g)


def _toeplitz(gd, offset, rows, cols, below=0.0):
    heads, n = gd.shape
    length = rows + cols - 1
    assert offset + rows <= n
    lo = offset - cols + 1
    hvec = gd[:, max(lo, 0):offset + rows]
    if lo < 0:
        hvec = jnp.concatenate([jnp.full((heads, -lo), below, gd.dtype), hvec], axis=1)
    rev = jnp.concatenate([hvec[:, ::-1], jnp.zeros((heads, 1), gd.dtype)], axis=1)
    flat = jnp.tile(rev, (1, rows))[:, :rows * length].reshape(heads, rows, length)
    return flat[:, :, rows - 1:rows - 1 + cols]


def _diff_lambda(lam_ref):
    l = lam_ref[...]
    a = jnp.sum(l[0:1, :] * l[1:2, :], axis=-1, keepdims=True)
    b = jnp.sum(l[2:3, :] * l[3:4, :], axis=-1, keepdims=True)
    return jnp.exp(a) - jnp.exp(b) + LAM_INIT


def _ada_kernel(c_ref, w_ref, b_ref, o_ref):
    c = c_ref[...]
    o_ref[...] = _mm(c * jax.nn.sigmoid(c), w_ref[...]) + b_ref[...]


def _ada(c, w_ada, b_ada):
    m = c.shape[0]
    n = w_ada.shape[1]
    tn = 512
    return pl.pallas_call(
        _ada_kernel,
        grid=(n // tn,),
        in_specs=[pl.BlockSpec((m, D_MODEL), lambda j: (0, 0)),
                  pl.BlockSpec((D_MODEL, tn), lambda j: (0, j)),
                  pl.BlockSpec((1, tn), lambda j: (0, j))],
        out_specs=pl.BlockSpec((m, tn), lambda j: (0, j)),
        out_shape=jax.ShapeDtypeStruct((m, n), F32),
        compiler_params=_cparams(("arbitrary",)),
        name="ada",
    )(c, w_ada, b_ada.reshape(1, n))


PROJ_GROUPS = (("qa", 512, BF16), ("da", 1024, F32), ("qb", 512, BF16), ("nsa", 512, F32), ("kvw", 256, F32),
               ("gate", 128, F32), ("ma", 1024, F32), ("mb", 1024, F32))
PROJ_KEY_GROUPS = (("ksel", 256), ("kwin", 256))
LOG2E = 1.4426950408889634
Q_SCALE = {"qa": DA_HD ** -0.5 * LOG2E, "qb": NSA_HD ** -0.5 * LOG2E}
PROJ_W = sum(w for _, w, _ in PROJ_GROUPS)
PROJ_W_KV = PROJ_W + sum(w for _, w in PROJ_KEY_GROUPS)
PROJ_VT = (("davt", DA_HEADS * DA_VD), ("selvt", NSA_KV * NSA_HD), ("winvt", NSA_KV * NSA_HD))


def _perm_w_in(w_in):
    parts = jnp.split(w_in, np.cumsum(SPLIT_SIZES)[:-1].tolist(), axis=1)
    qa, ka, va, qn, kvc, kvs, kvw, gb, ma, mb = parts
    da = jnp.concatenate([jnp.concatenate([ka[:, h * 128:(h + 1) * 128], va[:, h * 128:(h + 1) * 128]], 1)
                          for h in range(DA_HEADS)], 1)
    nsa = jnp.concatenate([jnp.concatenate([kvc[:, g * 128:(g + 1) * 128], kvs[:, g * 128:(g + 1) * 128]], 1)
                           for g in range(NSA_KV)], 1)
    gate = jnp.pad(gb, ((0, 0), (0, LANE - gb.shape[1])))
    keys2 = lambda kv: jnp.concatenate([kv[:, g * 128:g * 128 + 64] for g in range(NSA_KV) for _ in range(2)], 1)
    vals = lambda kv: jnp.concatenate([kv[:, g * 128 + 64:(g + 1) * 128] for g in range(NSA_KV)], 1)
    cols = dict(qa=qa, da=da, qb=qn, nsa=nsa, kvw=kvw, gate=gate, ma=ma, mb=mb, ksel=keys2(kvs), kwin=keys2(kvw))
    names = [n for n, _, _ in PROJ_GROUPS] + [n for n, _ in PROJ_KEY_GROUPS]
    w_perm = jnp.concatenate([cols[n] for n in names], 1).astype(BF16)
    w_vt = jnp.transpose(jnp.concatenate([va, vals(kvs), vals(kvw)], 1)).astype(BF16)
    return w_perm, w_vt


def _proj_kernel(x_ref, ada_ref, w_ref, *refs, with_kv):
    h = (x_ref[...] * (1.0 + ada_ref[1]) + ada_ref[0]).astype(BF16)
    o_refs = refs[1:] if with_kv else refs
    off = 0
    k = 0
    for name, width, dt in PROJ_GROUPS:
        acc = jnp.dot(h, w_ref[:, off:off + width], preferred_element_type=F32)
        o_refs[k][...] = acc.astype(dt)
        k += 1
        if with_kv and name in Q_SCALE:
            o_refs[k][...] = (acc * Q_SCALE[name]).astype(BF16)
            k += 1
        if with_kv and name == "da":
            o_refs[k][...] = jnp.concatenate([acc[:, hd * 256:hd * 256 + 128] for hd in range(DA_HEADS)],
                                             axis=1).astype(BF16)
            k += 1
        off += width
    if with_kv:
        for _, width in PROJ_KEY_GROUPS:
            o_refs[k][...] = jnp.dot(h, w_ref[:, off:off + width], preferred_element_type=F32).astype(BF16)
            k += 1
            off += width
        vt = _mm_nt(refs[0][...], h)
        r0 = 0
        for _, rows in PROJ_VT:
            o_refs[k][...] = vt[r0:r0 + rows].astype(BF16)
            k += 1
            r0 += rows


def _ada_spec(ada, tm, tiles_per_group):
    r = ada.shape[2]
    return pl.BlockSpec((6, None, r, D_MODEL), lambda i, *_: (0, i // tiles_per_group, 0, 0))


def _proj(x2d, ada, w_perm, w_vt, tm, tiles_per_group):
    m = x2d.shape[0]
    with_kv = w_vt is not None
    outs = []
    for name, w, dt in PROJ_GROUPS:
        outs.append((name, (m, w), (tm, w), dt))
        if with_kv and name in Q_SCALE:
            outs.append((name + "2", (m, w), (tm, w), BF16))
        if with_kv and name == "da":
            outs.append(("dak", (m, 512), (tm, 512), BF16))
    in_specs = [pl.BlockSpec((tm, D_MODEL), lambda i: (i, 0)),
                _ada_spec(ada, tm, tiles_per_group),
                pl.BlockSpec((D_MODEL, PROJ_W_KV if with_kv else PROJ_W), lambda i: (0, 0))]
    args = [x2d, ada, w_perm]
    if with_kv:
        assert tm == TKI
        outs += [(n, (m, w), (tm, w), BF16) for n, w in PROJ_KEY_GROUPS]
        outs += [(n, (m // tm, r, tm), (None, r, tm), BF16) for n, r in PROJ_VT]
        in_specs.append(pl.BlockSpec(w_vt.shape, lambda i: (0, 0)))
        args.append(w_vt)
    res = pl.pallas_call(
        functools.partial(_proj_kernel, with_kv=with_kv),
        grid=(m // tm,),
        in_specs=in_specs,
        out_specs=[pl.BlockSpec(blk, (lambda i: (i, 0)) if len(blk) == 2 else (lambda i: (i, 0, 0)))
                   for _, _, blk, _ in outs],
        out_shape=[jax.ShapeDtypeStruct(shape, dt) for _, shape, _, dt in outs],
        compiler_params=_cparams(("arbitrary",)),
        name="proj",
    )(*args)
    return {n: o for (n, _, _, _), o in zip(outs, res)}


class _Chain:
    def __init__(self, s_bufs, p_bufs, al_bufs, m_ref, acc_ref, score_fn, vext_fn, bias_fn):
        self.s, self.p, self.al = s_bufs, p_bufs, al_bufs
        self.m, self.acc = m_ref, acc_ref
        self.score_fn, self.vext_fn, self.bias_fn = score_fn, vext_fn, bias_fn
        self.maps = m_ref.shape[0]
        self.cur = 0

    def prime(self, j):
        self.cur = 0
        self.m[...] = jnp.full(self.m.shape, NEG, F32)
        self.acc[...] = jnp.zeros(self.acc.shape, F32)
        self.p[1][...] = jnp.zeros(self.p[1].shape, BF16)
        self.al[1][...] = jnp.ones(self.al[1].shape, F32)
        self.score_fn(j, self.s[0])

    def _finish(self, k, buf, pv):
        self.acc[k] = self.acc[k] * self.al[buf][k, 0:1, :] + pv

    def step(self, j_prev, j_next, table=None, valid=None, pen=None, prev_valid=None):
        c, o = self.cur, 1 - self.cur
        vext = self.vext_fn(j_prev)
        if prev_valid is not None:
            vext = jnp.where(prev_valid, vext, jnp.zeros_like(vext))
        pv = [jnp.dot(vext, self.p[o][k], preferred_element_type=F32) for k in range(self.maps)]
        if j_next is not None:
            self.score_fn(j_next, self.s[o])
        for k in range(self.maps):
            for l0 in range(0, self.m.shape[2], LANE):
                cols = slice(l0, l0 + LANE)
                s = self.s[c][k, :, cols]
                if table is not None:
                    s = s + self.bias_fn(table, l0)
                if pen is not None:
                    s = s + pen
                m_cur = jnp.max(s, axis=0, keepdims=True)
                if valid is not None:
                    m_cur = jnp.where(valid, m_cur, NEG)
                m_old = self.m[k, 0:1, cols]
                m_new = jnp.maximum(m_old, m_cur)
                self.p[c][k, :, cols] = jnp.exp2(s - m_new).astype(BF16)
                self.al[c][k, :, cols] = jnp.broadcast_to(jnp.exp2(m_old - m_new), (8, LANE))
                self.m[k, :, cols] = jnp.broadcast_to(m_new, (8, LANE))
            self._finish(k, o, pv[k])
        self.cur = o

    def step_eager(self, j_cur, j_next, table=None, valid=None, pen=None):
        c, o = self.cur, 1 - self.cur
        if j_next is not None:
            self.score_fn(j_next, self.s[o])
        vext = self.vext_fn(j_cur)
        if valid is not None:
            vext = jnp.where(valid, vext, jnp.zeros_like(vext))
        for k in range(self.maps):
            ps, alphas = [], []
            for l0 in range(0, self.m.shape[2], LANE):
                cols = slice(l0, l0 + LANE)
                s = self.s[c][k, :, cols]
                if table is not None:
                    s = s + self.bias_fn(table, l0)
                if pen is not None:
                    s = s + pen
                m_cur = jnp.max(s, axis=0, keepdims=True)
                if valid is not None:
                    m_cur = jnp.where(valid, m_cur, NEG)
                m_old = self.m[k, 0:1, cols]
                m_new = jnp.maximum(m_old, m_cur)
                ps.append(jnp.exp2(s - m_new).astype(BF16))
                alphas.append(jnp.exp2(m_old - m_new))
                self.m[k, :, cols] = jnp.broadcast_to(m_new, (8, LANE))
            self.acc[k] = (self.acc[k] * jnp.concatenate(alphas, axis=1)
                           + jnp.dot(vext, jnp.concatenate(ps, axis=1), preferred_element_type=F32))
        self.cur = o

    def flush(self, j_prev):
        o = 1 - self.cur
        vext = self.vext_fn(j_prev)
        for k in range(self.maps):
            self._finish(k, o, jnp.dot(vext, self.p[o][k], preferred_element_type=F32))


def _chain_scratch(maps, acc_rows, queries):
    return [pltpu.VMEM((maps, TKI, queries), F32), pltpu.VMEM((maps, TKI, queries), F32),
            pltpu.VMEM((maps, TKI, queries), BF16), pltpu.VMEM((maps, TKI, queries), BF16),
            pltpu.VMEM((maps, 8, queries), F32), pltpu.VMEM((maps, 8, queries), F32),
            pltpu.VMEM((maps, 8, queries), F32), pltpu.VMEM((maps, acc_rows, queries), F32)]


def _key_rows(ref, j):
    return pl.ds(pl.multiple_of(j * TKI, TKI), TKI)


def _da_kernel(q_ref, k_ref, vt_ref, tb_ref, lam_ref, o_ref, s_a, s_b, p_a, p_b, al_a, al_b, m_ref, acc_ref):
    qi = pl.program_id(2)
    q = q_ref[...]
    lane = lax.broadcasted_iota(jnp.int32, q.shape, 1)
    zero = jnp.zeros_like(q)
    q_maps = (jnp.where(lane < DA_HD, q, zero), jnp.where(lane >= DA_HD, q, zero))
    ones = jnp.ones((ONES_ROWS, TKI), BF16)

    def scores(j, dst):
        kk = k_ref[_key_rows(k_ref, j), :]
        for c in range(2):
            dst[c] = _mm_nt(kk, q_maps[c])

    chain = _Chain((s_a, s_b), (p_a, p_b), (al_a, al_b), m_ref, acc_ref, scores,
                   lambda j: jnp.concatenate([vt_ref[j], ones], axis=0),
                   lambda table, l0: tb_ref[table, :, l0:l0 + LANE])
    n_far = jnp.maximum(2 * qi - 1, 0)
    j_d = 2 * qi
    j_s = jnp.maximum(2 * qi - 1, 0)
    chain.prime(jnp.where(n_far > 0, 0, j_d))

    def pair(i, carry):
        a = 2 * i
        b = jnp.minimum(a + 1, n_far - 1)
        chain.step_eager(a, b)
        chain.step_eager(b, jnp.where(a + 2 < n_far, a + 2, j_d), valid=a + 1 < n_far)
        return carry

    lax.fori_loop(0, lax.shift_right_logical(n_far + 1, 1), pair, 0)
    chain.step_eager(j_d, j_s, table=1)
    chain.step_eager(j_s, j_d + 1, table=0, pen=jnp.where(qi > 0, 0.0, NEG))
    chain.step_eager(j_d + 1, None, table=2)
    lam = _diff_lambda(lam_ref)
    o_t = (acc_ref[0, 0:DA_VD, :] / acc_ref[0, DA_VD:DA_VD + 1, :]
           - lam * (acc_ref[1, 0:DA_VD, :] / acc_ref[1, DA_VD:DA_VD + 1, :]))
    o_ref[...] = o_t.T


def _da_prompt(qa, dak, davt, gd_a, da_lambda):
    b, s, _ = qa.shape
    tb = jnp.stack([_toeplitz(gd_a, TKI, TDA, TKI), _toeplitz(gd_a, 0, TDA, TKI, NEG),
                    _toeplitz(gd_a, -TKI, TDA, TKI, NEG)], axis=1)
    tb = jnp.swapaxes(tb, 2, 3) * LOG2E
    acc_rows = DA_VD + ONES_ROWS
    return pl.pallas_call(
        _da_kernel,
        grid=(b, DA_HEADS, s // TDA),
        in_specs=[pl.BlockSpec((None, TDA, 128), lambda bi, h, qi: (bi, qi, h)),
                  pl.BlockSpec((None, s, 128), lambda bi, h, qi: (bi, 0, h)),
                  pl.BlockSpec((None, s // TKI, DA_VD, TKI), lambda bi, h, qi: (bi, 0, h, 0)),
                  pl.BlockSpec((None, 3, TKI, TDA), lambda bi, h, qi: (h, 0, 0, 0)),
                  pl.BlockSpec((4, DA_HD), lambda bi, h, qi: (0, 0))],
        out_specs=pl.BlockSpec((None, TDA, 128), lambda bi, h, qi: (bi, qi, h)),
        out_shape=jax.ShapeDtypeStruct((b, s, DA_HEADS * DA_VD), F32),
        scratch_shapes=_chain_scratch(2, acc_rows, TDA),
        compiler_params=_cparams(("arbitrary", "arbitrary", "arbitrary")),
        name="da_prompt",
    )(qa, dak, davt, tb, da_lambda)


def _cmp_weights(cmp_w1, cmp_w2):
    w1 = jnp.zeros((CMP_STRIDE, 2, NSA_HD, 2, 2, CMP_HID), F32)
    for half in range(2):
        for c in range(2):
            blk = cmp_w1[c, half * CMP_STRIDE:(half + 1) * CMP_STRIDE]
            w1 = w1.at[:, c, :, half, c, :].set(blk)
    w1 = w1.reshape(CMP_STRIDE * 2 * NSA_HD, 2 * 2 * CMP_HID).astype(BF16)
    w2 = jnp.zeros((2, CMP_HID, 2, 2, NSA_HD), F32)
    for c in range(2):
        for rep in range(2):
            w2 = w2.at[c, :, c, rep, :].set(cmp_w2[c])
    w2 = w2.reshape(2 * CMP_HID, 2 * 2 * NSA_HD).astype(BF16)
    return w1, w2


def _cmp1_kernel(x_ref, w_ref, o_ref, *, pos_lanes, grp_lanes):
    for g in range(NSA_KV):
        xg = jnp.concatenate([x_ref[:, l * pos_lanes + g * grp_lanes:l * pos_lanes + g * grp_lanes + LANE]
                              for l in range(CMP_STRIDE)], axis=1)
        o_ref[:, g * 256:(g + 1) * 256] = _mm(xg, w_ref[...])


def _cmp_stage1(x2d, w1, grp_lanes):
    r, width = x2d.shape
    tm = math.gcd(r, 256)
    return pl.pallas_call(
        functools.partial(_cmp1_kernel, pos_lanes=width // CMP_STRIDE, grp_lanes=grp_lanes),
        grid=(r // tm,),
        in_specs=[pl.BlockSpec((tm, width), lambda i: (i, 0)),
                  pl.BlockSpec(w1.shape, lambda i: (0, 0))],
        out_specs=pl.BlockSpec((tm, 512), lambda i: (i, 0)),
        out_shape=jax.ShapeDtypeStruct((r, 512), F32),
        compiler_params=_cparams(("arbitrary",)),
        name="cmp_stage1",
    )(x2d, w1)


def _cmp1_rows_kernel(x_ref, w_ref, o_ref, *, tm):
    rows_pos = 2 * NSA_KV
    for g in range(NSA_KV):
        xg = jnp.concatenate([x_ref[pl.ds(l * rows_pos + g, tm, stride=CMP_STRIDE * rows_pos), :]
                              for l in range(CMP_STRIDE)], axis=1)
        o_ref[:, g * 256:(g + 1) * 256] = _mm(xg, w_ref[...])


def _cmp_stage1_rows(rows2d, w1):
    rows_chunk = CMP_STRIDE * 2 * NSA_KV
    r = rows2d.shape[0] // rows_chunk
    tm = math.gcd(r, 128)
    return pl.pallas_call(
        functools.partial(_cmp1_rows_kernel, tm=tm),
        grid=(r // tm,),
        in_specs=[pl.BlockSpec((tm * rows_chunk, LANE), lambda i: (i, 0)),
                  pl.BlockSpec(w1.shape, lambda i: (0, 0))],
        out_specs=pl.BlockSpec((tm, 512), lambda i: (i, 0)),
        out_shape=jax.ShapeDtypeStruct((r, 512), F32),
        compiler_params=_cparams(("arbitrary",)),
        name="cmp_stage1_rows",
    )(rows2d, w1)


def _cmp_stage2(ab, pe_ref, w1c_ref, w2_ref):
    n = ab.shape[0]
    cst = jnp.concatenate([_mm(pe_ref[c], w1c_ref[c])[0:1, :] for c in range(2)], axis=1)
    outs = []
    for g in range(NSA_KV):
        a = ab[:, g * 256:g * 256 + 128]
        bn = pltpu.roll(ab[:, g * 256 + 128:g * 256 + 256], n - 1, 0)
        hid = jax.nn.gelu(a + bn + cst)
        outs.append(_mm(hid, w2_ref[...]))
    return jnp.concatenate(outs, axis=1)


def _cmp2_kernel(ab_ref, pe_ref, w1c_ref, w2_ref, o_ref):
    o_ref[...] = _cmp_stage2(ab_ref[...], pe_ref, w1c_ref, w2_ref).astype(BF16)


def _cmp_stage2_prompt(ab, pe8, w1c, w2):
    b, n, _ = ab.shape
    return pl.pallas_call(
        _cmp2_kernel,
        grid=(b,),
        in_specs=[pl.BlockSpec((None, n, 512), lambda i: (i, 0, 0)),
                  pl.BlockSpec(pe8.shape, lambda i: (0, 0, 0)),
                  pl.BlockSpec(w1c.shape, lambda i: (0, 0, 0)),
                  pl.BlockSpec(w2.shape, lambda i: (0, 0))],
        out_specs=pl.BlockSpec((None, n, 512), lambda i: (i, 0, 0)),
        out_shape=jax.ShapeDtypeStruct((b, n, 512), BF16),
        compiler_params=_cparams(("arbitrary",)),
        name="cmp_stage2",
    )(ab, pe8, w1c, w2)


def _ovl_t(n_chunks):
    n = np.arange(n_chunks)[:, None]
    j = np.arange(SEL_LANES)[None, :]
    ovl = (n * CMP_STRIDE < j * SEL_BLOCK + SEL_BLOCK) & (j * SEL_BLOCK < n * CMP_STRIDE + CMP_LEN)
    ovl &= n < n_chunks - 1
    return jnp.asarray(ovl.astype(np.float32)).astype(BF16)


def _topk_select(score, n_pick):
    lane = lax.broadcasted_iota(jnp.int32, score.shape, 1).astype(F32)
    sel = jnp.zeros(score.shape, F32)
    picks = jnp.zeros(score.shape, F32)
    sc = score
    for it in range(n_pick):
        m = jnp.max(sc, axis=-1, keepdims=True)
        first = jnp.min(jnp.where(sc == m, lane, float(SEL_LANES)), axis=-1, keepdims=True)
        hit = lane == first
        sel = jnp.where(hit, 1.0, sel)
        picks = jnp.where(lane == float(it), first, picks)
        sc = jnp.where(hit, REMOVED, sc)
    return sel, picks


def _topk_mask_cols(score_t, n_pick):
    blk = lax.broadcasted_iota(jnp.int32, score_t.shape, 0).astype(F32)
    sel = jnp.zeros(score_t.shape, F32)
    sc = score_t
    for _ in range(n_pick):
        m = jnp.max(sc, axis=0, keepdims=True)
        first = jnp.min(jnp.where(sc == m, blk, float(SEL_LANES)), axis=0, keepdims=True)
        hit = blk == first
        sel = jnp.where(hit, 1.0, sel)
        sc = jnp.where(hit, REMOVED, sc)
    return sel


def _stack_heads(q):
    lane = lax.broadcasted_iota(jnp.int32, (q.shape[0], LANE), 1)
    zero = jnp.zeros((q.shape[0], LANE), q.dtype)
    parts = []
    for hp in range(NSA_HPG):
        blk = q[:, (hp // 2) * LANE:(hp // 2 + 1) * LANE]
        keep = (lane < NSA_HD) if hp % 2 == 0 else (lane >= NSA_HD)
        parts.append(jnp.where(keep, blk, zero))
    return jnp.concatenate(parts, axis=0)


def _unstack_heads(o, tq):
    lane = lax.broadcasted_iota(jnp.int32, (tq, LANE), 1)
    pairs = [jnp.where(lane < NSA_HD, o[(2 * m) * tq:(2 * m + 1) * tq], o[(2 * m + 1) * tq:(2 * m + 2) * tq])
             for m in range(2)]
    return jnp.concatenate(pairs, axis=1)


def _nsa_cmp_kernel(q_ref, kcvc_ref, ovl_ref, oc_ref, sel_ref, *, n_cmp, n_slc):
    qi = pl.program_id(2)
    n_chunks = kcvc_ref.shape[0]
    qs = _stack_heads(q_ref[...])
    kc2 = kcvc_ref[:, 0:128]
    vc2 = kcvc_ref[:, 128:256]
    s = _mm_nt(qs, kc2) * (NSA_HD ** -0.5)
    rows = NSA_HPG * TQ
    t = qi * TQ + (lax.broadcasted_iota(jnp.int32, (rows, n_chunks), 0) & (TQ - 1))
    n = lax.broadcasted_iota(jnp.int32, (rows, n_chunks), 1)
    p = _masked_softmax(s, (n * CMP_STRIDE + (CMP_LEN - 1) <= t) & (n < n_cmp))
    oc_ref[...] = _unstack_heads(_mm(p, vc2), TQ)
    psum = p[0:TQ] + p[TQ:2 * TQ] + p[2 * TQ:3 * TQ] + p[3 * TQ:4 * TQ]
    imp = _mm3(psum, ovl_ref[...])
    tq = qi * TQ + lax.broadcasted_iota(jnp.int32, (TQ, SEL_LANES), 0)
    blk = lax.broadcasted_iota(jnp.int32, (TQ, SEL_LANES), 1)
    cur = lax.shift_right_logical(tq, 6)
    forced = (blk == 0) | (blk == cur) | (blk == cur - 1)
    score = jnp.where(forced, SEL_BIG, jnp.where(blk * SEL_BLOCK <= tq, imp, -SEL_BIG))
    score = jnp.where(blk < n_slc, score, REMOVED)
    sel = _topk_mask_cols(score.T, min(N_SEL, n_slc)).T
    sel_ref[...] = jnp.where(sel > 0.5, 0.0, NEG).astype(BF16)


def _nsa_cmp_prompt(qb, kcvc, ovl):
    b, s, _ = qb.shape
    n_chunks = kcvc.shape[1]
    kern = functools.partial(_nsa_cmp_kernel, n_cmp=n_chunks - 1, n_slc=s // SEL_BLOCK)
    return pl.pallas_call(
        kern,
        grid=(b, NSA_KV, s // TQ),
        in_specs=[pl.BlockSpec((None, TQ, 256), lambda bi, g, qi: (bi, qi, g)),
                  pl.BlockSpec((None, n_chunks, 256), lambda bi, g, qi: (bi, 0, g)),
                  pl.BlockSpec(ovl.shape, lambda bi, g, qi: (0, 0))],
        out_specs=[pl.BlockSpec((None, TQ, 256), lambda bi, g, qi: (bi, qi, g)),
                   pl.BlockSpec((None, None, TQ, SEL_LANES), lambda bi, g, qi: (bi, g, qi, 0))],
        out_shape=[jax.ShapeDtypeStruct((b, s, 512), F32),
                   jax.ShapeDtypeStruct((b, NSA_KV, s, SEL_LANES), BF16)],
        compiler_params=_cparams(("arbitrary", "arbitrary", "arbitrary")),
        name="nsa_cmp",
    )(qb, kcvc, ovl)


def _nsa_sw_kernel(q_ref, ks_ref, kw_ref, vs_ref, vw_ref, sel_ref, e_ref, tb_ref, os_ref, ow_ref, *scratch):
    m = pl.program_id(2)
    qs = _stack_heads(q_ref[...])
    qs_sel = jnp.concatenate([qs, jnp.concatenate([sel_ref[...]] * NSA_HPG, axis=0)], axis=1)
    ones = jnp.ones((ONES_ROWS, TKI), BF16)

    def sel_scores(j, dst):
        dst[0] = _mm_nt(jnp.concatenate([ks_ref[_key_rows(ks_ref, j), :], e_ref[j]], axis=1), qs_sel)

    def win_scores(j, dst):
        dst[0] = _mm_nt(kw_ref[_key_rows(kw_ref, j), :], qs)

    bias = lambda table, l0: tb_ref[table, :, l0:l0 + LANE]
    sel = _Chain(scratch[0:2], scratch[2:4], scratch[4:6], scratch[6], scratch[7], sel_scores,
                 lambda j: jnp.concatenate([vs_ref[j], ones], axis=0), bias)
    win = _Chain(scratch[8:10], scratch[10:12], scratch[12:14], scratch[14], scratch[15], win_scores,
                 lambda j: jnp.concatenate([vw_ref[j], ones], axis=0), bias)

    def heads_out(acc_ref):
        o_t = acc_ref[0, 0:NSA_HD, :] / acc_ref[0, NSA_HD:NSA_HD + 1, :]
        pairs = [jnp.concatenate([o_t[:, (2 * k) * TQS:(2 * k + 1) * TQS],
                                  o_t[:, (2 * k + 1) * TQS:(2 * k + 2) * TQS]], axis=0).T
                 for k in range(NSA_HPG // 2)]
        return jnp.concatenate(pairs, axis=1)

    n_far = jnp.maximum(m - 1, 0)
    j1 = jnp.maximum(m - 1, 0)
    j2 = jnp.maximum(m - 2, 0)
    pen1 = jnp.where(m >= 1, 0.0, NEG)
    pen2 = jnp.where(m >= 2, 0.0, NEG)
    sel.prime(jnp.where(n_far > 0, 0, m))
    win.prime(m)

    def pair(i, carry):
        a = 2 * i
        sel.step(jnp.maximum(a - 1, 0), jnp.minimum(a + 1, n_far - 1))
        sel.step(a, jnp.where(a + 2 < n_far, a + 2, m), valid=a + 1 < n_far)
        return carry

    lax.fori_loop(0, lax.shift_right_logical(n_far + 1, 1), pair, 0)
    sel.step(jnp.maximum(n_far - 1, 0), j1, table=0, prev_valid=(n_far & 1) == 0)
    win.step(m, j1, table=0)
    sel.step(m, None, table=1, pen=pen1)
    win.step(m, j2, table=1, pen=pen1)
    sel.flush(j1)
    win.step(j1, None, table=2, pen=pen2)
    win.flush(j2)
    os_ref[...] = heads_out(sel.acc)
    ow_ref[...] = heads_out(win.acc)


def _nsa_sw_prompt(qb, ksel, kwin, selvt, winvt, sel, gd_b):
    b, s, _ = qb.shape
    assert TQS == TKI and WINDOW == 2 * TKI
    cols = NSA_HPG * TQS
    i = np.arange(TQS)[:, None]
    j = np.arange(TKI)[None, :]
    edge = jnp.asarray(np.broadcast_to(np.where(j >= i, 0.0, NEG).astype(np.float32), (NSA_HEADS, TQS, TKI)))
    tb = jnp.stack([_toeplitz(gd_b, 0, TQS, TKI, NEG), _toeplitz(gd_b, TKI, TQS, TKI), edge])
    tb = tb.reshape(3, NSA_KV, NSA_HPG, TQS, TKI).transpose(1, 0, 4, 2, 3).reshape(NSA_KV, 3, TKI, cols) * LOG2E
    nk = s // TKI
    e = np.zeros((nk, TKI, SEL_LANES), np.float32)
    for kj in range(nk):
        for k in range(TKI):
            e[kj, k, kj * (TKI // SEL_BLOCK) + k // SEL_BLOCK] = 1.0
    e = jnp.asarray(e).astype(BF16)
    acc_rows = NSA_HD + ONES_ROWS
    return pl.pallas_call(
        _nsa_sw_kernel,
        grid=(b, NSA_KV, s // TQS),
        in_specs=[pl.BlockSpec((None, TQS, 256), lambda bi, g, qi: (bi, qi, g)),
                  pl.BlockSpec((None, s, LANE), lambda bi, g, qi: (bi, 0, g)),
                  pl.BlockSpec((None, s, LANE), lambda bi, g, qi: (bi, 0, g)),
                  pl.BlockSpec((None, nk, NSA_HD, TKI), lambda bi, g, qi: (bi, 0, g, 0)),
                  pl.BlockSpec((None, nk, NSA_HD, TKI), lambda bi, g, qi: (bi, 0, g, 0)),
                  pl.BlockSpec((None, None, TQS, SEL_LANES), lambda bi, g, qi: (bi, g, qi, 0)),
                  pl.BlockSpec(e.shape, lambda bi, g, qi: (0, 0, 0)),
                  pl.BlockSpec((None, 3, TKI, cols), lambda bi, g, qi: (g, 0, 0, 0))],
        out_specs=[pl.BlockSpec((None, TQS, 256), lambda bi, g, qi: (bi, qi, g)),
                   pl.BlockSpec((None, TQS, 256), lambda bi, g, qi: (bi, qi, g))],
        out_shape=[jax.ShapeDtypeStruct((b, s, 512), F32), jax.ShapeDtypeStruct((b, s, 512), F32)],
        scratch_shapes=_chain_scratch(1, acc_rows, cols) + _chain_scratch(1, acc_rows, cols),
        compiler_params=_cparams(("arbitrary", "arbitrary", "arbitrary")),
        name="nsa_sel_win",
    )(qb, ksel, kwin, selvt, winvt, sel, e, tb)


def _layer_norm(x, g, b):
    mu = jnp.mean(x, axis=-1, keepdims=True)
    xc = x - mu
    var = jnp.mean(xc * xc, axis=-1, keepdims=True)
    return xc * lax.rsqrt(var + LN_EPS) * g + b


def _gate_expand():
    e = np.zeros((3, LANE, NSA_HEADS * NSA_HD), np.float32)
    for h in range(NSA_HEADS):
        for j in range(3):
            e[j, h * 3 + j, h * NSA_HD:(h + 1) * NSA_HD] = 1.0
    return jnp.asarray(e).astype(BF16)


def _tail1_kernel(oa_ref, oc_ref, os_ref, ow_ref, gate_ref, ma_ref, mb_ref, x_ref, ada_ref,
                  wa_ref, wb_ref, wo_ref, sub_ref, eg_ref, g1_ref, b1_ref, o_ref):
    oa = oa_ref[...]
    parts = []
    for h in range(DA_HEADS):
        of = oa[:, h * DA_VD:(h + 1) * DA_VD]
        rr = lax.rsqrt(jnp.mean(of * of, axis=-1, keepdims=True) + RMS_EPS)
        parts.append(of * rr * sub_ref[...] * (1.0 - LAM_INIT))
    oan = jnp.concatenate(parts, axis=1)
    sg = jax.nn.sigmoid(gate_ref[...])
    ob = (_mm3(sg, eg_ref[0]) * oc_ref[...] + _mm3(sg, eg_ref[1]) * os_ref[...]
          + _mm3(sg, eg_ref[2]) * ow_ref[...])
    y = (jax.nn.sigmoid(ma_ref[...]) * _mm(oan, wa_ref[...])
         + jax.nn.sigmoid(mb_ref[...]) * _mm(ob, wb_ref[...]))
    z = ALPHA * x_ref[...] + ada_ref[2] * _mm(y, wo_ref[...])
    o_ref[...] = _layer_norm(z, g1_ref[...], b1_ref[...])


def _tail1(oa, oc, os_, ow, gate, ma, mb, x2d, ada, wa, wb, wo, sub, eg, g1, b1, tm, tiles_per_group):
    m = x2d.shape[0]
    row = lambda w: pl.BlockSpec((tm, w), lambda i: (i, 0))
    full = lambda a: pl.BlockSpec(a.shape, lambda i: (0,) * a.ndim)
    return pl.pallas_call(
        _tail1_kernel,
        grid=(m // tm,),
        in_specs=[row(512), row(512), row(512), row(512), row(128), row(1024), row(1024), row(1024),
                  _ada_spec(ada, tm, tiles_per_group),
                  full(wa), full(wb), full(wo), full(sub), full(eg), full(g1), full(b1)],
        out_specs=row(1024),
        out_shape=jax.ShapeDtypeStruct((m, D_MODEL), F32),
        compiler_params=_cparams(("arbitrary",)),
        name="tail_merge",
    )(oa, oc, os_, ow, gate, ma, mb, x2d, ada, wa, wb, wo, sub, eg, g1, b1)


def _tail2_kernel(x_ref, ada_ref, wu_ref, wd_ref, g2_ref, b2_ref, o_ref, h_scr, acc):
    f = pl.program_id(1)

    @pl.when(f == 0)
    def _():
        h_scr[...] = (x_ref[...] * (1.0 + ada_ref[4]) + ada_ref[3]).astype(BF16)
        acc[...] = jnp.zeros(acc.shape, F32)

    u = jnp.maximum(jnp.dot(h_scr[...], wu_ref[...], preferred_element_type=F32), 0.0)
    acc[...] += _mm(u * u, wd_ref[...])

    @pl.when(f == pl.num_programs(1) - 1)
    def _():
        z = ALPHA * x_ref[...] + ada_ref[5] * acc[...]
        o_ref[...] = _layer_norm(z, g2_ref[...], b2_ref[...])


def _tail2(x1, ada, wu, wd, g2, b2, tm, tiles_per_group):
    m = x1.shape[0]
    tf = 1024
    return pl.pallas_call(
        _tail2_kernel,
        grid=(m // tm, D_FF // tf),
        in_specs=[pl.BlockSpec((tm, D_MODEL), lambda i, f: (i, 0)),
                  _ada_spec(ada, tm, tiles_per_group),
                  pl.BlockSpec((D_MODEL, tf), lambda i, f: (0, f)),
                  pl.BlockSpec((tf, D_MODEL), lambda i, f: (f, 0)),
                  pl.BlockSpec((1, D_MODEL), lambda i, f: (0, 0)),
                  pl.BlockSpec((1, D_MODEL), lambda i, f: (0, 0))],
        out_specs=pl.BlockSpec((tm, D_MODEL), lambda i, f: (i, 0)),
        out_shape=jax.ShapeDtypeStruct((m, D_MODEL), F32),
        scratch_shapes=[pltpu.VMEM((tm, D_MODEL), BF16), pltpu.VMEM((tm, D_MODEL), F32)],
        compiler_params=_cparams(("arbitrary", "arbitrary")),
        name="tail_mlp",
    )(x1, ada, wu, wd, g2, b2)


def _da_decode_kernel(pt_ref, *refs):
    pages = refs[:PP]
    q_ref, kn_ref, vn_ref, bl_ref, b0_ref, lam_ref, o_ref, m_ref, l_ref, a_ref = refs[PP:]
    j = pl.program_id(1)
    last = j == pl.num_programs(1) - 1
    scale = DA_HD ** -0.5
    rows_pg = PAGE * 2 * DA_HEADS

    @pl.when(j == 0)
    def _():
        m_ref[...] = jnp.full(m_ref.shape, NEG, F32)
        l_ref[...] = jnp.zeros(l_ref.shape, F32)
        a_ref[...] = jnp.zeros(a_ref.shape, F32)

    q = q_ref[...]
    row = lax.broadcasted_iota(jnp.int32, (8, rows_pg), 0)
    col = lax.broadcasted_iota(jnp.int32, (8, rows_pg), 1)
    cmask = jnp.where((col & 7) == lax.shift_right_logical(row, 1), 0.0, NEG)
    scores = []
    xs = []
    for k in range(PP):
        x = pages[k][...].astype(BF16)
        sc = _mm_nt(q, x) * scale + cmask
        if k == PP - 1:
            sc = sc + jnp.where(last, bl_ref[...], 0.0)
        scores.append(sc)
        xs.append(x)
    s = jnp.concatenate(scores, axis=1)
    m_old = m_ref[...]
    m_new = jnp.maximum(m_old, jnp.max(s, axis=-1, keepdims=True))
    p = jnp.exp(s - m_new)
    alpha = jnp.exp(m_old - m_new)
    l_ref[...] = alpha * l_ref[...] + jnp.sum(p, axis=-1, keepdims=True)
    acc = alpha * a_ref[...]
    for k in range(PP):
        pv = pltpu.roll(p[:, k * rows_pg:(k + 1) * rows_pg], DA_HEADS, 1)
        acc = acc + _mm(pv, xs[k])
    a_ref[...] = acc
    m_ref[...] = m_new

    @pl.when(last)
    def _():
        s_new = jnp.sum(q.astype(F32) * kn_ref[...], axis=-1, keepdims=True) * scale + b0_ref[:, 0:1]
        m_o = m_ref[...]
        m_n = jnp.maximum(m_o, s_new)
        p_new = jnp.exp(s_new - m_n)
        al = jnp.exp(m_o - m_n)
        raw = (al * a_ref[...] + p_new * vn_ref[...]) / (al * l_ref[...] + p_new)
        o_ref[...] = raw - _diff_lambda(lam_ref) * pltpu.roll(raw, 7, 0)


def _da_decode(page_table, cache_rows, q8, k_new, v_new, bl, b0, da_lambda):
    b, n_pages = page_table.shape
    rows_pg = PAGE * 2 * DA_HEADS
    page_spec = lambda k: pl.BlockSpec((rows_pg, LANE), lambda bi, j, pt: (pt[bi, j * PP + k], 0))
    per_row = lambda: pl.BlockSpec((None, 8, LANE), lambda bi, j, pt: (bi, 0, 0))
    grid_spec = pltpu.PrefetchScalarGridSpec(
        num_scalar_prefetch=1,
        grid=(b, n_pages // PP),
        in_specs=[page_spec(k) for k in range(PP)] + [
            per_row(), per_row(), per_row(),
            pl.BlockSpec((8, rows_pg), lambda bi, j, pt: (0, 0)),
            pl.BlockSpec((8, LANE), lambda bi, j, pt: (0, 0)),
            pl.BlockSpec((4, DA_HD), lambda bi, j, pt: (0, 0))],
        out_specs=per_row(),
        scratch_shapes=[pltpu.VMEM((8, 1), F32), pltpu.VMEM((8, 1), F32), pltpu.VMEM((8, LANE), F32)])
    return pl.pallas_call(
        _da_decode_kernel,
        grid_spec=grid_spec,
        out_shape=jax.ShapeDtypeStruct((b, 8, LANE), F32),
        compiler_params=_cparams(("arbitrary", "arbitrary")),
        name="da_decode",
    )(page_table, *([cache_rows] * PP), q8, k_new, v_new, bl, b0, da_lambda)


def _nsa_decode1_kernel(pt_ref, ab_hbm, q_ref, swa_ref, new_ref, bw_ref, pe_ref, w1c_ref, w2_ref, ovl_ref,
                        oc_ref, ow_ref, idx_ref, abuf, sem, *, n_pages):
    b = pl.program_id(0)
    copies = [pltpu.make_async_copy(ab_hbm.at[pt_ref[b, p]], abuf.at[p], sem) for p in range(n_pages)]
    for c in copies:
        c.start()
    for c in copies:
        c.wait()
    n_chunks = n_pages * (PAGE // CMP_STRIDE)
    kcvc = _cmp_stage2(abuf[...].reshape(n_chunks, 512), pe_ref, w1c_ref, w2_ref)
    scale = NSA_HD ** -0.5
    swa = swa_ref[...].astype(BF16)
    n_win = swa.shape[0]
    for g in range(NSA_KV):
        q = q_ref[g]
        s = _mm_nt(q, kcvc[:, g * 256:g * 256 + 128]) * scale
        n = lax.broadcasted_iota(jnp.int32, s.shape, 1)
        p = _masked_softmax(s, n < n_chunks - 1)
        oc_ref[g] = _mm(p, kcvc[:, g * 256 + 128:g * 256 + 256])
        psum = jnp.sum(p[0:NSA_HPG], axis=0, keepdims=True)
        imp = _mm3(jnp.broadcast_to(psum, (8, n_chunks)), ovl_ref[...])
        blk = lax.broadcasted_iota(jnp.int32, imp.shape, 1)
        n_blk = n_chunks * CMP_STRIDE // SEL_BLOCK
        forced = (blk == 0) | (blk == n_blk - 1)
        score = jnp.where(blk < n_blk, jnp.where(forced, SEL_BIG, imp), REMOVED)
        _, picks = _topk_select(score, N_SEL - 1)
        idx_ref[g] = picks.astype(jnp.int32)
        new = new_ref[g:g + 1, :]
        sw = _mm_nt(q, swa) * scale + bw_ref[g][:, 0:n_win]
        s_new = (jnp.sum(q.astype(F32) * new, axis=-1, keepdims=True) * scale
                 + bw_ref[g][:, n_win:n_win + 1])
        m = jnp.maximum(jnp.max(sw, axis=-1, keepdims=True), s_new)
        e = jnp.exp(sw - m)
        e_new = jnp.exp(s_new - m)
        den = jnp.sum(e, axis=-1, keepdims=True) + e_new
        ow_ref[g] = (_mm(e, swa) + e_new * new) / den


def _nsa_decode1(page_table, ab_pool, qc, cache_swa, kvw_new, bw, pe8, w1c, w2, ovl):
    b, n_pages = page_table.shape
    full = lambda a: pl.BlockSpec(a.shape, lambda bi, pt: (0,) * a.ndim)
    out4 = lambda: pl.BlockSpec((None, NSA_KV, 8, LANE), lambda bi, pt: (bi, 0, 0, 0))
    grid_spec = pltpu.PrefetchScalarGridSpec(
        num_scalar_prefetch=1,
        grid=(b,),
        in_specs=[pl.BlockSpec(memory_space=pl.ANY),
                  pl.BlockSpec((None, NSA_KV, 8, LANE), lambda bi, pt: (bi, 0, 0, 0)),
                  pl.BlockSpec((None, cache_swa.shape[1], LANE), lambda bi, pt: (bi, 0, 0)),
                  pl.BlockSpec((None, NSA_KV, LANE), lambda bi, pt: (bi, 0, 0)),
                  full(bw), full(pe8), full(w1c), full(w2), full(ovl)],
        out_specs=[out4(), out4(), out4()],
        scratch_shapes=[pltpu.VMEM((n_pages, PAGE // CMP_STRIDE, 512), F32), pltpu.SemaphoreType.DMA(())])
    return pl.pallas_call(
        functools.partial(_nsa_decode1_kernel, n_pages=n_pages),
        grid_spec=grid_spec,
        out_shape=[jax.ShapeDtypeStruct((b, NSA_KV, 8, LANE), F32),
                   jax.ShapeDtypeStruct((b, NSA_KV, 8, LANE), F32),
                   jax.ShapeDtypeStruct((b, NSA_KV, 8, LANE), jnp.int32)],
        compiler_params=_cparams(("arbitrary",)),
        name="nsa_decode_cmp_win",
    )(page_table, ab_pool, qc, cache_swa, kvw_new, bw, pe8, w1c, w2, ovl)


def _nsa_decode2_kernel(pt_ref, idx_ref, *refs, n_blk, n_pick):
    blks = refs[:n_pick]
    q_ref, new_ref, bs_ref, o_ref = refs[n_pick:]
    b = pl.program_id(0)
    g = pl.program_id(1)
    scale = NSA_HD ** -0.5
    q = q_ref[...]
    cols = blks[0].shape[0]
    scores = []
    xs = []
    for k in range(n_pick):
        x = blks[k][...].astype(BF16)
        blk = idx_ref[b, g, k]
        bias = (jnp.where(blk == n_blk - 1, bs_ref[:, cols:2 * cols], 0.0)
                + jnp.where(blk == n_blk - 2, bs_ref[:, 2 * cols:3 * cols], 0.0))
        scores.append(_mm_nt(q, x) * scale + bs_ref[:, 0:cols] + bias)
        xs.append(x)
    new = new_ref[...]
    s_new = jnp.sum(q.astype(F32) * new, axis=-1, keepdims=True) * scale + bs_ref[:, 3 * cols:3 * cols + 1]
    s = jnp.concatenate(scores, axis=1)
    m = jnp.maximum(jnp.max(s, axis=-1, keepdims=True), s_new)
    p = jnp.exp(s - m)
    p_new = jnp.exp(s_new - m)
    acc = p_new * new
    for k in range(n_pick):
        acc = acc + _mm(p[:, k * cols:(k + 1) * cols], xs[k])
    o_ref[...] = acc / (jnp.sum(p, axis=-1, keepdims=True) + p_new)


def _nsa_decode2(page_table, idx, cache_rows, qc, nsa_new, bs):
    b = page_table.shape[0]
    n_pick = idx.shape[2]
    rows_blk = SEL_BLOCK * 2 * NSA_KV

    def blk_spec(k):
        def blk_map(bi, g, pt, ix):
            blk = ix[bi, g, k]
            return (pt[bi, lax.shift_right_logical(blk, 1)] * 2 + (blk & 1), 0)
        return pl.BlockSpec((rows_blk, LANE), blk_map)

    grid_spec = pltpu.PrefetchScalarGridSpec(
        num_scalar_prefetch=2,
        grid=(b, NSA_KV),
        in_specs=[blk_spec(k) for k in range(n_pick)] + [
            pl.BlockSpec((None, None, 8, LANE), lambda bi, g, pt, ix: (bi, g, 0, 0)),
            pl.BlockSpec((None, None, 1, LANE), lambda bi, g, pt, ix: (bi, g, 0, 0)),
            pl.BlockSpec((None, 8, bs.shape[2]), lambda bi, g, pt, ix: (g, 0, 0))],
        out_specs=pl.BlockSpec((None, None, 8, LANE), lambda bi, g, pt, ix: (bi, g, 0, 0)))
    return pl.pallas_call(
        functools.partial(_nsa_decode2_kernel, n_blk=page_table.shape[1] * PAGE // SEL_BLOCK, n_pick=n_pick),
        grid_spec=grid_spec,
        out_shape=jax.ShapeDtypeStruct((b, NSA_KV, 8, LANE), F32),
        compiler_params=_cparams(("arbitrary", "arbitrary")),
        name="nsa_decode_sel",
    )(page_table, idx, *([cache_rows] * n_pick), qc, nsa_new, bs)


def _prompt_mixers(pr, b, s, gd_a, gd_b, da_lambda, w1, w2, pe8, w1c):
    sh = lambda a: a.reshape(b, s, a.shape[-1])
    nk = s // TKI
    vt = lambda a: a.reshape(b, nk, a.shape[1], TKI)
    o_a = _da_prompt(sh(pr["qa2"]), sh(pr["dak"]), vt(pr["davt"]), gd_a, da_lambda)
    n_chunks = s // CMP_STRIDE
    ab = _cmp_stage1(pr["nsa"].reshape(b * n_chunks, CMP_STRIDE * 512), w1, 2 * LANE)
    kcvc = _cmp_stage2_prompt(ab.reshape(b, n_chunks, 512), pe8, w1c, w2)
    o_c, sel = _nsa_cmp_prompt(sh(pr["qb"]), kcvc, _ovl_t(n_chunks))
    o_s, o_w = _nsa_sw_prompt(sh(pr["qb2"]), sh(pr["ksel"]), sh(pr["kwin"]), vt(pr["selvt"]), vt(pr["winvt"]),
                              sel, gd_b)
    flat = lambda a: a.reshape(b * s, a.shape[-1])
    return flat(o_a), flat(o_c), flat(o_s), flat(o_w)


def _sample_mixers(pr, page_table, cache_da, cache_nsa, cache_swa, gd_a, gd_b, da_lambda, w1, w2, pe8, w1c):
    b, n_pages = page_table.shape
    past = n_pages * PAGE
    n_pool = cache_da.shape[0]
    da_rows = cache_da.reshape(n_pool, PAGE, DA_HEADS, 2, LANE).transpose(0, 1, 3, 2, 4).reshape(-1, LANE)
    nsa_rows = cache_nsa.reshape(n_pool, PAGE, NSA_KV, 2, LANE).transpose(0, 1, 3, 2, 4).reshape(-1, LANE)
    qa = pr["qa"].reshape(b, DA_HEADS, 2, DA_HD)
    q8 = jnp.zeros((b, DA_HEADS, 2, 2, DA_HD), BF16)
    for c in range(2):
        q8 = q8.at[:, :, c, c, :].set(qa[:, :, c])
    q8 = q8.reshape(b, 8, LANE)
    da_new = pr["da"].reshape(b, DA_HEADS, 2, LANE)
    k_new = jnp.repeat(da_new[:, :, 0], 2, axis=1)
    v_new = jnp.repeat(da_new[:, :, 1], 2, axis=1)
    gda8 = jnp.repeat(gd_a, 2, axis=0)
    bl = jnp.repeat(gda8[:, PAGE - jnp.arange(PAGE)], 2 * DA_HEADS, axis=1)
    b0 = jnp.broadcast_to(gda8[:, 0:1], (8, LANE))
    o_a = _da_decode(page_table, da_rows, q8, k_new, v_new, bl, b0, da_lambda)
    o_a = o_a[:, 0::2, :].reshape(b, 512)
    chunks = PAGE // CMP_STRIDE
    ab_pool = _cmp_stage1_rows(nsa_rows, w1).reshape(n_pool, chunks, 512)
    qb = pr["qb"].reshape(b, NSA_KV, NSA_HPG, NSA_HD)
    qc = jnp.zeros((b, NSA_KV, 8, LANE), BF16).at[:, :, :NSA_HPG, :NSA_HD].set(qb)
    gdb = jnp.pad(gd_b.reshape(NSA_KV, NSA_HPG, -1), ((0, 0), (0, 8 - NSA_HPG), (0, 0)))
    own = jnp.arange(NSA_KV)[:, None, None]
    n_win = cache_swa.shape[1]
    grp_w = jnp.arange(n_win * NSA_KV)[None, None, :] % NSA_KV
    bw = jnp.where(grp_w == own, jnp.repeat(gdb[:, :, n_win - jnp.arange(n_win)], NSA_KV, axis=2), NEG)
    bw = jnp.concatenate([bw, jnp.broadcast_to(gdb[:, :, 0:1], (NSA_KV, 8, LANE))], axis=2)
    o_c, o_w, idx = _nsa_decode1(page_table, ab_pool, qc, cache_swa.reshape(b, n_win * NSA_KV, LANE),
                                 pr["kvw"].reshape(b, NSA_KV, LANE), bw, pe8, w1c, w2,
                                 _ovl_t(past // CMP_STRIDE))
    idx = idx[:, :, 0, :N_SEL - 1]
    rows_blk = SEL_BLOCK * 2 * NSA_KV
    kind = jnp.arange(rows_blk)[None, None, :] % (2 * NSA_KV)
    keep = kind == NSA_KV + own
    bs = jnp.concatenate([
        jnp.where(keep, 0.0, NEG) * jnp.ones((1, 8, 1), F32),
        jnp.repeat(gdb[:, :, SEL_BLOCK - jnp.arange(SEL_BLOCK)], 2 * NSA_KV, axis=2),
        jnp.repeat(gdb[:, :, 2 * SEL_BLOCK - jnp.arange(SEL_BLOCK)], 2 * NSA_KV, axis=2),
        jnp.broadcast_to(gdb[:, :, 0:1], (NSA_KV, 8, LANE))], axis=2)
    sel_new = pr["nsa"].reshape(b, NSA_KV, 2, LANE)[:, :, 1:2, :]
    o_s = _nsa_decode2(page_table, idx, nsa_rows, qc, sel_new, bs)
    o_c = o_c[:, :, :NSA_HPG, :NSA_HD].reshape(b, 512)
    o_w = o_w[:, :, :NSA_HPG, NSA_HD:].reshape(b, 512)
    o_s = o_s[:, :, :NSA_HPG, NSA_HD:].reshape(b, 512)
    return o_a, o_c, o_s, o_w


def kernel(x_prompt, x_sample, cache_da_kv, cache_nsa_kv, cache_swa_kv, page_table, c_prompt, c_sample, rel_bias, w_ada, b_ada, w_in, da_lambda, da_subln, cmp_pe, cmp_w1, cmp_w2, w_br_a, w_br_b, w_out, ln1_g, ln1_b, w_up, w_down, ln2_g, ln2_b):
    bp, s, _ = x_prompt.shape
    bs_ = x_sample.shape[0]
    w_perm, w_vt = _perm_w_in(w_in[0])
    w1, w2 = _cmp_weights(cmp_w1[0], cmp_w2[0])
    pe8 = jnp.broadcast_to(cmp_pe[0].reshape(2, 1, CMP_LEN * NSA_HD), (2, 8, CMP_LEN * NSA_HD))
    w1c = cmp_w1[0].reshape(2, CMP_LEN * NSA_HD, CMP_HID)
    gd_a = _dist_bias(rel_bias[:, :DA_HEADS], 1024)
    gd_b = _dist_bias(rel_bias[:, DA_HEADS:], 1024)
    wa, wb, wo = w_br_a[0].astype(BF16), w_br_b[0].astype(BF16), w_out[0].astype(BF16)
    wu, wd = w_up[0].astype(BF16), w_down[0].astype(BF16)
    sub = da_subln[0].reshape(1, DA_VD)
    eg = _gate_expand()
    g1, b1 = ln1_g[0].reshape(1, D_MODEL), ln1_b[0].reshape(1, D_MODEL)
    g2, b2 = ln2_g[0].reshape(1, D_MODEL), ln2_b[0].reshape(1, D_MODEL)
    lam = da_lambda[0]

    n_c = bp + bs_
    c_all = jnp.pad(jnp.concatenate([c_prompt, c_sample], 0), ((0, (-n_c) % 8), (0, 0)))
    ada = _ada(c_all, w_ada[0], b_ada[0])[:n_c].reshape(n_c, 6, D_MODEL)
    ada_p = jnp.transpose(ada[:bp], (1, 0, 2)).reshape(6, bp, 1, D_MODEL)
    ada_s = jnp.transpose(ada[bp:], (1, 0, 2)).reshape(6, 1, bs_, D_MODEL)

    def tail(mix, pr, x2d, ada_x, tm, tpg):
        o_a, o_c, o_s, o_w = mix
        x1 = _tail1(o_a, o_c, o_s, o_w, pr["gate"], pr["ma"], pr["mb"], x2d, ada_x,
                    wa, wb, wo, sub, eg, g1, b1, tm, tpg)
        return _tail2(x1, ada_x, wu, wd, g2, b2, tm, tpg)

    xp = x_prompt.reshape(bp * s, D_MODEL)
    pr_p = _proj(xp, ada_p, w_perm, w_vt, TKI, s // TKI)
    mix_p = _prompt_mixers(pr_p, bp, s, gd_a, gd_b, lam, w1, w2, pe8, w1c)
    tm_t = 512
    y_p = tail(mix_p, pr_p, xp, ada_p, tm_t, s // tm_t).reshape(bp, s, D_MODEL)
    xs = x_sample.reshape(bs_, D_MODEL)
    pr_s = _proj(xs, ada_s, w_perm, None, bs_, 1)
    mix_s = _sample_mixers(pr_s, page_table, cache_da_kv[0], cache_nsa_kv[0], cache_swa_kv[0],
                           gd_a, gd_b, lam, w1, w2, pe8, w1c)
    y_s = tail(mix_s, pr_s, xs, ada_s, bs_, 1).reshape(bs_, 1, D_MODEL)

    win = min(WINDOW, s)
    new_da_p = pr_p["da"].reshape(1, bp, s, DA_HEADS, 4 * DA_HD)
    new_nsa_p = pr_p["nsa"].reshape(1, bp, s, NSA_KV, 4 * NSA_HD)
    new_swa_p = pr_p["kvw"].reshape(bp, s, NSA_KV, 2 * NSA_HD)[None, :, s - win:]
    new_da_s = pr_s["da"].reshape(1, bs_, 1, DA_HEADS, 4 * DA_HD)
    new_nsa_s = pr_s["nsa"].reshape(1, bs_, 1, NSA_KV, 4 * NSA_HD)
    new_swa_s = jnp.concatenate([cache_swa_kv[0][:, 1:], pr_s["kvw"].reshape(bs_, 1, NSA_KV, 2 * NSA_HD)],
                                axis=1)[None]
    return (y_p, y_s, new_da_p, new_nsa_p, new_swa_p, new_da_s, new_nsa_s, new_swa_s)
```

```python
import functools
import math

import numpy as np
import jax
import jax.numpy as jnp
from jax import lax
from jax.experimental import pallas as pl
from jax.experimental.pallas import tpu as pltpu

F32 = jnp.float32
BF16 = jnp.bfloat16

D_MODEL = 1024
PAGE = 128
DA_HEADS = 4
DA_HD = 64
DA_VD = 128
NSA_HEADS = 8
NSA_KV = 2
NSA_HPG = 4
NSA_HD = 64
CMP_STRIDE = 16
CMP_LEN = 32
CMP_HID = 64
SEL_BLOCK = 64
N_SEL = 16
WINDOW = 512
D_FF = 4096
N_BUCKETS = 32
MAX_DIST = 128
DEPTH = 1
ALPHA = (2 * DEPTH) ** 0.25
LN_EPS = 1e-5
RMS_EPS = 1e-5
NEG = -1e30
SEL_BIG = 1e9
LAM_INIT = 0.8 - 0.6 * math.exp(-0.3 * 0)
SPLIT_SIZES = (512, 512, 512, 512, 256, 256, 256, 24, 1024, 1024)

LANE = 128
VMEM_LIMIT = 56 * 1024 * 1024
TQ = 128
TKI = 256
TQS = 256
ONES_ROWS = 16
TDA = 512
SEL_LANES = 128
REMOVED = -3e38
PP = 16


def _cparams(sem):
    return pltpu.CompilerParams(dimension_semantics=sem, vmem_limit_bytes=VMEM_LIMIT)


def _mm(a, b):
    return jnp.dot(a.astype(BF16), b.astype(BF16), preferred_element_type=F32)


def _mm_nt(a, b):
    return lax.dot_general(a.astype(BF16), b.astype(BF16), (((1,), (1,)), ((), ())),
                           preferred_element_type=F32)


def _mm3(x, w):
    hi = x.astype(BF16)
    r = x - hi.astype(F32)
    mid = r.astype(BF16)
    lo = (r - mid.astype(F32)).astype(BF16)
    return (jnp.dot(hi, w, preferred_element_type=F32) + jnp.dot(mid, w, preferred_element_type=F32)
            + jnp.dot(lo, w, preferred_element_type=F32))


def _masked_softmax(s, valid):
    l = jnp.where(valid, s, NEG)
    m = jnp.max(l, axis=-1, keepdims=True)
    e = jnp.where(valid, jnp.exp(l - m), 0.0)
    return e / jnp.maximum(jnp.sum(e, axis=-1, keepdims=True), 1e-30)


def _t5_bucket(dist):
    n = jnp.maximum(dist, 0)
    max_exact = N_BUCKETS // 2
    nf = jnp.maximum(n, 1).astype(F32)
    large = max_exact + (jnp.log(nf / max_exact) / math.log(MAX_DIST / max_exact)
                         * (N_BUCKETS - max_exact)).astype(jnp.int32)
    return jnp.where(n < max_exact, n, jnp.minimum(large, N_BUCKETS - 1))


def _dist_bias(tbl, n):
    d = jnp.arange(n, dtype=jnp.int32)
    g = tbl[_t5_bucket(d)] - tbl[N_BUCKETS - 1][None, :]
    return jnp.transpose(g)


def _toeplitz(gd, offset, rows, cols, below=0.0):
    heads, n = gd.shape
    length = rows + cols - 1
    assert offset + rows <= n
    lo = offset - cols + 1
    hvec = gd[:, max(lo, 0):offset + rows]
    if lo < 0:
        hvec = jnp.concatenate([jnp.full((heads, -lo), below, gd.dtype), hvec], axis=1)
    rev = jnp.concatenate([hvec[:, ::-1], jnp.zeros((heads, 1), gd.dtype)], axis=1)
    flat = jnp.tile(rev, (1, rows))[:, :rows * length].reshape(heads, rows, length)
    return flat[:, :, rows - 1:rows - 1 + cols]


def _diff_lambda(lam_ref):
    l = lam_ref[...]
    a = jnp.sum(l[0:1, :] * l[1:2, :], axis=-1, keepdims=True)
    b = jnp.sum(l[2:3, :] * l[3:4, :], axis=-1, keepdims=True)
    return jnp.exp(a) - jnp.exp(b) + LAM_INIT


def _ada_kernel(c_ref, w_ref, b_ref, o_ref):
    c = c_ref[...]
    o_ref[...] = _mm(c * jax.nn.sigmoid(c), w_ref[...]) + b_ref[...]


def _ada(c, w_ada, b_ada):
    m = c.shape[0]
    n = w_ada.shape[1]
    tn = 512
    return pl.pallas_call(
        _ada_kernel,
        grid=(n // tn,),
        in_specs=[pl.BlockSpec((m, D_MODEL), lambda j: (0, 0)),
                  pl.BlockSpec((D_MODEL, tn), lambda j: (0, j)),
                  pl.BlockSpec((1, tn), lambda j: (0, j))],
        out_specs=pl.BlockSpec((m, tn), lambda j: (0, j)),
        out_shape=jax.ShapeDtypeStruct((m, n), F32),
        compiler_params=_cparams(("arbitrary",)),
        name="ada",
    )(c, w_ada, b_ada.reshape(1, n))


PROJ_GROUPS = (("qa", 512, BF16), ("da", 1024, F32), ("qb", 512, BF16), ("nsa", 512, F32), ("kvw", 256, F32),
               ("gate", 128, F32), ("ma", 1024, F32), ("mb", 1024, F32))
PROJ_KEY_GROUPS = (("ksel", 256), ("kwin", 256))
STATE_ROWS = {"da": DA_HEADS, "nsa": NSA_KV}
LOG2E = 1.4426950408889634
Q_SCALE = {"qa": DA_HD ** -0.5 * LOG2E, "qb": NSA_HD ** -0.5 * LOG2E}
PROJ_W = sum(w for _, w, _ in PROJ_GROUPS)
PROJ_W_KV = PROJ_W + sum(w for _, w in PROJ_KEY_GROUPS)
PROJ_VT = (("davt", DA_HEADS * DA_VD), ("selvt", NSA_KV * NSA_HD), ("winvt", NSA_KV * NSA_HD))


def _perm_w_in(w_in):
    parts = jnp.split(w_in, np.cumsum(SPLIT_SIZES)[:-1].tolist(), axis=1)
    qa, ka, va, qn, kvc, kvs, kvw, gb, ma, mb = parts
    da = jnp.concatenate([jnp.concatenate([ka[:, h * 128:(h + 1) * 128], va[:, h * 128:(h + 1) * 128]], 1)
                          for h in range(DA_HEADS)], 1)
    nsa = jnp.concatenate([jnp.concatenate([kvc[:, g * 128:(g + 1) * 128], kvs[:, g * 128:(g + 1) * 128]], 1)
                           for g in range(NSA_KV)], 1)
    gate = jnp.pad(gb, ((0, 0), (0, LANE - gb.shape[1])))
    keys2 = lambda kv: jnp.concatenate([kv[:, g * 128:g * 128 + 64] for g in range(NSA_KV) for _ in range(2)], 1)
    vals = lambda kv: jnp.concatenate([kv[:, g * 128 + 64:(g + 1) * 128] for g in range(NSA_KV)], 1)
    cols = dict(qa=qa, da=da, qb=qn, nsa=nsa, kvw=kvw, gate=gate, ma=ma, mb=mb, ksel=keys2(kvs), kwin=keys2(kvw))
    names = [n for n, _, _ in PROJ_GROUPS] + [n for n, _ in PROJ_KEY_GROUPS]
    w_perm = jnp.concatenate([cols[n] for n in names], 1).astype(BF16)
    w_vt = jnp.transpose(jnp.concatenate([va, vals(kvs), vals(kvw)], 1)).astype(BF16)
    return w_perm, w_vt


def _proj_kernel(x_ref, ada_ref, w_ref, *refs, with_kv):
    h = (x_ref[...] * (1.0 + ada_ref[1]) + ada_ref[0]).astype(BF16)
    o_refs = refs[1:] if with_kv else refs
    off = 0
    k = 0
    for name, width, dt in PROJ_GROUPS:
        acc = jnp.dot(h, w_ref[:, off:off + width], preferred_element_type=F32)
        if with_kv and name in STATE_ROWS:
            units = STATE_ROWS[name]
            tm = acc.shape[0]
            for u in range(units):
                for half in range(2):
                    o_refs[k][pl.ds(half * units + u, tm, stride=2 * units), :] = (
                        acc[:, (u * 2 + half) * LANE:(u * 2 + half + 1) * LANE])
        else:
            o_refs[k][...] = acc.astype(dt)
        k += 1
        if with_kv and name in Q_SCALE:
            o_refs[k][...] = (acc * Q_SCALE[name]).astype(BF16)
            k += 1
        if with_kv and name == "da":
            o_refs[k][...] = jnp.concatenate([acc[:, hd * 256:hd * 256 + 128] for hd in range(DA_HEADS)],
                                             axis=1).astype(BF16)
            k += 1
        off += width
    if with_kv:
        for _, width in PROJ_KEY_GROUPS:
            o_refs[k][...] = jnp.dot(h, w_ref[:, off:off + width], preferred_element_type=F32).astype(BF16)
            k += 1
            off += width
        vt = _mm_nt(refs[0][...], h)
        r0 = 0
        for _, rows in PROJ_VT:
            o_refs[k][...] = vt[r0:r0 + rows].astype(BF16)
            k += 1
            r0 += rows


def _ada_spec(ada, tm, tiles_per_group):
    r = ada.shape[2]
    return pl.BlockSpec((6, None, r, D_MODEL), lambda i, *_: (0, i // tiles_per_group, 0, 0))


def _proj(x2d, ada, w_perm, w_vt, tm, tiles_per_group):
    m = x2d.shape[0]
    with_kv = w_vt is not None
    outs = []
    for name, w, dt in PROJ_GROUPS:
        if with_kv and name in STATE_ROWS:
            rows_pos = w // LANE
            outs.append((name, (m * rows_pos, LANE), (tm * rows_pos, LANE), dt))
        else:
            outs.append((name, (m, w), (tm, w), dt))
        if with_kv and name in Q_SCALE:
            outs.append((name + "2", (m, w), (tm, w), BF16))
        if with_kv and name == "da":
            outs.append(("dak", (m, 512), (tm, 512), BF16))
    in_specs = [pl.BlockSpec((tm, D_MODEL), lambda i: (i, 0)),
                _ada_spec(ada, tm, tiles_per_group),
                pl.BlockSpec((D_MODEL, PROJ_W_KV if with_kv else PROJ_W), lambda i: (0, 0))]
    args = [x2d, ada, w_perm]
    if with_kv:
        assert tm == TKI
        outs += [(n, (m, w), (tm, w), BF16) for n, w in PROJ_KEY_GROUPS]
        outs += [(n, (m // tm, r, tm), (None, r, tm), BF16) for n, r in PROJ_VT]
        in_specs.append(pl.BlockSpec(w_vt.shape, lambda i: (0, 0)))
        args.append(w_vt)
    res = pl.pallas_call(
        functools.partial(_proj_kernel, with_kv=with_kv),
        grid=(m // tm,),
        in_specs=in_specs,
        out_specs=[pl.BlockSpec(blk, (lambda i: (i, 0)) if len(blk) == 2 else (lambda i: (i, 0, 0)))
                   for _, _, blk, _ in outs],
        out_shape=[jax.ShapeDtypeStruct(shape, dt) for _, shape, _, dt in outs],
        compiler_params=_cparams(("arbitrary",)),
        name="proj",
    )(*args)
    return {n: o for (n, _, _, _), o in zip(outs, res)}


class _Chain:
    def __init__(self, s_bufs, p_bufs, al_bufs, m_ref, acc_ref, score_fn, vext_fn, bias_fn):
        self.s, self.p, self.al = s_bufs, p_bufs, al_bufs
        self.m, self.acc = m_ref, acc_ref
        self.score_fn, self.vext_fn, self.bias_fn = score_fn, vext_fn, bias_fn
        self.maps = m_ref.shape[0]
        self.cur = 0

    def prime(self, j):
        self.cur = 0
        self.m[...] = jnp.full(self.m.shape, NEG, F32)
        self.acc[...] = jnp.zeros(self.acc.shape, F32)
        self.p[1][...] = jnp.zeros(self.p[1].shape, BF16)
        self.al[1][...] = jnp.ones(self.al[1].shape, F32)
        self.score_fn(j, self.s[0])

    def _finish(self, k, buf, pv):
        self.acc[k] = self.acc[k] * self.al[buf][k, 0:1, :] + pv

    def step(self, j_prev, j_next, table=None, valid=None, pen=None, prev_valid=None):
        c, o = self.cur, 1 - self.cur
        vext = self.vext_fn(j_prev)
        if prev_valid is not None:
            vext = jnp.where(prev_valid, vext, jnp.zeros_like(vext))
        pv = [jnp.dot(vext, self.p[o][k], preferred_element_type=F32) for k in range(self.maps)]
        if j_next is not None:
            self.score_fn(j_next, self.s[o])
        for k in range(self.maps):
            for l0 in range(0, self.m.shape[2], LANE):
                cols = slice(l0, l0 + LANE)
                s = self.s[c][k, :, cols]
                if table is not None:
                    s = s + self.bias_fn(table, l0)
                if pen is not None:
                    s = s + pen
                m_cur = jnp.max(s, axis=0, keepdims=True)
                if valid is not None:
                    m_cur = jnp.where(valid, m_cur, NEG)
                m_old = self.m[k, 0:1, cols]
                m_new = jnp.maximum(m_old, m_cur)
                self.p[c][k, :, cols] = jnp.exp2(s - m_new).astype(BF16)
                self.al[c][k, :, cols] = jnp.broadcast_to(jnp.exp2(m_old - m_new), (8, LANE))
                self.m[k, :, cols] = jnp.broadcast_to(m_new, (8, LANE))
            self._finish(k, o, pv[k])
        self.cur = o

    def step_eager(self, j_cur, j_next, table=None, valid=None, pen=None):
        c, o = self.cur, 1 - self.cur
        if j_next is not None:
            self.score_fn(j_next, self.s[o])
        vext = self.vext_fn(j_cur)
        if valid is not None:
            vext = jnp.where(valid, vext, jnp.zeros_like(vext))
        for k in range(self.maps):
            ps, alphas = [], []
            for l0 in range(0, self.m.shape[2], LANE):
                cols = slice(l0, l0 + LANE)
                s = self.s[c][k, :, cols]
                if table is not None:
                    s = s + self.bias_fn(table, l0)
                if pen is not None:
                    s = s + pen
                m_cur = jnp.max(s, axis=0, keepdims=True)
                if valid is not None:
                    m_cur = jnp.where(valid, m_cur, NEG)
                m_old = self.m[k, 0:1, cols]
                m_new = jnp.maximum(m_old, m_cur)
                ps.append(jnp.exp2(s - m_new).astype(BF16))
                alphas.append(jnp.exp2(m_old - m_new))
                self.m[k, :, cols] = jnp.broadcast_to(m_new, (8, LANE))
            self.acc[k] = (self.acc[k] * jnp.concatenate(alphas, axis=1)
                           + jnp.dot(vext, jnp.concatenate(ps, axis=1), preferred_element_type=F32))
        self.cur = o

    def flush(self, j_prev):
        o = 1 - self.cur
        vext = self.vext_fn(j_prev)
        for k in range(self.maps):
            self._finish(k, o, jnp.dot(vext, self.p[o][k], preferred_element_type=F32))


def _chain_scratch(maps, acc_rows, queries):
    return [pltpu.VMEM((maps, TKI, queries), F32), pltpu.VMEM((maps, TKI, queries), F32),
            pltpu.VMEM((maps, TKI, queries), BF16), pltpu.VMEM((maps, TKI, queries), BF16),
            pltpu.VMEM((maps, 8, queries), F32), pltpu.VMEM((maps, 8, queries), F32),
            pltpu.VMEM((maps, 8, queries), F32), pltpu.VMEM((maps, acc_rows, queries), F32)]


def _key_rows(ref, j):
    return pl.ds(pl.multiple_of(j * TKI, TKI), TKI)


def _da_kernel(q_ref, k_ref, vt_ref, tb_ref, lam_ref, o_ref, s_a, s_b, p_a, p_b, al_a, al_b, m_ref, acc_ref):
    qi = pl.program_id(2)
    q = q_ref[...]
    lane = lax.broadcasted_iota(jnp.int32, q.shape, 1)
    zero = jnp.zeros_like(q)
    qt_maps = tuple(jnp.where(keep, q, zero).astype(F32).T.astype(BF16) for keep in (lane < DA_HD, lane >= DA_HD))
    ones = jnp.ones((ONES_ROWS, TKI), BF16)

    def scores(j, dst):
        kk = k_ref[_key_rows(k_ref, j), :]
        for c in range(2):
            dst[c] = jnp.dot(kk, qt_maps[c], preferred_element_type=F32)

    chain = _Chain((s_a, s_b), (p_a, p_b), (al_a, al_b), m_ref, acc_ref, scores,
                   lambda j: jnp.concatenate([vt_ref[j], ones], axis=0),
                   lambda table, l0: tb_ref[table, :, l0:l0 + LANE])
    n_far = jnp.maximum(2 * qi - 1, 0)
    j_d = 2 * qi
    j_s = jnp.maximum(2 * qi - 1, 0)
    chain.prime(jnp.where(n_far > 0, 0, j_d))

    def pair(i, carry):
        a = 2 * i
        b = jnp.minimum(a + 1, n_far - 1)
        chain.step_eager(a, b)
        chain.step_eager(b, jnp.where(a + 2 < n_far, a + 2, j_d), valid=a + 1 < n_far)
        return carry

    lax.fori_loop(0, lax.shift_right_logical(n_far + 1, 1), pair, 0)
    chain.step_eager(j_d, j_s, table=1)
    chain.step_eager(j_s, j_d + 1, table=0, pen=jnp.where(qi > 0, 0.0, NEG))
    chain.step_eager(j_d + 1, None, table=2)
    lam = _diff_lambda(lam_ref)
    o_t = (acc_ref[0, 0:DA_VD, :] / acc_ref[0, DA_VD:DA_VD + 1, :]
           - lam * (acc_ref[1, 0:DA_VD, :] / acc_ref[1, DA_VD:DA_VD + 1, :]))
    o_ref[...] = o_t.T


def _da_prompt(qa, dak, davt, gd_a, da_lambda):
    b, s, _ = qa.shape
    tb = jnp.stack([_toeplitz(gd_a, TKI, TDA, TKI), _toeplitz(gd_a, 0, TDA, TKI, NEG),
                    _toeplitz(gd_a, -TKI, TDA, TKI, NEG)], axis=1)
    tb = jnp.swapaxes(tb, 2, 3) * LOG2E
    acc_rows = DA_VD + ONES_ROWS
    return pl.pallas_call(
        _da_kernel,
        grid=(b, DA_HEADS, s // TDA),
        in_specs=[pl.BlockSpec((None, TDA, 128), lambda bi, h, qi: (bi, qi, h)),
                  pl.BlockSpec((None, s, 128), lambda bi, h, qi: (bi, 0, h)),
                  pl.BlockSpec((None, s // TKI, DA_VD, TKI), lambda bi, h, qi: (bi, 0, h, 0)),
                  pl.BlockSpec((None, 3, TKI, TDA), lambda bi, h, qi: (h, 0, 0, 0)),
                  pl.BlockSpec((4, DA_HD), lambda bi, h, qi: (0, 0))],
        out_specs=pl.BlockSpec((None, TDA, 128), lambda bi, h, qi: (bi, qi, h)),
        out_shape=jax.ShapeDtypeStruct((b, s, DA_HEADS * DA_VD), F32),
        scratch_shapes=_chain_scratch(2, acc_rows, TDA),
        compiler_params=_cparams(("arbitrary", "arbitrary", "arbitrary")),
        name="da_prompt",
    )(qa, dak, davt, tb, da_lambda)


def _cmp_weights(cmp_w1, cmp_w2):
    w1 = jnp.zeros((CMP_STRIDE, 2, NSA_HD, 2, 2, CMP_HID), F32)
    for half in range(2):
        for c in range(2):
            blk = cmp_w1[c, half * CMP_STRIDE:(half + 1) * CMP_STRIDE]
            w1 = w1.at[:, c, :, half, c, :].set(blk)
    w1 = w1.reshape(CMP_STRIDE * 2 * NSA_HD, 2 * 2 * CMP_HID).astype(BF16)
    w2 = jnp.zeros((2, CMP_HID, 2, 2, NSA_HD), F32)
    for c in range(2):
        for rep in range(2):
            w2 = w2.at[c, :, c, rep, :].set(cmp_w2[c])
    w2 = w2.reshape(2 * CMP_HID, 2 * 2 * NSA_HD).astype(BF16)
    return w1, w2


def _cmp1_rows_kernel(x_ref, w_ref, o_ref, *, tm):
    rows_pos = 2 * NSA_KV
    for g in range(NSA_KV):
        xg = jnp.concatenate([x_ref[pl.ds(l * rows_pos + g, tm, stride=CMP_STRIDE * rows_pos), :]
                              for l in range(CMP_STRIDE)], axis=1)
        o_ref[:, g * 256:(g + 1) * 256] = _mm(xg, w_ref[...])


def _cmp_stage1_rows(rows2d, w1):
    rows_chunk = CMP_STRIDE * 2 * NSA_KV
    r = rows2d.shape[0] // rows_chunk
    tm = math.gcd(r, 128)
    return pl.pallas_call(
        functools.partial(_cmp1_rows_kernel, tm=tm),
        grid=(r // tm,),
        in_specs=[pl.BlockSpec((tm * rows_chunk, LANE), lambda i: (i, 0)),
                  pl.BlockSpec(w1.shape, lambda i: (0, 0))],
        out_specs=pl.BlockSpec((tm, 512), lambda i: (i, 0)),
        out_shape=jax.ShapeDtypeStruct((r, 512), F32),
        compiler_params=_cparams(("arbitrary",)),
        name="cmp_stage1_rows",
    )(rows2d, w1)


def _cmp_stage2(ab, pe_ref, w1c_ref, w2_ref):
    n = ab.shape[0]
    cst = jnp.concatenate([_mm(pe_ref[c], w1c_ref[c])[0:1, :] for c in range(2)], axis=1)
    outs = []
    for g in range(NSA_KV):
        a = ab[:, g * 256:g * 256 + 128]
        bn = pltpu.roll(ab[:, g * 256 + 128:g * 256 + 256], n - 1, 0)
        hid = jax.nn.gelu(a + bn + cst)
        outs.append(_mm(hid, w2_ref[...]))
    return jnp.concatenate(outs, axis=1)


def _cmp2_kernel(ab_ref, pe_ref, w1c_ref, w2_ref, o_ref):
    o_ref[...] = _cmp_stage2(ab_ref[...], pe_ref, w1c_ref, w2_ref).astype(BF16)


def _cmp_stage2_prompt(ab, pe8, w1c, w2):
    b, n, _ = ab.shape
    return pl.pallas_call(
        _cmp2_kernel,
        grid=(b,),
        in_specs=[pl.BlockSpec((None, n, 512), lambda i: (i, 0, 0)),
                  pl.BlockSpec(pe8.shape, lambda i: (0, 0, 0)),
                  pl.BlockSpec(w1c.shape, lambda i: (0, 0, 0)),
                  pl.BlockSpec(w2.shape, lambda i: (0, 0))],
        out_specs=pl.BlockSpec((None, n, 512), lambda i: (i, 0, 0)),
        out_shape=jax.ShapeDtypeStruct((b, n, 512), BF16),
        compiler_params=_cparams(("arbitrary",)),
        name="cmp_stage2",
    )(ab, pe8, w1c, w2)


def _ovl_t(n_chunks):
    n = np.arange(n_chunks)[:, None]
    j = np.arange(SEL_LANES)[None, :]
    ovl = (n * CMP_STRIDE < j * SEL_BLOCK + SEL_BLOCK) & (j * SEL_BLOCK < n * CMP_STRIDE + CMP_LEN)
    ovl &= n < n_chunks - 1
    return jnp.asarray(ovl.astype(np.float32)).astype(BF16)


def _topk_select(score, n_pick):
    lane = lax.broadcasted_iota(jnp.int32, score.shape, 1).astype(F32)
    sel = jnp.zeros(score.shape, F32)
    picks = jnp.zeros(score.shape, F32)
    sc = score
    for it in range(n_pick):
        m = jnp.max(sc, axis=-1, keepdims=True)
        first = jnp.min(jnp.where(sc == m, lane, float(SEL_LANES)), axis=-1, keepdims=True)
        hit = lane == first
        sel = jnp.where(hit, 1.0, sel)
        picks = jnp.where(lane == float(it), first, picks)
        sc = jnp.where(hit, REMOVED, sc)
    return sel, picks


def _topk_mask_cols(score_t, n_pick):
    blk = lax.broadcasted_iota(jnp.int32, score_t.shape, 0).astype(F32)
    sel = jnp.zeros(score_t.shape, F32)
    sc = score_t
    for _ in range(n_pick):
        m = jnp.max(sc, axis=0, keepdims=True)
        first = jnp.min(jnp.where(sc == m, blk, float(SEL_LANES)), axis=0, keepdims=True)
        hit = blk == first
        sel = jnp.where(hit, 1.0, sel)
        sc = jnp.where(hit, REMOVED, sc)
    return sel


def _stack_heads(q):
    lane = lax.broadcasted_iota(jnp.int32, (q.shape[0], LANE), 1)
    zero = jnp.zeros((q.shape[0], LANE), q.dtype)
    parts = []
    for hp in range(NSA_HPG):
        blk = q[:, (hp // 2) * LANE:(hp // 2 + 1) * LANE]
        keep = (lane < NSA_HD) if hp % 2 == 0 else (lane >= NSA_HD)
        parts.append(jnp.where(keep, blk, zero))
    return jnp.concatenate(parts, axis=0)


def _unstack_heads(o, tq):
    lane = lax.broadcasted_iota(jnp.int32, (tq, LANE), 1)
    pairs = [jnp.where(lane < NSA_HD, o[(2 * m) * tq:(2 * m + 1) * tq], o[(2 * m + 1) * tq:(2 * m + 2) * tq])
             for m in range(2)]
    return jnp.concatenate(pairs, axis=1)


def _nsa_cmp_kernel(q_ref, kcvc_ref, ovl_ref, oc_ref, sel_ref, *, n_cmp, n_slc):
    qi = pl.program_id(2)
    n_chunks = kcvc_ref.shape[0]
    qs = _stack_heads(q_ref[...])
    kc2 = kcvc_ref[:, 0:128]
    vc2 = kcvc_ref[:, 128:256]
    s = _mm_nt(qs, kc2) * (NSA_HD ** -0.5)
    rows = NSA_HPG * TQ
    t = qi * TQ + (lax.broadcasted_iota(jnp.int32, (rows, n_chunks), 0) & (TQ - 1))
    n = lax.broadcasted_iota(jnp.int32, (rows, n_chunks), 1)
    p = _masked_softmax(s, (n * CMP_STRIDE + (CMP_LEN - 1) <= t) & (n < n_cmp))
    oc_ref[...] = _unstack_heads(_mm(p, vc2), TQ)
    psum = p[0:TQ] + p[TQ:2 * TQ] + p[2 * TQ:3 * TQ] + p[3 * TQ:4 * TQ]
    imp = _mm3(psum, ovl_ref[...])
    tq = qi * TQ + lax.broadcasted_iota(jnp.int32, (TQ, SEL_LANES), 0)
    blk = lax.broadcasted_iota(jnp.int32, (TQ, SEL_LANES), 1)
    cur = lax.shift_right_logical(tq, 6)
    forced = (blk == 0) | (blk == cur) | (blk == cur - 1)
    score = jnp.where(forced, SEL_BIG, jnp.where(blk * SEL_BLOCK <= tq, imp, -SEL_BIG))
    score = jnp.where(blk < n_slc, score, REMOVED)
    sel_t = _topk_mask_cols(score.T, min(N_SEL, n_slc))
    sel_ref[...] = jnp.where(sel_t > 0.5, 0.0, NEG).astype(BF16)


def _nsa_cmp_prompt(qb, kcvc, ovl):
    b, s, _ = qb.shape
    n_chunks = kcvc.shape[1]
    kern = functools.partial(_nsa_cmp_kernel, n_cmp=n_chunks - 1, n_slc=s // SEL_BLOCK)
    return pl.pallas_call(
        kern,
        grid=(b, NSA_KV, s // TQ),
        in_specs=[pl.BlockSpec((None, TQ, 256), lambda bi, g, qi: (bi, qi, g)),
                  pl.BlockSpec((None, n_chunks, 256), lambda bi, g, qi: (bi, 0, g)),
                  pl.BlockSpec(ovl.shape, lambda bi, g, qi: (0, 0))],
        out_specs=[pl.BlockSpec((None, TQ, 256), lambda bi, g, qi: (bi, qi, g)),
                   pl.BlockSpec((None, None, TQ, SEL_LANES), lambda bi, g, qi: (bi, g, qi, 0))],
        out_shape=[jax.ShapeDtypeStruct((b, s, 512), F32),
                   jax.ShapeDtypeStruct((b, NSA_KV, s, SEL_LANES), BF16)],
        compiler_params=_cparams(("arbitrary", "arbitrary", "arbitrary")),
        name="nsa_cmp",
    )(qb, kcvc, ovl)


def _nsa_sw_kernel(q_ref, ks_ref, kw_ref, vs_ref, vw_ref, sel_ref, e_ref, tb_ref, os_ref, ow_ref, *scratch):
    m = pl.program_id(2)
    qs = _stack_heads(q_ref[...])
    qs_t = qs.astype(F32).T.astype(BF16)
    sel_t = jnp.concatenate([sel_ref[t * TQ:(t + 1) * TQ, :] for t in range(TQS // TQ)] * NSA_HPG, axis=1)
    qs_sel_t = jnp.concatenate([qs_t, sel_t], axis=0)
    ones = jnp.ones((ONES_ROWS, TKI), BF16)

    def sel_scores(j, dst):
        keys = jnp.concatenate([ks_ref[_key_rows(ks_ref, j), :], e_ref[j]], axis=1)
        dst[0] = jnp.dot(keys, qs_sel_t, preferred_element_type=F32)

    def win_scores(j, dst):
        dst[0] = jnp.dot(kw_ref[_key_rows(kw_ref, j), :], qs_t, preferred_element_type=F32)

    bias = lambda table, l0: tb_ref[table, :, l0:l0 + LANE]
    sel = _Chain(scratch[0:2], scratch[2:4], scratch[4:6], scratch[6], scratch[7], sel_scores,
                 lambda j: jnp.concatenate([vs_ref[j], ones], axis=0), bias)
    win = _Chain(scratch[8:10], scratch[10:12], scratch[12:14], scratch[14], scratch[15], win_scores,
                 lambda j: jnp.concatenate([vw_ref[j], ones], axis=0), bias)

    def heads_out(acc_ref):
        o_t = acc_ref[0, 0:NSA_HD, :] / acc_ref[0, NSA_HD:NSA_HD + 1, :]
        pairs = [jnp.concatenate([o_t[:, (2 * k) * TQS:(2 * k + 1) * TQS],
                                  o_t[:, (2 * k + 1) * TQS:(2 * k + 2) * TQS]], axis=0).T
                 for k in range(NSA_HPG // 2)]
        return jnp.concatenate(pairs, axis=1)

    n_far = jnp.maximum(m - 1, 0)
    j1 = jnp.maximum(m - 1, 0)
    j2 = jnp.maximum(m - 2, 0)
    pen1 = jnp.where(m >= 1, 0.0, NEG)
    pen2 = jnp.where(m >= 2, 0.0, NEG)
    sel.prime(jnp.where(n_far > 0, 0, m))
    win.prime(m)

    def pair(i, carry):
        a = 2 * i
        sel.step(jnp.maximum(a - 1, 0), jnp.minimum(a + 1, n_far - 1))
        sel.step(a, jnp.where(a + 2 < n_far, a + 2, m), valid=a + 1 < n_far)
        return carry

    lax.fori_loop(0, lax.shift_right_logical(n_far + 1, 1), pair, 0)
    sel.step(jnp.maximum(n_far - 1, 0), j1, table=0, prev_valid=(n_far & 1) == 0)
    win.step(m, j1, table=0)
    sel.step(m, None, table=1, pen=pen1)
    win.step(m, j2, table=1, pen=pen1)
    sel.flush(j1)
    win.step(j1, None, table=2, pen=pen2)
    win.flush(j2)
    os_ref[...] = heads_out(sel.acc)
    ow_ref[...] = heads_out(win.acc)


def _nsa_sw_prompt(qb, ksel, kwin, selvt, winvt, sel, gd_b):
    b, s, _ = qb.shape
    assert TQS == TKI and WINDOW == 2 * TKI
    cols = NSA_HPG * TQS
    i = np.arange(TQS)[:, None]
    j = np.arange(TKI)[None, :]
    edge = jnp.asarray(np.broadcast_to(np.where(j >= i, 0.0, NEG).astype(np.float32), (NSA_HEADS, TQS, TKI)))
    tb = jnp.stack([_toeplitz(gd_b, 0, TQS, TKI, NEG), _toeplitz(gd_b, TKI, TQS, TKI), edge])
    tb = tb.reshape(3, NSA_KV, NSA_HPG, TQS, TKI).transpose(1, 0, 4, 2, 3).reshape(NSA_KV, 3, TKI, cols) * LOG2E
    nk = s // TKI
    e = np.zeros((nk, TKI, SEL_LANES), np.float32)
    for kj in range(nk):
        for k in range(TKI):
            e[kj, k, kj * (TKI // SEL_BLOCK) + k // SEL_BLOCK] = 1.0
    e = jnp.asarray(e).astype(BF16)
    acc_rows = NSA_HD + ONES_ROWS
    return pl.pallas_call(
        _nsa_sw_kernel,
        grid=(b, NSA_KV, s // TQS),
        in_specs=[pl.BlockSpec((None, TQS, 256), lambda bi, g, qi: (bi, qi, g)),
                  pl.BlockSpec((None, s, LANE), lambda bi, g, qi: (bi, 0, g)),
                  pl.BlockSpec((None, s, LANE), lambda bi, g, qi: (bi, 0, g)),
                  pl.BlockSpec((None, nk, NSA_HD, TKI), lambda bi, g, qi: (bi, 0, g, 0)),
                  pl.BlockSpec((None, nk, NSA_HD, TKI), lambda bi, g, qi: (bi, 0, g, 0)),
                  pl.BlockSpec((None, None, TQS, SEL_LANES), lambda bi, g, qi: (bi, g, qi, 0)),
                  pl.BlockSpec(e.shape, lambda bi, g, qi: (0, 0, 0)),
                  pl.BlockSpec((None, 3, TKI, cols), lambda bi, g, qi: (g, 0, 0, 0))],
        out_specs=[pl.BlockSpec((None, TQS, 256), lambda bi, g, qi: (bi, qi, g)),
                   pl.BlockSpec((None, TQS, 256), lambda bi, g, qi: (bi, qi, g))],
        out_shape=[jax.ShapeDtypeStruct((b, s, 512), F32), jax.ShapeDtypeStruct((b, s, 512), F32)],
        scratch_shapes=_chain_scratch(1, acc_rows, cols) + _chain_scratch(1, acc_rows, cols),
        compiler_params=_cparams(("arbitrary", "arbitrary", "arbitrary")),
        name="nsa_sel_win",
    )(qb, ksel, kwin, selvt, winvt, sel, e, tb)


def _layer_norm(x, g, b):
    mu = jnp.mean(x, axis=-1, keepdims=True)
    xc = x - mu
    var = jnp.mean(xc * xc, axis=-1, keepdims=True)
    return xc * lax.rsqrt(var + LN_EPS) * g + b


def _gate_expand():
    e = np.zeros((3, LANE, NSA_HEADS * NSA_HD), np.float32)
    for h in range(NSA_HEADS):
        for j in range(3):
            e[j, h * 3 + j, h * NSA_HD:(h + 1) * NSA_HD] = 1.0
    return jnp.asarray(e).astype(BF16)


def _tail1_kernel(oa_ref, oc_ref, os_ref, ow_ref, gate_ref, ma_ref, mb_ref, x_ref, ada_ref,
                  wa_ref, wb_ref, wo_ref, sub_ref, eg_ref, g1_ref, b1_ref, o_ref):
    oa = oa_ref[...]
    parts = []
    for h in range(DA_HEADS):
        of = oa[:, h * DA_VD:(h + 1) * DA_VD]
        rr = lax.rsqrt(jnp.mean(of * of, axis=-1, keepdims=True) + RMS_EPS)
        parts.append(of * rr * sub_ref[...] * (1.0 - LAM_INIT))
    oan = jnp.concatenate(parts, axis=1)
    sg = jax.nn.sigmoid(gate_ref[...])
    ob = (_mm3(sg, eg_ref[0]) * oc_ref[...] + _mm3(sg, eg_ref[1]) * os_ref[...]
          + _mm3(sg, eg_ref[2]) * ow_ref[...])
    y = (jax.nn.sigmoid(ma_ref[...]) * _mm(oan, wa_ref[...])
         + jax.nn.sigmoid(mb_ref[...]) * _mm(ob, wb_ref[...]))
    z = ALPHA * x_ref[...] + ada_ref[2] * _mm(y, wo_ref[...])
    o_ref[...] = _layer_norm(z, g1_ref[...], b1_ref[...])


def _tail1(oa, oc, os_, ow, gate, ma, mb, x2d, ada, wa, wb, wo, sub, eg, g1, b1, tm, tiles_per_group):
    m = x2d.shape[0]
    row = lambda w: pl.BlockSpec((tm, w), lambda i: (i, 0))
    full = lambda a: pl.BlockSpec(a.shape, lambda i: (0,) * a.ndim)
    return pl.pallas_call(
        _tail1_kernel,
        grid=(m // tm,),
        in_specs=[row(512), row(512), row(512), row(512), row(128), row(1024), row(1024), row(1024),
                  _ada_spec(ada, tm, tiles_per_group),
                  full(wa), full(wb), full(wo), full(sub), full(eg), full(g1), full(b1)],
        out_specs=row(1024),
        out_shape=jax.ShapeDtypeStruct((m, D_MODEL), F32),
        compiler_params=_cparams(("arbitrary",)),
        name="tail_merge",
    )(oa, oc, os_, ow, gate, ma, mb, x2d, ada, wa, wb, wo, sub, eg, g1, b1)


def _tail2_kernel(x_ref, ada_ref, wu_ref, wd_ref, g2_ref, b2_ref, o_ref, h_scr, acc):
    f = pl.program_id(1)

    @pl.when(f == 0)
    def _():
        h_scr[...] = (x_ref[...] * (1.0 + ada_ref[4]) + ada_ref[3]).astype(BF16)
        acc[...] = jnp.zeros(acc.shape, F32)

    u = jnp.maximum(jnp.dot(h_scr[...], wu_ref[...], preferred_element_type=F32), 0.0)
    acc[...] += _mm(u * u, wd_ref[...])

    @pl.when(f == pl.num_programs(1) - 1)
    def _():
        z = ALPHA * x_ref[...] + ada_ref[5] * acc[...]
        o_ref[...] = _layer_norm(z, g2_ref[...], b2_ref[...])


def _tail2(x1, ada, wu, wd, g2, b2, tm, tiles_per_group):
    m = x1.shape[0]
    tf = 1024
    return pl.pallas_call(
        _tail2_kernel,
        grid=(m // tm, D_FF // tf),
        in_specs=[pl.BlockSpec((tm, D_MODEL), lambda i, f: (i, 0)),
                  _ada_spec(ada, tm, tiles_per_group),
                  pl.BlockSpec((D_MODEL, tf), lambda i, f: (0, f)),
                  pl.BlockSpec((tf, D_MODEL), lambda i, f: (f, 0)),
                  pl.BlockSpec((1, D_MODEL), lambda i, f: (0, 0)),
                  pl.BlockSpec((1, D_MODEL), lambda i, f: (0, 0))],
        out_specs=pl.BlockSpec((tm, D_MODEL), lambda i, f: (i, 0)),
        out_shape=jax.ShapeDtypeStruct((m, D_MODEL), F32),
        scratch_shapes=[pltpu.VMEM((tm, D_MODEL), BF16), pltpu.VMEM((tm, D_MODEL), F32)],
        compiler_params=_cparams(("arbitrary", "arbitrary")),
        name="tail_mlp",
    )(x1, ada, wu, wd, g2, b2)


def _da_decode_kernel(pt_ref, *refs):
    pages = refs[:PP]
    q_ref, kn_ref, vn_ref, bl_ref, b0_ref, lam_ref, o_ref, m_ref, l_ref, a_ref = refs[PP:]
    j = pl.program_id(1)
    last = j == pl.num_programs(1) - 1
    scale = DA_HD ** -0.5
    rows_pg = PAGE * 2 * DA_HEADS

    @pl.when(j == 0)
    def _():
        m_ref[...] = jnp.full(m_ref.shape, NEG, F32)
        l_ref[...] = jnp.zeros(l_ref.shape, F32)
        a_ref[...] = jnp.zeros(a_ref.shape, F32)

    q = q_ref[...]
    row = lax.broadcasted_iota(jnp.int32, (8, rows_pg), 0)
    col = lax.broadcasted_iota(jnp.int32, (8, rows_pg), 1)
    cmask = jnp.where((col & 7) == lax.shift_right_logical(row, 1), 0.0, NEG)
    scores = []
    xs = []
    for k in range(PP):
        x = pages[k][...].astype(BF16)
        sc = _mm_nt(q, x) * scale + cmask
        if k == PP - 1:
            sc = sc + jnp.where(last, bl_ref[...], 0.0)
        scores.append(sc)
        xs.append(x)
    s = jnp.concatenate(scores, axis=1)
    m_old = m_ref[...]
    m_new = jnp.maximum(m_old, jnp.max(s, axis=-1, keepdims=True))
    p = jnp.exp(s - m_new)
    alpha = jnp.exp(m_old - m_new)
    l_ref[...] = alpha * l_ref[...] + jnp.sum(p, axis=-1, keepdims=True)
    acc = alpha * a_ref[...]
    for k in range(PP):
        pv = pltpu.roll(p[:, k * rows_pg:(k + 1) * rows_pg], DA_HEADS, 1)
        acc = acc + _mm(pv, xs[k])
    a_ref[...] = acc
    m_ref[...] = m_new

    @pl.when(last)
    def _():
        s_new = jnp.sum(q.astype(F32) * kn_ref[...], axis=-1, keepdims=True) * scale + b0_ref[:, 0:1]
        m_o = m_ref[...]
        m_n = jnp.maximum(m_o, s_new)
        p_new = jnp.exp(s_new - m_n)
        al = jnp.exp(m_o - m_n)
        raw = (al * a_ref[...] + p_new * vn_ref[...]) / (al * l_ref[...] + p_new)
        o_ref[...] = raw - _diff_lambda(lam_ref) * pltpu.roll(raw, 7, 0)


def _da_decode(page_table, cache_rows, q8, k_new, v_new, bl, b0, da_lambda):
    b, n_pages = page_table.shape
    rows_pg = PAGE * 2 * DA_HEADS
    page_spec = lambda k: pl.BlockSpec((rows_pg, LANE), lambda bi, j, pt: (pt[bi, j * PP + k], 0))
    per_row = lambda: pl.BlockSpec((None, 8, LANE), lambda bi, j, pt: (bi, 0, 0))
    grid_spec = pltpu.PrefetchScalarGridSpec(
        num_scalar_prefetch=1,
        grid=(b, n_pages // PP),
        in_specs=[page_spec(k) for k in range(PP)] + [
            per_row(), per_row(), per_row(),
            pl.BlockSpec((8, rows_pg), lambda bi, j, pt: (0, 0)),
            pl.BlockSpec((8, LANE), lambda bi, j, pt: (0, 0)),
            pl.BlockSpec((4, DA_HD), lambda bi, j, pt: (0, 0))],
        out_specs=per_row(),
        scratch_shapes=[pltpu.VMEM((8, 1), F32), pltpu.VMEM((8, 1), F32), pltpu.VMEM((8, LANE), F32)])
    return pl.pallas_call(
        _da_decode_kernel,
        grid_spec=grid_spec,
        out_shape=jax.ShapeDtypeStruct((b, 8, LANE), F32),
        compiler_params=_cparams(("arbitrary", "arbitrary")),
        name="da_decode",
    )(page_table, *([cache_rows] * PP), q8, k_new, v_new, bl, b0, da_lambda)


def _nsa_decode1_kernel(pt_ref, ab_hbm, q_ref, swa_ref, new_ref, bw_ref, pe_ref, w1c_ref, w2_ref, ovl_ref,
                        oc_ref, ow_ref, idx_ref, abuf, sem, *, n_pages):
    b = pl.program_id(0)
    copies = [pltpu.make_async_copy(ab_hbm.at[pt_ref[b, p]], abuf.at[p], sem) for p in range(n_pages)]
    for c in copies:
        c.start()
    for c in copies:
        c.wait()
    n_chunks = n_pages * (PAGE // CMP_STRIDE)
    kcvc = _cmp_stage2(abuf[...].reshape(n_chunks, 512), pe_ref, w1c_ref, w2_ref)
    scale = NSA_HD ** -0.5
    swa = swa_ref[...].astype(BF16)
    n_win = swa.shape[0]
    for g in range(NSA_KV):
        q = q_ref[g]
        s = _mm_nt(q, kcvc[:, g * 256:g * 256 + 128]) * scale
        n = lax.broadcasted_iota(jnp.int32, s.shape, 1)
        p = _masked_softmax(s, n < n_chunks - 1)
        oc_ref[g] = _mm(p, kcvc[:, g * 256 + 128:g * 256 + 256])
        psum = jnp.sum(p[0:NSA_HPG], axis=0, keepdims=True)
        imp = _mm3(jnp.broadcast_to(psum, (8, n_chunks)), ovl_ref[...])
        blk = lax.broadcasted_iota(jnp.int32, imp.shape, 1)
        n_blk = n_chunks * CMP_STRIDE // SEL_BLOCK
        forced = (blk == 0) | (blk == n_blk - 1)
        score = jnp.where(blk < n_blk, jnp.where(forced, SEL_BIG, imp), REMOVED)
        _, picks = _topk_select(score, N_SEL - 1)
        idx_ref[g] = picks.astype(jnp.int32)
        new = new_ref[g:g + 1, :]
        sw = _mm_nt(q, swa) * scale + bw_ref[g][:, 0:n_win]
        s_new = (jnp.sum(q.astype(F32) * new, axis=-1, keepdims=True) * scale
                 + bw_ref[g][:, n_win:n_win + 1])
        m = jnp.maximum(jnp.max(sw, axis=-1, keepdims=True), s_new)
        e = jnp.exp(sw - m)
        e_new = jnp.exp(s_new - m)
        den = jnp.sum(e, axis=-1, keepdims=True) + e_new
        ow_ref[g] = (_mm(e, swa) + e_new * new) / den


def _nsa_decode1(page_table, ab_pool, qc, cache_swa, kvw_new, bw, pe8, w1c, w2, ovl):
    b, n_pages = page_table.shape
    full = lambda a: pl.BlockSpec(a.shape, lambda bi, pt: (0,) * a.ndim)
    out4 = lambda: pl.BlockSpec((None, NSA_KV, 8, LANE), lambda bi, pt: (bi, 0, 0, 0))
    grid_spec = pltpu.PrefetchScalarGridSpec(
        num_scalar_prefetch=1,
        grid=(b,),
        in_specs=[pl.BlockSpec(memory_space=pl.ANY),
                  pl.BlockSpec((None, NSA_KV, 8, LANE), lambda bi, pt: (bi, 0, 0, 0)),
                  pl.BlockSpec((None, cache_swa.shape[1], LANE), lambda bi, pt: (bi, 0, 0)),
                  pl.BlockSpec((None, NSA_KV, LANE), lambda bi, pt: (bi, 0, 0)),
                  full(bw), full(pe8), full(w1c), full(w2), full(ovl)],
        out_specs=[out4(), out4(), out4()],
        scratch_shapes=[pltpu.VMEM((n_pages, PAGE // CMP_STRIDE, 512), F32), pltpu.SemaphoreType.DMA(())])
    return pl.pallas_call(
        functools.partial(_nsa_decode1_kernel, n_pages=n_pages),
        grid_spec=grid_spec,
        out_shape=[jax.ShapeDtypeStruct((b, NSA_KV, 8, LANE), F32),
                   jax.ShapeDtypeStruct((b, NSA_KV, 8, LANE), F32),
                   jax.ShapeDtypeStruct((b, NSA_KV, 8, LANE), jnp.int32)],
        compiler_params=_cparams(("arbitrary",)),
        name="nsa_decode_cmp_win",
    )(page_table, ab_pool, qc, cache_swa, kvw_new, bw, pe8, w1c, w2, ovl)


def _nsa_decode2_kernel(pt_ref, idx_ref, *refs, n_blk, n_pick):
    blks = refs[:n_pick]
    q_ref, new_ref, bs_ref, o_ref = refs[n_pick:]
    b = pl.program_id(0)
    g = pl.program_id(1)
    scale = NSA_HD ** -0.5
    q = q_ref[...]
    cols = blks[0].shape[0]
    scores = []
    xs = []
    for k in range(n_pick):
        x = blks[k][...].astype(BF16)
        blk = idx_ref[b, g, k]
        bias = (jnp.where(blk == n_blk - 1, bs_ref[:, cols:2 * cols], 0.0)
                + jnp.where(blk == n_blk - 2, bs_ref[:, 2 * cols:3 * cols], 0.0))
        scores.append(_mm_nt(q, x) * scale + bs_ref[:, 0:cols] + bias)
        xs.append(x)
    new = new_ref[...]
    s_new = jnp.sum(q.astype(F32) * new, axis=-1, keepdims=True) * scale + bs_ref[:, 3 * cols:3 * cols + 1]
    s = jnp.concatenate(scores, axis=1)
    m = jnp.maximum(jnp.max(s, axis=-1, keepdims=True), s_new)
    p = jnp.exp(s - m)
    p_new = jnp.exp(s_new - m)
    acc = p_new * new
    for k in range(n_pick):
        acc = acc + _mm(p[:, k * cols:(k + 1) * cols], xs[k])
    o_ref[...] = acc / (jnp.sum(p, axis=-1, keepdims=True) + p_new)


def _nsa_decode2(page_table, idx, cache_rows, qc, nsa_new, bs):
    b = page_table.shape[0]
    n_pick = idx.shape[2]
    rows_blk = SEL_BLOCK * 2 * NSA_KV

    def blk_spec(k):
        def blk_map(bi, g, pt, ix):
            blk = ix[bi, g, k]
            return (pt[bi, lax.shift_right_logical(blk, 1)] * 2 + (blk & 1), 0)
        return pl.BlockSpec((rows_blk, LANE), blk_map)

    grid_spec = pltpu.PrefetchScalarGridSpec(
        num_scalar_prefetch=2,
        grid=(b, NSA_KV),
        in_specs=[blk_spec(k) for k in range(n_pick)] + [
            pl.BlockSpec((None, None, 8, LANE), lambda bi, g, pt, ix: (bi, g, 0, 0)),
            pl.BlockSpec((None, None, 1, LANE), lambda bi, g, pt, ix: (bi, g, 0, 0)),
            pl.BlockSpec((None, 8, bs.shape[2]), lambda bi, g, pt, ix: (g, 0, 0))],
        out_specs=pl.BlockSpec((None, None, 8, LANE), lambda bi, g, pt, ix: (bi, g, 0, 0)))
    return pl.pallas_call(
        functools.partial(_nsa_decode2_kernel, n_blk=page_table.shape[1] * PAGE // SEL_BLOCK, n_pick=n_pick),
        grid_spec=grid_spec,
        out_shape=jax.ShapeDtypeStruct((b, NSA_KV, 8, LANE), F32),
        compiler_params=_cparams(("arbitrary", "arbitrary")),
        name="nsa_decode_sel",
    )(page_table, idx, *([cache_rows] * n_pick), qc, nsa_new, bs)


def _prompt_mixers(pr, b, s, gd_a, gd_b, da_lambda, w1, w2, pe8, w1c):
    sh = lambda a: a.reshape(b, s, a.shape[-1])
    nk = s // TKI
    vt = lambda a: a.reshape(b, nk, a.shape[1], TKI)
    o_a = _da_prompt(sh(pr["qa2"]), sh(pr["dak"]), vt(pr["davt"]), gd_a, da_lambda)
    n_chunks = s // CMP_STRIDE
    ab = _cmp_stage1_rows(pr["nsa"], w1)
    kcvc = _cmp_stage2_prompt(ab.reshape(b, n_chunks, 512), pe8, w1c, w2)
    o_c, sel = _nsa_cmp_prompt(sh(pr["qb"]), kcvc, _ovl_t(n_chunks))
    o_s, o_w = _nsa_sw_prompt(sh(pr["qb2"]), sh(pr["ksel"]), sh(pr["kwin"]), vt(pr["selvt"]), vt(pr["winvt"]),
                              sel, gd_b)
    flat = lambda a: a.reshape(b * s, a.shape[-1])
    return flat(o_a), flat(o_c), flat(o_s), flat(o_w)


def _sample_mixers(pr, page_table, cache_da, cache_nsa, cache_swa, gd_a, gd_b, da_lambda, w1, w2, pe8, w1c):
    b, n_pages = page_table.shape
    past = n_pages * PAGE
    n_pool = cache_da.shape[0]
    da_rows = cache_da.reshape(n_pool, PAGE, DA_HEADS, 2, LANE).transpose(0, 1, 3, 2, 4).reshape(-1, LANE)
    nsa_rows = cache_nsa.reshape(n_pool, PAGE, NSA_KV, 2, LANE).transpose(0, 1, 3, 2, 4).reshape(-1, LANE)
    qa = pr["qa"].reshape(b, DA_HEADS, 2, DA_HD)
    q8 = jnp.zeros((b, DA_HEADS, 2, 2, DA_HD), BF16)
    for c in range(2):
        q8 = q8.at[:, :, c, c, :].set(qa[:, :, c])
    q8 = q8.reshape(b, 8, LANE)
    da_new = pr["da"].reshape(b, DA_HEADS, 2, LANE)
    k_new = jnp.repeat(da_new[:, :, 0], 2, axis=1)
    v_new = jnp.repeat(da_new[:, :, 1], 2, axis=1)
    gda8 = jnp.repeat(gd_a, 2, axis=0)
    bl = jnp.repeat(gda8[:, PAGE - jnp.arange(PAGE)], 2 * DA_HEADS, axis=1)
    b0 = jnp.broadcast_to(gda8[:, 0:1], (8, LANE))
    o_a = _da_decode(page_table, da_rows, q8, k_new, v_new, bl, b0, da_lambda)
    o_a = o_a[:, 0::2, :].reshape(b, 512)
    chunks = PAGE // CMP_STRIDE
    ab_pool = _cmp_stage1_rows(nsa_rows, w1).reshape(n_pool, chunks, 512)
    qb = pr["qb"].reshape(b, NSA_KV, NSA_HPG, NSA_HD)
    qc = jnp.zeros((b, NSA_KV, 8, LANE), BF16).at[:, :, :NSA_HPG, :NSA_HD].set(qb)
    gdb = jnp.pad(gd_b.reshape(NSA_KV, NSA_HPG, -1), ((0, 0), (0, 8 - NSA_HPG), (0, 0)))
    own = jnp.arange(NSA_KV)[:, None, None]
    n_win = cache_swa.shape[1]
    grp_w = jnp.arange(n_win * NSA_KV)[None, None, :] % NSA_KV
    bw = jnp.where(grp_w == own, jnp.repeat(gdb[:, :, n_win - jnp.arange(n_win)], NSA_KV, axis=2), NEG)
    bw = jnp.concatenate([bw, jnp.broadcast_to(gdb[:, :, 0:1], (NSA_KV, 8, LANE))], axis=2)
    o_c, o_w, idx = _nsa_decode1(page_table, ab_pool, qc, cache_swa.reshape(b, n_win * NSA_KV, LANE),
                                 pr["kvw"].reshape(b, NSA_KV, LANE), bw, pe8, w1c, w2,
                                 _ovl_t(past // CMP_STRIDE))
    idx = idx[:, :, 0, :N_SEL - 1]
    rows_blk = SEL_BLOCK * 2 * NSA_KV
    kind = jnp.arange(rows_blk)[None, None, :] % (2 * NSA_KV)
    keep = kind == NSA_KV + own
    bs = jnp.concatenate([
        jnp.where(keep, 0.0, NEG) * jnp.ones((1, 8, 1), F32),
        jnp.repeat(gdb[:, :, SEL_BLOCK - jnp.arange(SEL_BLOCK)], 2 * NSA_KV, axis=2),
        jnp.repeat(gdb[:, :, 2 * SEL_BLOCK - jnp.arange(SEL_BLOCK)], 2 * NSA_KV, axis=2),
        jnp.broadcast_to(gdb[:, :, 0:1], (NSA_KV, 8, LANE))], axis=2)
    sel_new = pr["nsa"].reshape(b, NSA_KV, 2, LANE)[:, :, 1:2, :]
    o_s = _nsa_decode2(page_table, idx, nsa_rows, qc, sel_new, bs)
    o_c = o_c[:, :, :NSA_HPG, :NSA_HD].reshape(b, 512)
    o_w = o_w[:, :, :NSA_HPG, NSA_HD:].reshape(b, 512)
    o_s = o_s[:, :, :NSA_HPG, NSA_HD:].reshape(b, 512)
    return o_a, o_c, o_s, o_w


def kernel(x_prompt, x_sample, cache_da_kv, cache_nsa_kv, cache_swa_kv, page_table, c_prompt, c_sample, rel_bias, w_ada, b_ada, w_in, da_lambda, da_subln, cmp_pe, cmp_w1, cmp_w2, w_br_a, w_br_b, w_out, ln1_g, ln1_b, w_up, w_down, ln2_g, ln2_b):
    bp, s, _ = x_prompt.shape
    bs_ = x_sample.shape[0]
    w_perm, w_vt = _perm_w_in(w_in[0])
    w1, w2 = _cmp_weights(cmp_w1[0], cmp_w2[0])
    pe8 = jnp.broadcast_to(cmp_pe[0].reshape(2, 1, CMP_LEN * NSA_HD), (2, 8, CMP_LEN * NSA_HD))
    w1c = cmp_w1[0].reshape(2, CMP_LEN * NSA_HD, CMP_HID)
    gd_a = _dist_bias(rel_bias[:, :DA_HEADS], 1024)
    gd_b = _dist_bias(rel_bias[:, DA_HEADS:], 1024)
    wa, wb, wo = w_br_a[0].astype(BF16), w_br_b[0].astype(BF16), w_out[0].astype(BF16)
    wu, wd = w_up[0].astype(BF16), w_down[0].astype(BF16)
    sub = da_subln[0].reshape(1, DA_VD)
    eg = _gate_expand()
    g1, b1 = ln1_g[0].reshape(1, D_MODEL), ln1_b[0].reshape(1, D_MODEL)
    g2, b2 = ln2_g[0].reshape(1, D_MODEL), ln2_b[0].reshape(1, D_MODEL)
    lam = da_lambda[0]

    n_c = bp + bs_
    c_all = jnp.pad(jnp.concatenate([c_prompt, c_sample], 0), ((0, (-n_c) % 8), (0, 0)))
    ada = _ada(c_all, w_ada[0], b_ada[0])[:n_c].reshape(n_c, 6, D_MODEL)
    ada_p = jnp.transpose(ada[:bp], (1, 0, 2)).reshape(6, bp, 1, D_MODEL)
    ada_s = jnp.transpose(ada[bp:], (1, 0, 2)).reshape(6, 1, bs_, D_MODEL)

    def tail(mix, pr, x2d, ada_x, tm, tpg):
        o_a, o_c, o_s, o_w = mix
        x1 = _tail1(o_a, o_c, o_s, o_w, pr["gate"], pr["ma"], pr["mb"], x2d, ada_x,
                    wa, wb, wo, sub, eg, g1, b1, tm, tpg)
        return _tail2(x1, ada_x, wu, wd, g2, b2, tm, tpg)

    xp = x_prompt.reshape(bp * s, D_MODEL)
    pr_p = _proj(xp, ada_p, w_perm, w_vt, TKI, s // TKI)
    mix_p = _prompt_mixers(pr_p, bp, s, gd_a, gd_b, lam, w1, w2, pe8, w1c)
    tm_t = 512
    y_p = tail(mix_p, pr_p, xp, ada_p, tm_t, s // tm_t).reshape(bp, s, D_MODEL)
    xs = x_sample.reshape(bs_, D_MODEL)
    pr_s = _proj(xs, ada_s, w_perm, None, bs_, 1)
    mix_s = _sample_mixers(pr_s, page_table, cache_da_kv[0], cache_nsa_kv[0], cache_swa_kv[0],
                           gd_a, gd_b, lam, w1, w2, pe8, w1c)
    y_s = tail(mix_s, pr_s, xs, ada_s, bs_, 1).reshape(bs_, 1, D_MODEL)

    win = min(WINDOW, s)
    from_rows = lambda a, units: a.reshape(bp, s, 2, units, LANE).transpose(0, 1, 3, 2, 4).reshape(1, bp, s, units, 256)
    new_da_p = from_rows(pr_p["da"], DA_HEADS)
    new_nsa_p = from_rows(pr_p["nsa"], NSA_KV)
    new_swa_p = pr_p["kvw"].reshape(bp, s, NSA_KV, 2 * NSA_HD)[None, :, s - win:]
    new_da_s = pr_s["da"].reshape(1, bs_, 1, DA_HEADS, 4 * DA_HD)
    new_nsa_s = pr_s["nsa"].reshape(1, bs_, 1, NSA_KV, 4 * NSA_HD)
    new_swa_s = jnp.concatenate([cache_swa_kv[0][:, 1:], pr_s["kvw"].reshape(bs_, 1, NSA_KV, 2 * NSA_HD)],
                                axis=1)[None]
    return (y_p, y_s, new_da_p, new_nsa_p, new_swa_p, new_da_s, new_nsa_s, new_swa_s)
```

```python
import functools
import math

import numpy as np
import jax
import jax.numpy as jnp
from jax import lax
from jax.experimental import pallas as pl
from jax.experimental.pallas import tpu as pltpu

F32 = jnp.float32
BF16 = jnp.bfloat16

D_MODEL = 1024
PAGE = 128
DA_HEADS = 4
DA_HD = 64
DA_VD = 128
NSA_HEADS = 8
NSA_KV = 2
NSA_HPG = 4
NSA_HD = 64
CMP_STRIDE = 16
CMP_LEN = 32
CMP_HID = 64
SEL_BLOCK = 64
N_SEL = 16
WINDOW = 512
D_FF = 4096
N_BUCKETS = 32
MAX_DIST = 128
DEPTH = 1
ALPHA = (2 * DEPTH) ** 0.25
LN_EPS = 1e-5
RMS_EPS = 1e-5
NEG = -1e30
SEL_BIG = 1e9
LAM_INIT = 0.8 - 0.6 * math.exp(-0.3 * 0)
SPLIT_SIZES = (512, 512, 512, 512, 256, 256, 256, 24, 1024, 1024)

LANE = 128
VMEM_LIMIT = 56 * 1024 * 1024
TQ = 128
TKI = 256
TQS = 256
ONES_ROWS = 16
TDA = 512
SEL_LANES = 128
REMOVED = -3e38
PP = 16


def _cparams(sem):
    return pltpu.CompilerParams(dimension_semantics=sem, vmem_limit_bytes=VMEM_LIMIT)


def _mm(a, b):
    return jnp.dot(a.astype(BF16), b.astype(BF16), preferred_element_type=F32)


def _mm_nt(a, b):
    return lax.dot_general(a.astype(BF16), b.astype(BF16), (((1,), (1,)), ((), ())),
                           preferred_element_type=F32)


def _mm3(x, w):
    hi = x.astype(BF16)
    r = x - hi.astype(F32)
    mid = r.astype(BF16)
    lo = (r - mid.astype(F32)).astype(BF16)
    return (jnp.dot(hi, w, preferred_element_type=F32) + jnp.dot(mid, w, preferred_element_type=F32)
            + jnp.dot(lo, w, preferred_element_type=F32))


def _masked_softmax(s, valid):
    l = jnp.where(valid, s, NEG)
    m = jnp.max(l, axis=-1, keepdims=True)
    e = jnp.where(valid, jnp.exp(l - m), 0.0)
    return e / jnp.maximum(jnp.sum(e, axis=-1, keepdims=True), 1e-30)


def _t5_bucket(dist):
    n = jnp.maximum(dist, 0)
    max_exact = N_BUCKETS // 2
    nf = jnp.maximum(n, 1).astype(F32)
    large = max_exact + (jnp.log(nf / max_exact) / math.log(MAX_DIST / max_exact)
                         * (N_BUCKETS - max_exact)).astype(jnp.int32)
    return jnp.where(n < max_exact, n, jnp.minimum(large, N_BUCKETS - 1))


def _dist_bias(tbl, n):
    d = jnp.arange(n, dtype=jnp.int32)
    g = tbl[_t5_bucket(d)] - tbl[N_BUCKETS - 1][None, :]
    return jnp.transpose(g)


def _toeplitz(gd, offset, rows, cols, below=0.0):
    heads, n = gd.shape
    length = rows + cols - 1
    assert offset + rows <= n
    lo = offset - cols + 1
    hvec = gd[:, max(lo, 0):offset + rows]
    if lo < 0:
        hvec = jnp.concatenate([jnp.full((heads, -lo), below, gd.dtype), hvec], axis=1)
    rev = jnp.concatenate([hvec[:, ::-1], jnp.zeros((heads, 1), gd.dtype)], axis=1)
    flat = jnp.tile(rev, (1, rows))[:, :rows * length].reshape(heads, rows, length)
    return flat[:, :, rows - 1:rows - 1 + cols]


def _diff_lambda(lam_ref):
    l = lam_ref[...]
    a = jnp.sum(l[0:1, :] * l[1:2, :], axis=-1, keepdims=True)
    b = jnp.sum(l[2:3, :] * l[3:4, :], axis=-1, keepdims=True)
    return jnp.exp(a) - jnp.exp(b) + LAM_INIT


def _ada_kernel(c_ref, w_ref, b_ref, o_ref):
    c = c_ref[...]
    o_ref[...] = _mm(c * jax.nn.sigmoid(c), w_ref[...]) + b_ref[...]


def _ada(c, w_ada, b_ada):
    m = c.shape[0]
    n = w_ada.shape[1]
    tn = 512
    return pl.pallas_call(
        _ada_kernel,
        grid=(n // tn,),
        in_specs=[pl.BlockSpec((m, D_MODEL), lambda j: (0, 0)),
                  pl.BlockSpec((D_MODEL, tn), lambda j: (0, j)),
                  pl.BlockSpec((1, tn), lambda j: (0, j))],
        out_specs=pl.BlockSpec((m, tn), lambda j: (0, j)),
        out_shape=jax.ShapeDtypeStruct((m, n), F32),
        compiler_params=_cparams(("arbitrary",)),
        name="ada",
    )(c, w_ada, b_ada.reshape(1, n))


PROJ_GROUPS = (("qa", 512, BF16), ("da", 1024, F32), ("qb", 512, BF16), ("nsa", 512, F32), ("kvw", 256, F32),
               ("gate", 128, F32), ("ma", 1024, F32), ("mb", 1024, F32))
PROJ_KEY_GROUPS = (("ksel", 256), ("kwin", 256))
STATE_ROWS = {"da": DA_HEADS, "nsa": NSA_KV}
LOG2E = 1.4426950408889634
Q_SCALE = {"qa": DA_HD ** -0.5 * LOG2E, "qb": NSA_HD ** -0.5 * LOG2E}
PROJ_W = sum(w for _, w, _ in PROJ_GROUPS)
PROJ_W_KV = PROJ_W + sum(w for _, w in PROJ_KEY_GROUPS)
PROJ_VT = (("davt", DA_HEADS * DA_VD), ("selvt", NSA_KV * NSA_HD), ("winvt", NSA_KV * NSA_HD))


def _perm_w_in(w_in):
    parts = jnp.split(w_in, np.cumsum(SPLIT_SIZES)[:-1].tolist(), axis=1)
    qa, ka, va, qn, kvc, kvs, kvw, gb, ma, mb = parts
    da = jnp.concatenate([jnp.concatenate([ka[:, h * 128:(h + 1) * 128], va[:, h * 128:(h + 1) * 128]], 1)
                          for h in range(DA_HEADS)], 1)
    nsa = jnp.concatenate([jnp.concatenate([kvc[:, g * 128:(g + 1) * 128], kvs[:, g * 128:(g + 1) * 128]], 1)
                           for g in range(NSA_KV)], 1)
    gate = jnp.pad(gb, ((0, 0), (0, LANE - gb.shape[1])))
    keys2 = lambda kv: jnp.concatenate([kv[:, g * 128:g * 128 + 64] for g in range(NSA_KV) for _ in range(2)], 1)
    vals = lambda kv: jnp.concatenate([kv[:, g * 128 + 64:(g + 1) * 128] for g in range(NSA_KV)], 1)
    cols = dict(qa=qa, da=da, qb=qn, nsa=nsa, kvw=kvw, gate=gate, ma=ma, mb=mb, ksel=keys2(kvs), kwin=keys2(kvw))
    names = [n for n, _, _ in PROJ_GROUPS] + [n for n, _ in PROJ_KEY_GROUPS]
    w_perm = jnp.concatenate([cols[n] for n in names], 1).astype(BF16)
    w_vt = jnp.transpose(jnp.concatenate([va, vals(kvs), vals(kvw)], 1)).astype(BF16)
    return w_perm, w_vt


def _proj_kernel(x_ref, ada_ref, w_ref, *refs, with_kv):
    h = (x_ref[...] * (1.0 + ada_ref[1]) + ada_ref[0]).astype(BF16)
    o_refs = refs[1:] if with_kv else refs
    off = 0
    k = 0
    for name, width, dt in PROJ_GROUPS:
        acc = jnp.dot(h, w_ref[:, off:off + width], preferred_element_type=F32)
        if with_kv and name in STATE_ROWS:
            units = STATE_ROWS[name]
            tm = acc.shape[0]
            for u in range(units):
                for half in range(2):
                    o_refs[k][pl.ds(half * units + u, tm, stride=2 * units), :] = (
                        acc[:, (u * 2 + half) * LANE:(u * 2 + half + 1) * LANE])
        else:
            o_refs[k][...] = acc.astype(dt)
        k += 1
        if with_kv and name in Q_SCALE:
            o_refs[k][...] = (acc * Q_SCALE[name]).astype(BF16)
            k += 1
        if with_kv and name == "da":
            o_refs[k][...] = jnp.concatenate([acc[:, hd * 256:hd * 256 + 128] for hd in range(DA_HEADS)],
                                             axis=1).astype(BF16)
            k += 1
        off += width
    if with_kv:
        for _, width in PROJ_KEY_GROUPS:
            o_refs[k][...] = jnp.dot(h, w_ref[:, off:off + width], preferred_element_type=F32).astype(BF16)
            k += 1
            off += width
        vt = _mm_nt(refs[0][...], h)
        r0 = 0
        for _, rows in PROJ_VT:
            o_refs[k][...] = vt[r0:r0 + rows].astype(BF16)
            k += 1
            r0 += rows


def _ada_spec(ada, tm, tiles_per_group):
    r = ada.shape[2]
    return pl.BlockSpec((6, None, r, D_MODEL), lambda i, *_: (0, i // tiles_per_group, 0, 0))


def _proj(x2d, ada, w_perm, w_vt, tm, tiles_per_group):
    m = x2d.shape[0]
    with_kv = w_vt is not None
    outs = []
    for name, w, dt in PROJ_GROUPS:
        if with_kv and name in STATE_ROWS:
            rows_pos = w // LANE
            outs.append((name, (m * rows_pos, LANE), (tm * rows_pos, LANE), dt))
        else:
            outs.append((name, (m, w), (tm, w), dt))
        if with_kv and name in Q_SCALE:
            outs.append((name + "2", (m, w), (tm, w), BF16))
        if with_kv and name == "da":
            outs.append(("dak", (m, 512), (tm, 512), BF16))
    in_specs = [pl.BlockSpec((tm, D_MODEL), lambda i: (i, 0)),
                _ada_spec(ada, tm, tiles_per_group),
                pl.BlockSpec((D_MODEL, PROJ_W_KV if with_kv else PROJ_W), lambda i: (0, 0))]
    args = [x2d, ada, w_perm]
    if with_kv:
        assert tm == TKI
        outs += [(n, (m, w), (tm, w), BF16) for n, w in PROJ_KEY_GROUPS]
        outs += [(n, (m // tm, r, tm), (None, r, tm), BF16) for n, r in PROJ_VT]
        in_specs.append(pl.BlockSpec(w_vt.shape, lambda i: (0, 0)))
        args.append(w_vt)
    res = pl.pallas_call(
        functools.partial(_proj_kernel, with_kv=with_kv),
        grid=(m // tm,),
        in_specs=in_specs,
        out_specs=[pl.BlockSpec(blk, (lambda i: (i, 0)) if len(blk) == 2 else (lambda i: (i, 0, 0)))
                   for _, _, blk, _ in outs],
        out_shape=[jax.ShapeDtypeStruct(shape, dt) for _, shape, _, dt in outs],
        compiler_params=_cparams(("arbitrary",)),
        name="proj",
    )(*args)
    return {n: o for (n, _, _, _), o in zip(outs, res)}


class _Chain:
    def __init__(self, s_bufs, p_bufs, al_bufs, m_ref, acc_ref, score_fn, vext_fn, bias_fn):
        self.s, self.p, self.al = s_bufs, p_bufs, al_bufs
        self.m, self.acc = m_ref, acc_ref
        self.score_fn, self.vext_fn, self.bias_fn = score_fn, vext_fn, bias_fn
        self.maps = m_ref.shape[0]
        self.cur = 0

    def prime(self, j):
        self.cur = 0
        self.m[...] = jnp.full(self.m.shape, NEG, F32)
        self.acc[...] = jnp.zeros(self.acc.shape, F32)
        self.p[1][...] = jnp.zeros(self.p[1].shape, BF16)
        self.al[1][...] = jnp.ones(self.al[1].shape, F32)
        self.score_fn(j, self.s[0])

    def _finish(self, k, buf, pv):
        self.acc[k] = self.acc[k] * self.al[buf][k, 0:1, :] + pv

    def step(self, j_prev, j_next, table=None, valid=None, pen=None, prev_valid=None):
        c, o = self.cur, 1 - self.cur
        vext = self.vext_fn(j_prev)
        if prev_valid is not None:
            vext = jnp.where(prev_valid, vext, jnp.zeros_like(vext))
        pv = [jnp.dot(vext, self.p[o][k], preferred_element_type=F32) for k in range(self.maps)]
        if j_next is not None:
            self.score_fn(j_next, self.s[o])
        for k in range(self.maps):
            for l0 in range(0, self.m.shape[2], LANE):
                cols = slice(l0, l0 + LANE)
                s = self.s[c][k, :, cols]
                if table is not None:
                    s = s + self.bias_fn(table, l0)
                if pen is not None:
                    s = s + pen
                m_cur = jnp.max(s, axis=0, keepdims=True)
                if valid is not None:
                    m_cur = jnp.where(valid, m_cur, NEG)
                m_old = self.m[k, 0:1, cols]
                m_new = jnp.maximum(m_old, m_cur)
                self.p[c][k, :, cols] = jnp.exp2(s - m_new).astype(BF16)
                self.al[c][k, :, cols] = jnp.broadcast_to(jnp.exp2(m_old - m_new), (8, LANE))
                self.m[k, :, cols] = jnp.broadcast_to(m_new, (8, LANE))
            self._finish(k, o, pv[k])
        self.cur = o

    def step_eager(self, j_cur, j_next, table=None, valid=None, pen=None):
        c, o = self.cur, 1 - self.cur
        if j_next is not None:
            self.score_fn(j_next, self.s[o])
        vext = self.vext_fn(j_cur)
        if valid is not None:
            vext = jnp.where(valid, vext, jnp.zeros_like(vext))
        for k in range(self.maps):
            ps, alphas = [], []
            for l0 in range(0, self.m.shape[2], LANE):
                cols = slice(l0, l0 + LANE)
                s = self.s[c][k, :, cols]
                if table is not None:
                    s = s + self.bias_fn(table, l0)
                if pen is not None:
                    s = s + pen
                m_cur = jnp.max(s, axis=0, keepdims=True)
                if valid is not None:
                    m_cur = jnp.where(valid, m_cur, NEG)
                m_old = self.m[k, 0:1, cols]
                m_new = jnp.maximum(m_old, m_cur)
                ps.append(jnp.exp2(s - m_new).astype(BF16))
                alphas.append(jnp.exp2(m_old - m_new))
                self.m[k, :, cols] = jnp.broadcast_to(m_new, (8, LANE))
            self.acc[k] = (self.acc[k] * jnp.concatenate(alphas, axis=1)
                           + jnp.dot(vext, jnp.concatenate(ps, axis=1), preferred_element_type=F32))
        self.cur = o

    def flush(self, j_prev):
        o = 1 - self.cur
        vext = self.vext_fn(j_prev)
        for k in range(self.maps):
            self._finish(k, o, jnp.dot(vext, self.p[o][k], preferred_element_type=F32))


def _chain_scratch(maps, acc_rows, queries):
    return [pltpu.VMEM((maps, TKI, queries), F32), pltpu.VMEM((maps, TKI, queries), F32),
            pltpu.VMEM((maps, TKI, queries), BF16), pltpu.VMEM((maps, TKI, queries), BF16),
            pltpu.VMEM((maps, 8, queries), F32), pltpu.VMEM((maps, 8, queries), F32),
            pltpu.VMEM((maps, 8, queries), F32), pltpu.VMEM((maps, acc_rows, queries), F32)]


def _key_rows(ref, j):
    return pl.ds(pl.multiple_of(j * TKI, TKI), TKI)


def _da_kernel(q_ref, k_ref, vt_ref, tb_ref, lam_ref, o_ref, s_a, s_b, p_a, p_b, al_a, al_b, m_ref, acc_ref):
    qi = pl.program_id(2)
    q = q_ref[...]
    lane = lax.broadcasted_iota(jnp.int32, q.shape, 1)
    zero = jnp.zeros_like(q)
    qt_maps = tuple(jnp.where(keep, q, zero).astype(F32).T.astype(BF16) for keep in (lane < DA_HD, lane >= DA_HD))
    ones = jnp.ones((ONES_ROWS, TKI), BF16)

    def scores(j, dst):
        kk = k_ref[_key_rows(k_ref, j), :]
        for c in range(2):
            dst[c] = jnp.dot(kk, qt_maps[c], preferred_element_type=F32)

    chain = _Chain((s_a, s_b), (p_a, p_b), (al_a, al_b), m_ref, acc_ref, scores,
                   lambda j: jnp.concatenate([vt_ref[j], ones], axis=0),
                   lambda table, l0: tb_ref[table, :, l0:l0 + LANE])
    n_far = jnp.maximum(2 * qi - 1, 0)
    j_d = 2 * qi
    j_s = jnp.maximum(2 * qi - 1, 0)
    chain.prime(jnp.where(n_far > 0, 0, j_d))

    def pair(i, carry):
        a = 2 * i
        b = jnp.minimum(a + 1, n_far - 1)
        chain.step_eager(a, b)
        chain.step_eager(b, jnp.where(a + 2 < n_far, a + 2, j_d), valid=a + 1 < n_far)
        return carry

    lax.fori_loop(0, lax.shift_right_logical(n_far + 1, 1), pair, 0)
    chain.step_eager(j_d, j_s, table=1)
    chain.step_eager(j_s, j_d + 1, table=0, pen=jnp.where(qi > 0, 0.0, NEG))
    chain.step_eager(j_d + 1, None, table=2)
    lam = _diff_lambda(lam_ref)
    o_t = (acc_ref[0, 0:DA_VD, :] / acc_ref[0, DA_VD:DA_VD + 1, :]
           - lam * (acc_ref[1, 0:DA_VD, :] / acc_ref[1, DA_VD:DA_VD + 1, :]))
    o_ref[...] = o_t.T


def _da_prompt(qa, dak, davt, gd_a, da_lambda):
    b, s, _ = qa.shape
    tb = jnp.stack([_toeplitz(gd_a, TKI, TDA, TKI), _toeplitz(gd_a, 0, TDA, TKI, NEG),
                    _toeplitz(gd_a, -TKI, TDA, TKI, NEG)], axis=1)
    tb = jnp.swapaxes(tb, 2, 3) * LOG2E
    acc_rows = DA_VD + ONES_ROWS
    return pl.pallas_call(
        _da_kernel,
        grid=(b, DA_HEADS, s // TDA),
        in_specs=[pl.BlockSpec((None, TDA, 128), lambda bi, h, qi: (bi, qi, h)),
                  pl.BlockSpec((None, s, 128), lambda bi, h, qi: (bi, 0, h)),
                  pl.BlockSpec((None, s // TKI, DA_VD, TKI), lambda bi, h, qi: (bi, 0, h, 0)),
                  pl.BlockSpec((None, 3, TKI, TDA), lambda bi, h, qi: (h, 0, 0, 0)),
                  pl.BlockSpec((4, DA_HD), lambda bi, h, qi: (0, 0))],
        out_specs=pl.BlockSpec((None, TDA, 128), lambda bi, h, qi: (bi, qi, h)),
        out_shape=jax.ShapeDtypeStruct((b, s, DA_HEADS * DA_VD), F32),
        scratch_shapes=_chain_scratch(2, acc_rows, TDA),
        compiler_params=_cparams(("arbitrary", "arbitrary", "arbitrary")),
        name="da_prompt",
    )(qa, dak, davt, tb, da_lambda)


def _cmp_weights(cmp_w1, cmp_w2):
    w1 = jnp.zeros((CMP_STRIDE, 2, NSA_HD, 2, 2, CMP_HID), F32)
    for half in range(2):
        for c in range(2):
            blk = cmp_w1[c, half * CMP_STRIDE:(half + 1) * CMP_STRIDE]
            w1 = w1.at[:, c, :, half, c, :].set(blk)
    w1 = w1.reshape(CMP_STRIDE * 2 * NSA_HD, 2 * 2 * CMP_HID).astype(BF16)
    w2 = jnp.zeros((2, CMP_HID, 2, 2, NSA_HD), F32)
    for c in range(2):
        for rep in range(2):
            w2 = w2.at[c, :, c, rep, :].set(cmp_w2[c])
    w2 = w2.reshape(2 * CMP_HID, 2 * 2 * NSA_HD).astype(BF16)
    return w1, w2


def _cmp1_rows_kernel(x_ref, w_ref, o_ref, *, tm):
    rows_pos = 2 * NSA_KV
    for g in range(NSA_KV):
        xg = jnp.concatenate([x_ref[pl.ds(l * rows_pos + g, tm, stride=CMP_STRIDE * rows_pos), :]
                              for l in range(CMP_STRIDE)], axis=1)
        o_ref[:, g * 256:(g + 1) * 256] = _mm(xg, w_ref[...])


def _cmp_stage1_rows(rows2d, w1):
    rows_chunk = CMP_STRIDE * 2 * NSA_KV
    r = rows2d.shape[0] // rows_chunk
    tm = math.gcd(r, 128)
    return pl.pallas_call(
        functools.partial(_cmp1_rows_kernel, tm=tm),
        grid=(r // tm,),
        in_specs=[pl.BlockSpec((tm * rows_chunk, LANE), lambda i: (i, 0)),
                  pl.BlockSpec(w1.shape, lambda i: (0, 0))],
        out_specs=pl.BlockSpec((tm, 512), lambda i: (i, 0)),
        out_shape=jax.ShapeDtypeStruct((r, 512), F32),
        compiler_params=_cparams(("arbitrary",)),
        name="cmp_stage1_rows",
    )(rows2d, w1)


def _cmp_stage2(ab, pe_ref, w1c_ref, w2_ref):
    n = ab.shape[0]
    cst = jnp.concatenate([_mm(pe_ref[c], w1c_ref[c])[0:1, :] for c in range(2)], axis=1)
    outs = []
    for g in range(NSA_KV):
        a = ab[:, g * 256:g * 256 + 128]
        bn = pltpu.roll(ab[:, g * 256 + 128:g * 256 + 256], n - 1, 0)
        hid = jax.nn.gelu(a + bn + cst)
        outs.append(_mm(hid, w2_ref[...]))
    return jnp.concatenate(outs, axis=1)


def _cmp2_kernel(ab_ref, pe_ref, w1c_ref, w2_ref, o_ref):
    o_ref[...] = _cmp_stage2(ab_ref[...], pe_ref, w1c_ref, w2_ref).astype(BF16)


def _cmp_stage2_prompt(ab, pe8, w1c, w2):
    b, n, _ = ab.shape
    return pl.pallas_call(
        _cmp2_kernel,
        grid=(b,),
        in_specs=[pl.BlockSpec((None, n, 512), lambda i: (i, 0, 0)),
                  pl.BlockSpec(pe8.shape, lambda i: (0, 0, 0)),
                  pl.BlockSpec(w1c.shape, lambda i: (0, 0, 0)),
                  pl.BlockSpec(w2.shape, lambda i: (0, 0))],
        out_specs=pl.BlockSpec((None, n, 512), lambda i: (i, 0, 0)),
        out_shape=jax.ShapeDtypeStruct((b, n, 512), BF16),
        compiler_params=_cparams(("arbitrary",)),
        name="cmp_stage2",
    )(ab, pe8, w1c, w2)


def _ovl_t(n_chunks):
    n = np.arange(n_chunks)[:, None]
    j = np.arange(SEL_LANES)[None, :]
    ovl = (n * CMP_STRIDE < j * SEL_BLOCK + SEL_BLOCK) & (j * SEL_BLOCK < n * CMP_STRIDE + CMP_LEN)
    ovl &= n < n_chunks - 1
    return jnp.asarray(ovl.astype(np.float32)).astype(BF16)


def _topk_picks_cols(score_t, n_pick):
    blk = lax.broadcasted_iota(jnp.int32, score_t.shape, 0).astype(F32)
    picks = jnp.zeros(score_t.shape, F32)
    sc = score_t
    for it in range(n_pick):
        m = jnp.max(sc, axis=0, keepdims=True)
        first = jnp.min(jnp.where(sc == m, blk, float(SEL_LANES)), axis=0, keepdims=True)
        picks = jnp.where(blk == float(it), first, picks)
        sc = jnp.where(blk == first, REMOVED, sc)
    return picks


def _topk_mask_cols(score_t, n_pick):
    blk = lax.broadcasted_iota(jnp.int32, score_t.shape, 0).astype(F32)
    sel = jnp.zeros(score_t.shape, F32)
    sc = score_t
    for _ in range(n_pick):
        m = jnp.max(sc, axis=0, keepdims=True)
        first = jnp.min(jnp.where(sc == m, blk, float(SEL_LANES)), axis=0, keepdims=True)
        hit = blk == first
        sel = jnp.where(hit, 1.0, sel)
        sc = jnp.where(hit, REMOVED, sc)
    return sel


def _stack_heads(q):
    lane = lax.broadcasted_iota(jnp.int32, (q.shape[0], LANE), 1)
    zero = jnp.zeros((q.shape[0], LANE), q.dtype)
    parts = []
    for hp in range(NSA_HPG):
        blk = q[:, (hp // 2) * LANE:(hp // 2 + 1) * LANE]
        keep = (lane < NSA_HD) if hp % 2 == 0 else (lane >= NSA_HD)
        parts.append(jnp.where(keep, blk, zero))
    return jnp.concatenate(parts, axis=0)


def _unstack_heads(o, tq):
    lane = lax.broadcasted_iota(jnp.int32, (tq, LANE), 1)
    pairs = [jnp.where(lane < NSA_HD, o[(2 * m) * tq:(2 * m + 1) * tq], o[(2 * m + 1) * tq:(2 * m + 2) * tq])
             for m in range(2)]
    return jnp.concatenate(pairs, axis=1)


def _nsa_cmp_kernel(q_ref, kcvc_ref, ovl_ref, oc_ref, sel_ref, *, n_cmp, n_slc):
    qi = pl.program_id(2)
    n_chunks = kcvc_ref.shape[0]
    qs = _stack_heads(q_ref[...])
    kc2 = kcvc_ref[:, 0:128]
    vc2 = kcvc_ref[:, 128:256]
    s = _mm_nt(qs, kc2) * (NSA_HD ** -0.5)
    rows = NSA_HPG * TQ
    t = qi * TQ + (lax.broadcasted_iota(jnp.int32, (rows, n_chunks), 0) & (TQ - 1))
    n = lax.broadcasted_iota(jnp.int32, (rows, n_chunks), 1)
    p = _masked_softmax(s, (n * CMP_STRIDE + (CMP_LEN - 1) <= t) & (n < n_cmp))
    oc_ref[...] = _unstack_heads(_mm(p, vc2), TQ)
    psum = p[0:TQ] + p[TQ:2 * TQ] + p[2 * TQ:3 * TQ] + p[3 * TQ:4 * TQ]
    imp = _mm3(psum, ovl_ref[...])
    tq = qi * TQ + lax.broadcasted_iota(jnp.int32, (TQ, SEL_LANES), 0)
    blk = lax.broadcasted_iota(jnp.int32, (TQ, SEL_LANES), 1)
    cur = lax.shift_right_logical(tq, 6)
    forced = (blk == 0) | (blk == cur) | (blk == cur - 1)
    score = jnp.where(forced, SEL_BIG, jnp.where(blk * SEL_BLOCK <= tq, imp, -SEL_BIG))
    score = jnp.where(blk < n_slc, score, REMOVED)
    sel_t = _topk_mask_cols(score.T, min(N_SEL, n_slc))
    sel_ref[...] = jnp.where(sel_t > 0.5, 0.0, NEG).astype(BF16)


def _nsa_cmp_prompt(qb, kcvc, ovl):
    b, s, _ = qb.shape
    n_chunks = kcvc.shape[1]
    kern = functools.partial(_nsa_cmp_kernel, n_cmp=n_chunks - 1, n_slc=s // SEL_BLOCK)
    return pl.pallas_call(
        kern,
        grid=(b, NSA_KV, s // TQ),
        in_specs=[pl.BlockSpec((None, TQ, 256), lambda bi, g, qi: (bi, qi, g)),
                  pl.BlockSpec((None, n_chunks, 256), lambda bi, g, qi: (bi, 0, g)),
                  pl.BlockSpec(ovl.shape, lambda bi, g, qi: (0, 0))],
        out_specs=[pl.BlockSpec((None, TQ, 256), lambda bi, g, qi: (bi, qi, g)),
                   pl.BlockSpec((None, None, TQ, SEL_LANES), lambda bi, g, qi: (bi, g, qi, 0))],
        out_shape=[jax.ShapeDtypeStruct((b, s, 512), F32),
                   jax.ShapeDtypeStruct((b, NSA_KV, s, SEL_LANES), BF16)],
        compiler_params=_cparams(("arbitrary", "arbitrary", "arbitrary")),
        name="nsa_cmp",
    )(qb, kcvc, ovl)


def _nsa_sw_kernel(q_ref, ks_ref, kw_ref, vs_ref, vw_ref, sel_ref, e_ref, tb_ref, os_ref, ow_ref, *scratch):
    m = pl.program_id(2)
    qs = _stack_heads(q_ref[...])
    qs_t = qs.astype(F32).T.astype(BF16)
    sel_t = jnp.concatenate([sel_ref[t * TQ:(t + 1) * TQ, :] for t in range(TQS // TQ)] * NSA_HPG, axis=1)
    qs_sel_t = jnp.concatenate([qs_t, sel_t], axis=0)
    ones = jnp.ones((ONES_ROWS, TKI), BF16)

    def sel_scores(j, dst):
        keys = jnp.concatenate([ks_ref[_key_rows(ks_ref, j), :], e_ref[j]], axis=1)
        dst[0] = jnp.dot(keys, qs_sel_t, preferred_element_type=F32)

    def win_scores(j, dst):
        dst[0] = jnp.dot(kw_ref[_key_rows(kw_ref, j), :], qs_t, preferred_element_type=F32)

    bias = lambda table, l0: tb_ref[table, :, l0:l0 + LANE]
    sel = _Chain(scratch[0:2], scratch[2:4], scratch[4:6], scratch[6], scratch[7], sel_scores,
                 lambda j: jnp.concatenate([vs_ref[j], ones], axis=0), bias)
    win = _Chain(scratch[8:10], scratch[10:12], scratch[12:14], scratch[14], scratch[15], win_scores,
                 lambda j: jnp.concatenate([vw_ref[j], ones], axis=0), bias)

    def heads_out(acc_ref):
        o_t = acc_ref[0, 0:NSA_HD, :] / acc_ref[0, NSA_HD:NSA_HD + 1, :]
        pairs = [jnp.concatenate([o_t[:, (2 * k) * TQS:(2 * k + 1) * TQS],
                                  o_t[:, (2 * k + 1) * TQS:(2 * k + 2) * TQS]], axis=0).T
                 for k in range(NSA_HPG // 2)]
        return jnp.concatenate(pairs, axis=1)

    n_far = jnp.maximum(m - 1, 0)
    j1 = jnp.maximum(m - 1, 0)
    j2 = jnp.maximum(m - 2, 0)
    pen1 = jnp.where(m >= 1, 0.0, NEG)
    pen2 = jnp.where(m >= 2, 0.0, NEG)
    sel.prime(jnp.where(n_far > 0, 0, m))
    win.prime(m)

    def pair(i, carry):
        a = 2 * i
        sel.step(jnp.maximum(a - 1, 0), jnp.minimum(a + 1, n_far - 1))
        sel.step(a, jnp.where(a + 2 < n_far, a + 2, m), valid=a + 1 < n_far)
        return carry

    lax.fori_loop(0, lax.shift_right_logical(n_far + 1, 1), pair, 0)
    sel.step(jnp.maximum(n_far - 1, 0), j1, table=0, prev_valid=(n_far & 1) == 0)
    win.step(m, j1, table=0)
    sel.step(m, None, table=1, pen=pen1)
    win.step(m, j2, table=1, pen=pen1)
    sel.flush(j1)
    win.step(j1, None, table=2, pen=pen2)
    win.flush(j2)
    os_ref[...] = heads_out(sel.acc)
    ow_ref[...] = heads_out(win.acc)


def _nsa_sw_prompt(qb, ksel, kwin, selvt, winvt, sel, gd_b):
    b, s, _ = qb.shape
    assert TQS == TKI and WINDOW == 2 * TKI
    cols = NSA_HPG * TQS
    i = np.arange(TQS)[:, None]
    j = np.arange(TKI)[None, :]
    edge = jnp.asarray(np.broadcast_to(np.where(j >= i, 0.0, NEG).astype(np.float32), (NSA_HEADS, TQS, TKI)))
    tb = jnp.stack([_toeplitz(gd_b, 0, TQS, TKI, NEG), _toeplitz(gd_b, TKI, TQS, TKI), edge])
    tb = tb.reshape(3, NSA_KV, NSA_HPG, TQS, TKI).transpose(1, 0, 4, 2, 3).reshape(NSA_KV, 3, TKI, cols) * LOG2E
    nk = s // TKI
    e = np.zeros((nk, TKI, SEL_LANES), np.float32)
    for kj in range(nk):
        for k in range(TKI):
            e[kj, k, kj * (TKI // SEL_BLOCK) + k // SEL_BLOCK] = 1.0
    e = jnp.asarray(e).astype(BF16)
    acc_rows = NSA_HD + ONES_ROWS
    return pl.pallas_call(
        _nsa_sw_kernel,
        grid=(b, NSA_KV, s // TQS),
        in_specs=[pl.BlockSpec((None, TQS, 256), lambda bi, g, qi: (bi, qi, g)),
                  pl.BlockSpec((None, s, LANE), lambda bi, g, qi: (bi, 0, g)),
                  pl.BlockSpec((None, s, LANE), lambda bi, g, qi: (bi, 0, g)),
                  pl.BlockSpec((None, nk, NSA_HD, TKI), lambda bi, g, qi: (bi, 0, g, 0)),
                  pl.BlockSpec((None, nk, NSA_HD, TKI), lambda bi, g, qi: (bi, 0, g, 0)),
                  pl.BlockSpec((None, None, TQS, SEL_LANES), lambda bi, g, qi: (bi, g, qi, 0)),
                  pl.BlockSpec(e.shape, lambda bi, g, qi: (0, 0, 0)),
                  pl.BlockSpec((None, 3, TKI, cols), lambda bi, g, qi: (g, 0, 0, 0))],
        out_specs=[pl.BlockSpec((None, TQS, 256), lambda bi, g, qi: (bi, qi, g)),
                   pl.BlockSpec((None, TQS, 256), lambda bi, g, qi: (bi, qi, g))],
        out_shape=[jax.ShapeDtypeStruct((b, s, 512), F32), jax.ShapeDtypeStruct((b, s, 512), F32)],
        scratch_shapes=_chain_scratch(1, acc_rows, cols) + _chain_scratch(1, acc_rows, cols),
        compiler_params=_cparams(("arbitrary", "arbitrary", "arbitrary")),
        name="nsa_sel_win",
    )(qb, ksel, kwin, selvt, winvt, sel, e, tb)


def _layer_norm(x, g, b):
    mu = jnp.mean(x, axis=-1, keepdims=True)
    xc = x - mu
    var = jnp.mean(xc * xc, axis=-1, keepdims=True)
    return xc * lax.rsqrt(var + LN_EPS) * g + b


def _gate_expand():
    e = np.zeros((3, LANE, NSA_HEADS * NSA_HD), np.float32)
    for h in range(NSA_HEADS):
        for j in range(3):
            e[j, h * 3 + j, h * NSA_HD:(h + 1) * NSA_HD] = 1.0
    return jnp.asarray(e).astype(BF16)


def _tail1_kernel(oa_ref, oc_ref, os_ref, ow_ref, gate_ref, ma_ref, mb_ref, x_ref, ada_ref,
                  wa_ref, wb_ref, wo_ref, sub_ref, eg_ref, g1_ref, b1_ref, o_ref):
    oa = oa_ref[...]
    parts = []
    for h in range(DA_HEADS):
        of = oa[:, h * DA_VD:(h + 1) * DA_VD]
        rr = lax.rsqrt(jnp.mean(of * of, axis=-1, keepdims=True) + RMS_EPS)
        parts.append(of * rr * sub_ref[...] * (1.0 - LAM_INIT))
    oan = jnp.concatenate(parts, axis=1)
    sg = jax.nn.sigmoid(gate_ref[...])
    ob = (_mm3(sg, eg_ref[0]) * oc_ref[...] + _mm3(sg, eg_ref[1]) * os_ref[...]
          + _mm3(sg, eg_ref[2]) * ow_ref[...])
    y = (jax.nn.sigmoid(ma_ref[...]) * _mm(oan, wa_ref[...])
         + jax.nn.sigmoid(mb_ref[...]) * _mm(ob, wb_ref[...]))
    z = ALPHA * x_ref[...] + ada_ref[2] * _mm(y, wo_ref[...])
    o_ref[...] = _layer_norm(z, g1_ref[...], b1_ref[...])


def _tail1(oa, oc, os_, ow, gate, ma, mb, x2d, ada, wa, wb, wo, sub, eg, g1, b1, tm, tiles_per_group):
    m = x2d.shape[0]
    row = lambda w: pl.BlockSpec((tm, w), lambda i: (i, 0))
    full = lambda a: pl.BlockSpec(a.shape, lambda i: (0,) * a.ndim)
    return pl.pallas_call(
        _tail1_kernel,
        grid=(m // tm,),
        in_specs=[row(512), row(512), row(512), row(512), row(128), row(1024), row(1024), row(1024),
                  _ada_spec(ada, tm, tiles_per_group),
                  full(wa), full(wb), full(wo), full(sub), full(eg), full(g1), full(b1)],
        out_specs=row(1024),
        out_shape=jax.ShapeDtypeStruct((m, D_MODEL), F32),
        compiler_params=_cparams(("arbitrary",)),
        name="tail_merge",
    )(oa, oc, os_, ow, gate, ma, mb, x2d, ada, wa, wb, wo, sub, eg, g1, b1)


def _tail2_kernel(x_ref, ada_ref, wu_ref, wd_ref, g2_ref, b2_ref, o_ref, h_scr, acc):
    f = pl.program_id(1)

    @pl.when(f == 0)
    def _():
        h_scr[...] = (x_ref[...] * (1.0 + ada_ref[4]) + ada_ref[3]).astype(BF16)
        acc[...] = jnp.zeros(acc.shape, F32)

    u = jnp.maximum(jnp.dot(h_scr[...], wu_ref[...], preferred_element_type=F32), 0.0)
    acc[...] += _mm(u * u, wd_ref[...])

    @pl.when(f == pl.num_programs(1) - 1)
    def _():
        z = ALPHA * x_ref[...] + ada_ref[5] * acc[...]
        o_ref[...] = _layer_norm(z, g2_ref[...], b2_ref[...])


def _tail2(x1, ada, wu, wd, g2, b2, tm, tiles_per_group):
    m = x1.shape[0]
    tf = 1024
    return pl.pallas_call(
        _tail2_kernel,
        grid=(m // tm, D_FF // tf),
        in_specs=[pl.BlockSpec((tm, D_MODEL), lambda i, f: (i, 0)),
                  _ada_spec(ada, tm, tiles_per_group),
                  pl.BlockSpec((D_MODEL, tf), lambda i, f: (0, f)),
                  pl.BlockSpec((tf, D_MODEL), lambda i, f: (f, 0)),
                  pl.BlockSpec((1, D_MODEL), lambda i, f: (0, 0)),
                  pl.BlockSpec((1, D_MODEL), lambda i, f: (0, 0))],
        out_specs=pl.BlockSpec((tm, D_MODEL), lambda i, f: (i, 0)),
        out_shape=jax.ShapeDtypeStruct((m, D_MODEL), F32),
        scratch_shapes=[pltpu.VMEM((tm, D_MODEL), BF16), pltpu.VMEM((tm, D_MODEL), F32)],
        compiler_params=_cparams(("arbitrary", "arbitrary")),
        name="tail_mlp",
    )(x1, ada, wu, wd, g2, b2)


def _da_decode_kernel(pt_ref, *refs):
    pages = refs[:PP]
    q_ref, kn_ref, vn_ref, bl_ref, b0_ref, lam_ref, o_ref, m_ref, l_ref, a_ref = refs[PP:]
    j = pl.program_id(1)
    last = j == pl.num_programs(1) - 1
    scale = DA_HD ** -0.5
    rows_pg = PAGE * 2 * DA_HEADS

    @pl.when(j == 0)
    def _():
        m_ref[...] = jnp.full(m_ref.shape, NEG, F32)
        l_ref[...] = jnp.zeros(l_ref.shape, F32)
        a_ref[...] = jnp.zeros(a_ref.shape, F32)

    q = q_ref[...]
    row = lax.broadcasted_iota(jnp.int32, (8, rows_pg), 0)
    col = lax.broadcasted_iota(jnp.int32, (8, rows_pg), 1)
    cmask = jnp.where((col & 7) == lax.shift_right_logical(row, 1), 0.0, NEG)
    scores = []
    xs = []
    for k in range(PP):
        x = pages[k][...].astype(BF16)
        sc = _mm_nt(q, x) * scale + cmask
        if k == PP - 1:
            sc = sc + jnp.where(last, bl_ref[...], 0.0)
        scores.append(sc)
        xs.append(x)
    s = jnp.concatenate(scores, axis=1)
    m_old = m_ref[...]
    m_new = jnp.maximum(m_old, jnp.max(s, axis=-1, keepdims=True))
    p = jnp.exp(s - m_new)
    alpha = jnp.exp(m_old - m_new)
    l_ref[...] = alpha * l_ref[...] + jnp.sum(p, axis=-1, keepdims=True)
    acc = alpha * a_ref[...]
    for k in range(PP):
        pv = pltpu.roll(p[:, k * rows_pg:(k + 1) * rows_pg], DA_HEADS, 1)
        acc = acc + _mm(pv, xs[k])
    a_ref[...] = acc
    m_ref[...] = m_new

    @pl.when(last)
    def _():
        s_new = jnp.sum(q.astype(F32) * kn_ref[...], axis=-1, keepdims=True) * scale + b0_ref[:, 0:1]
        m_o = m_ref[...]
        m_n = jnp.maximum(m_o, s_new)
        p_new = jnp.exp(s_new - m_n)
        al = jnp.exp(m_o - m_n)
        raw = (al * a_ref[...] + p_new * vn_ref[...]) / (al * l_ref[...] + p_new)
        o_ref[...] = raw - _diff_lambda(lam_ref) * pltpu.roll(raw, 7, 0)


def _da_decode(page_table, cache_rows, q8, k_new, v_new, bl, b0, da_lambda):
    b, n_pages = page_table.shape
    rows_pg = PAGE * 2 * DA_HEADS
    page_spec = lambda k: pl.BlockSpec((rows_pg, LANE), lambda bi, j, pt: (pt[bi, j * PP + k], 0))
    per_row = lambda: pl.BlockSpec((None, 8, LANE), lambda bi, j, pt: (bi, 0, 0))
    grid_spec = pltpu.PrefetchScalarGridSpec(
        num_scalar_prefetch=1,
        grid=(b, n_pages // PP),
        in_specs=[page_spec(k) for k in range(PP)] + [
            per_row(), per_row(), per_row(),
            pl.BlockSpec((8, rows_pg), lambda bi, j, pt: (0, 0)),
            pl.BlockSpec((8, LANE), lambda bi, j, pt: (0, 0)),
            pl.BlockSpec((4, DA_HD), lambda bi, j, pt: (0, 0))],
        out_specs=per_row(),
        scratch_shapes=[pltpu.VMEM((8, 1), F32), pltpu.VMEM((8, 1), F32), pltpu.VMEM((8, LANE), F32)])
    return pl.pallas_call(
        _da_decode_kernel,
        grid_spec=grid_spec,
        out_shape=jax.ShapeDtypeStruct((b, 8, LANE), F32),
        compiler_params=_cparams(("arbitrary", "arbitrary")),
        name="da_decode",
    )(page_table, *([cache_rows] * PP), q8, k_new, v_new, bl, b0, da_lambda)


def _nsa_decode1_kernel(pt_ref, ab_hbm, q_ref, swa_ref, new_ref, bw_ref, pe_ref, w1c_ref, w2_ref, ovl_ref,
                        oc_ref, ow_ref, idx_ref, abuf, sem, *, n_pages):
    b = pl.program_id(0)
    copies = [pltpu.make_async_copy(ab_hbm.at[pt_ref[b, p]], abuf.at[p], sem) for p in range(n_pages)]
    for c in copies:
        c.start()
    for c in copies:
        c.wait()
    n_chunks = n_pages * (PAGE // CMP_STRIDE)
    kcvc = _cmp_stage2(abuf[...].reshape(n_chunks, 512), pe_ref, w1c_ref, w2_ref)
    scale = NSA_HD ** -0.5
    swa = swa_ref[...].astype(BF16)
    n_win = swa.shape[0]
    score_rows = []
    for g in range(NSA_KV):
        q = q_ref[g]
        s = _mm_nt(q, kcvc[:, g * 256:g * 256 + 128]) * scale
        n = lax.broadcasted_iota(jnp.int32, s.shape, 1)
        p = _masked_softmax(s, n < n_chunks - 1)
        oc_ref[g] = _mm(p, kcvc[:, g * 256 + 128:g * 256 + 256])
        psum = jnp.sum(p[0:NSA_HPG], axis=0, keepdims=True)
        imp = _mm3(jnp.broadcast_to(psum, (8, n_chunks)), ovl_ref[...])
        blk = lax.broadcasted_iota(jnp.int32, imp.shape, 1)
        n_blk = n_chunks * CMP_STRIDE // SEL_BLOCK
        forced = (blk == 0) | (blk == n_blk - 1)
        score_rows.append(jnp.where(blk < n_blk, jnp.where(forced, SEL_BIG, imp), REMOVED)[0:1, :])
        new = new_ref[g:g + 1, :]
        sw = _mm_nt(q, swa) * scale + bw_ref[g][:, 0:n_win]
        s_new = (jnp.sum(q.astype(F32) * new, axis=-1, keepdims=True) * scale
                 + bw_ref[g][:, n_win:n_win + 1])
        m = jnp.maximum(jnp.max(sw, axis=-1, keepdims=True), s_new)
        e = jnp.exp(sw - m)
        e_new = jnp.exp(s_new - m)
        den = jnp.sum(e, axis=-1, keepdims=True) + e_new
        ow_ref[g] = (_mm(e, swa) + e_new * new) / den
    rest = jnp.full((SEL_LANES - NSA_KV, SEL_LANES), REMOVED, F32)
    picks_t = _topk_picks_cols(jnp.concatenate(score_rows + [rest], axis=0).T, N_SEL - 1)
    picks = picks_t.T
    for g in range(NSA_KV):
        idx_ref[g] = jnp.broadcast_to(picks[g:g + 1, :], (8, SEL_LANES)).astype(jnp.int32)


def _nsa_decode1(page_table, ab_pool, qc, cache_swa, kvw_new, bw, pe8, w1c, w2, ovl):
    b, n_pages = page_table.shape
    full = lambda a: pl.BlockSpec(a.shape, lambda bi, pt: (0,) * a.ndim)
    out4 = lambda: pl.BlockSpec((None, NSA_KV, 8, LANE), lambda bi, pt: (bi, 0, 0, 0))
    grid_spec = pltpu.PrefetchScalarGridSpec(
        num_scalar_prefetch=1,
        grid=(b,),
        in_specs=[pl.BlockSpec(memory_space=pl.ANY),
                  pl.BlockSpec((None, NSA_KV, 8, LANE), lambda bi, pt: (bi, 0, 0, 0)),
                  pl.BlockSpec((None, cache_swa.shape[1], LANE), lambda bi, pt: (bi, 0, 0)),
                  pl.BlockSpec((None, NSA_KV, LANE), lambda bi, pt: (bi, 0, 0)),
                  full(bw), full(pe8), full(w1c), full(w2), full(ovl)],
        out_specs=[out4(), out4(), out4()],
        scratch_shapes=[pltpu.VMEM((n_pages, PAGE // CMP_STRIDE, 512), F32), pltpu.SemaphoreType.DMA(())])
    return pl.pallas_call(
        functools.partial(_nsa_decode1_kernel, n_pages=n_pages),
        grid_spec=grid_spec,
        out_shape=[jax.ShapeDtypeStruct((b, NSA_KV, 8, LANE), F32),
                   jax.ShapeDtypeStruct((b, NSA_KV, 8, LANE), F32),
                   jax.ShapeDtypeStruct((b, NSA_KV, 8, LANE), jnp.int32)],
        compiler_params=_cparams(("arbitrary",)),
        name="nsa_decode_cmp_win",
    )(page_table, ab_pool, qc, cache_swa, kvw_new, bw, pe8, w1c, w2, ovl)


def _nsa_decode2_kernel(pt_ref, idx_ref, *refs, n_blk, n_pick):
    blks = refs[:n_pick]
    q_ref, new_ref, bs_ref, o_ref = refs[n_pick:]
    b = pl.program_id(0)
    g = pl.program_id(1)
    scale = NSA_HD ** -0.5
    q = q_ref[...]
    cols = blks[0].shape[0]
    scores = []
    xs = []
    for k in range(n_pick):
        x = blks[k][...].astype(BF16)
        blk = idx_ref[b, g, k]
        bias = (jnp.where(blk == n_blk - 1, bs_ref[:, cols:2 * cols], 0.0)
                + jnp.where(blk == n_blk - 2, bs_ref[:, 2 * cols:3 * cols], 0.0))
        scores.append(_mm_nt(q, x) * scale + bs_ref[:, 0:cols] + bias)
        xs.append(x)
    new = new_ref[...]
    s_new = jnp.sum(q.astype(F32) * new, axis=-1, keepdims=True) * scale + bs_ref[:, 3 * cols:3 * cols + 1]
    s = jnp.concatenate(scores, axis=1)
    m = jnp.maximum(jnp.max(s, axis=-1, keepdims=True), s_new)
    p = jnp.exp(s - m)
    p_new = jnp.exp(s_new - m)
    acc = p_new * new
    for k in range(n_pick):
        acc = acc + _mm(p[:, k * cols:(k + 1) * cols], xs[k])
    o_ref[...] = acc / (jnp.sum(p, axis=-1, keepdims=True) + p_new)


def _nsa_decode2(page_table, idx, cache_rows, qc, nsa_new, bs):
    b = page_table.shape[0]
    n_pick = idx.shape[2]
    rows_blk = SEL_BLOCK * 2 * NSA_KV

    def blk_spec(k):
        def blk_map(bi, g, pt, ix):
            blk = ix[bi, g, k]
            return (pt[bi, lax.shift_right_logical(blk, 1)] * 2 + (blk & 1), 0)
        return pl.BlockSpec((rows_blk, LANE), blk_map)

    grid_spec = pltpu.PrefetchScalarGridSpec(
        num_scalar_prefetch=2,
        grid=(b, NSA_KV),
        in_specs=[blk_spec(k) for k in range(n_pick)] + [
            pl.BlockSpec((None, None, 8, LANE), lambda bi, g, pt, ix: (bi, g, 0, 0)),
            pl.BlockSpec((None, None, 1, LANE), lambda bi, g, pt, ix: (bi, g, 0, 0)),
            pl.BlockSpec((None, 8, bs.shape[2]), lambda bi, g, pt, ix: (g, 0, 0))],
        out_specs=pl.BlockSpec((None, None, 8, LANE), lambda bi, g, pt, ix: (bi, g, 0, 0)))
    return pl.pallas_call(
        functools.partial(_nsa_decode2_kernel, n_blk=page_table.shape[1] * PAGE // SEL_BLOCK, n_pick=n_pick),
        grid_spec=grid_spec,
        out_shape=jax.ShapeDtypeStruct((b, NSA_KV, 8, LANE), F32),
        compiler_params=_cparams(("arbitrary", "arbitrary")),
        name="nsa_decode_sel",
    )(page_table, idx, *([cache_rows] * n_pick), qc, nsa_new, bs)


def _prompt_mixers(pr, b, s, gd_a, gd_b, da_lambda, w1, w2, pe8, w1c):
    sh = lambda a: a.reshape(b, s, a.shape[-1])
    nk = s // TKI
    vt = lambda a: a.reshape(b, nk, a.shape[1], TKI)
    o_a = _da_prompt(sh(pr["qa2"]), sh(pr["dak"]), vt(pr["davt"]), gd_a, da_lambda)
    n_chunks = s // CMP_STRIDE
    ab = _cmp_stage1_rows(pr["nsa"], w1)
    kcvc = _cmp_stage2_prompt(ab.reshape(b, n_chunks, 512), pe8, w1c, w2)
    o_c, sel = _nsa_cmp_prompt(sh(pr["qb"]), kcvc, _ovl_t(n_chunks))
    o_s, o_w = _nsa_sw_prompt(sh(pr["qb2"]), sh(pr["ksel"]), sh(pr["kwin"]), vt(pr["selvt"]), vt(pr["winvt"]),
                              sel, gd_b)
    flat = lambda a: a.reshape(b * s, a.shape[-1])
    return flat(o_a), flat(o_c), flat(o_s), flat(o_w)


def _sample_mixers(pr, page_table, cache_da, cache_nsa, cache_swa, gd_a, gd_b, da_lambda, w1, w2, pe8, w1c):
    b, n_pages = page_table.shape
    past = n_pages * PAGE
    n_pool = cache_da.shape[0]
    da_rows = cache_da.reshape(n_pool, PAGE, DA_HEADS, 2, LANE).transpose(0, 1, 3, 2, 4).reshape(-1, LANE)
    nsa_rows = cache_nsa.reshape(n_pool, PAGE, NSA_KV, 2, LANE).transpose(0, 1, 3, 2, 4).reshape(-1, LANE)
    qa = pr["qa"].reshape(b, DA_HEADS, 2, DA_HD)
    q8 = jnp.zeros((b, DA_HEADS, 2, 2, DA_HD), BF16)
    for c in range(2):
        q8 = q8.at[:, :, c, c, :].set(qa[:, :, c])
    q8 = q8.reshape(b, 8, LANE)
    da_new = pr["da"].reshape(b, DA_HEADS, 2, LANE)
    k_new = jnp.repeat(da_new[:, :, 0], 2, axis=1)
    v_new = jnp.repeat(da_new[:, :, 1], 2, axis=1)
    gda8 = jnp.repeat(gd_a, 2, axis=0)
    bl = jnp.repeat(gda8[:, PAGE - jnp.arange(PAGE)], 2 * DA_HEADS, axis=1)
    b0 = jnp.broadcast_to(gda8[:, 0:1], (8, LANE))
    o_a = _da_decode(page_table, da_rows, q8, k_new, v_new, bl, b0, da_lambda)
    o_a = o_a[:, 0::2, :].reshape(b, 512)
    chunks = PAGE // CMP_STRIDE
    ab_pool = _cmp_stage1_rows(nsa_rows, w1).reshape(n_pool, chunks, 512)
    qb = pr["qb"].reshape(b, NSA_KV, NSA_HPG, NSA_HD)
    qc = jnp.zeros((b, NSA_KV, 8, LANE), BF16).at[:, :, :NSA_HPG, :NSA_HD].set(qb)
    gdb = jnp.pad(gd_b.reshape(NSA_KV, NSA_HPG, -1), ((0, 0), (0, 8 - NSA_HPG), (0, 0)))
    own = jnp.arange(NSA_KV)[:, None, None]
    n_win = cache_swa.shape[1]
    grp_w = jnp.arange(n_win * NSA_KV)[None, None, :] % NSA_KV
    bw = jnp.where(grp_w == own, jnp.repeat(gdb[:, :, n_win - jnp.arange(n_win)], NSA_KV, axis=2), NEG)
    bw = jnp.concatenate([bw, jnp.broadcast_to(gdb[:, :, 0:1], (NSA_KV, 8, LANE))], axis=2)
    o_c, o_w, idx = _nsa_decode1(page_table, ab_pool, qc, cache_swa.reshape(b, n_win * NSA_KV, LANE),
                                 pr["kvw"].reshape(b, NSA_KV, LANE), bw, pe8, w1c, w2,
                                 _ovl_t(past // CMP_STRIDE))
    idx = idx[:, :, 0, :N_SEL - 1]
    rows_blk = SEL_BLOCK * 2 * NSA_KV
    kind = jnp.arange(rows_blk)[None, None, :] % (2 * NSA_KV)
    keep = kind == NSA_KV + own
    bs = jnp.concatenate([
        jnp.where(keep, 0.0, NEG) * jnp.ones((1, 8, 1), F32),
        jnp.repeat(gdb[:, :, SEL_BLOCK - jnp.arange(SEL_BLOCK)], 2 * NSA_KV, axis=2),
        jnp.repeat(gdb[:, :, 2 * SEL_BLOCK - jnp.arange(SEL_BLOCK)], 2 * NSA_KV, axis=2),
        jnp.broadcast_to(gdb[:, :, 0:1], (NSA_KV, 8, LANE))], axis=2)
    sel_new = pr["nsa"].reshape(b, NSA_KV, 2, LANE)[:, :, 1:2, :]
    o_s = _nsa_decode2(page_table, idx, nsa_rows, qc, sel_new, bs)
    o_c = o_c[:, :, :NSA_HPG, :NSA_HD].reshape(b, 512)
    o_w = o_w[:, :, :NSA_HPG, NSA_HD:].reshape(b, 512)
    o_s = o_s[:, :, :NSA_HPG, NSA_HD:].reshape(b, 512)
    return o_a, o_c, o_s, o_w


def kernel(x_prompt, x_sample, cache_da_kv, cache_nsa_kv, cache_swa_kv, page_table, c_prompt, c_sample, rel_bias, w_ada, b_ada, w_in, da_lambda, da_subln, cmp_pe, cmp_w1, cmp_w2, w_br_a, w_br_b, w_out, ln1_g, ln1_b, w_up, w_down, ln2_g, ln2_b):
    bp, s, _ = x_prompt.shape
    bs_ = x_sample.shape[0]
    w_perm, w_vt = _perm_w_in(w_in[0])
    w1, w2 = _cmp_weights(cmp_w1[0], cmp_w2[0])
    pe8 = jnp.broadcast_to(cmp_pe[0].reshape(2, 1, CMP_LEN * NSA_HD), (2, 8, CMP_LEN * NSA_HD))
    w1c = cmp_w1[0].reshape(2, CMP_LEN * NSA_HD, CMP_HID)
    gd_a = _dist_bias(rel_bias[:, :DA_HEADS], 1024)
    gd_b = _dist_bias(rel_bias[:, DA_HEADS:], 1024)
    wa, wb, wo = w_br_a[0].astype(BF16), w_br_b[0].astype(BF16), w_out[0].astype(BF16)
    wu, wd = w_up[0].astype(BF16), w_down[0].astype(BF16)
    sub = da_subln[0].reshape(1, DA_VD)
    eg = _gate_expand()
    g1, b1 = ln1_g[0].reshape(1, D_MODEL), ln1_b[0].reshape(1, D_MODEL)
    g2, b2 = ln2_g[0].reshape(1, D_MODEL), ln2_b[0].reshape(1, D_MODEL)
    lam = da_lambda[0]

    n_c = bp + bs_
    c_all = jnp.pad(jnp.concatenate([c_prompt, c_sample], 0), ((0, (-n_c) % 8), (0, 0)))
    ada = _ada(c_all, w_ada[0], b_ada[0])[:n_c].reshape(n_c, 6, D_MODEL)
    ada_p = jnp.transpose(ada[:bp], (1, 0, 2)).reshape(6, bp, 1, D_MODEL)
    ada_s = jnp.transpose(ada[bp:], (1, 0, 2)).reshape(6, 1, bs_, D_MODEL)

    def tail(mix, pr, x2d, ada_x, tm, tpg):
        o_a, o_c, o_s, o_w = mix
        x1 = _tail1(o_a, o_c, o_s, o_w, pr["gate"], pr["ma"], pr["mb"], x2d, ada_x,
                    wa, wb, wo, sub, eg, g1, b1, tm, tpg)
        tm2 = min(2 * tm, tm * tpg)
        return _tail2(x1, ada_x, wu, wd, g2, b2, tm2, tm * tpg // tm2)

    xp = x_prompt.reshape(bp * s, D_MODEL)
    pr_p = _proj(xp, ada_p, w_perm, w_vt, TKI, s // TKI)
    mix_p = _prompt_mixers(pr_p, bp, s, gd_a, gd_b, lam, w1, w2, pe8, w1c)
    tm_t = 512
    y_p = tail(mix_p, pr_p, xp, ada_p, tm_t, s // tm_t).reshape(bp, s, D_MODEL)
    xs = x_sample.reshape(bs_, D_MODEL)
    pr_s = _proj(xs, ada_s, w_perm, None, bs_, 1)
    mix_s = _sample_mixers(pr_s, page_table, cache_da_kv[0], cache_nsa_kv[0], cache_swa_kv[0],
                           gd_a, gd_b, lam, w1, w2, pe8, w1c)
    y_s = tail(mix_s, pr_s, xs, ada_s, bs_, 1).reshape(bs_, 1, D_MODEL)

    win = min(WINDOW, s)
    from_rows = lambda a, units: a.reshape(bp, s, 2, units, LANE).transpose(0, 1, 3, 2, 4).reshape(1, bp, s, units, 256)
    new_da_p = from_rows(pr_p["da"], DA_HEADS)
    new_nsa_p = from_rows(pr_p["nsa"], NSA_KV)
    new_swa_p = pr_p["kvw"].reshape(bp, s, NSA_KV, 2 * NSA_HD)[None, :, s - win:]
    new_da_s = pr_s["da"].reshape(1, bs_, 1, DA_HEADS, 4 * DA_HD)
    new_nsa_s = pr_s["nsa"].reshape(1, bs_, 1, NSA_KV, 4 * NSA_HD)
    new_swa_s = jnp.concatenate([cache_swa_kv[0][:, 1:], pr_s["kvw"].reshape(bs_, 1, NSA_KV, 2 * NSA_HD)],
                                axis=1)[None]
    return (y_p, y_s, new_da_p, new_nsa_p, new_swa_p, new_da_s, new_nsa_s, new_swa_s)
```

```python
import functools
import math

import numpy as np
import jax
import jax.numpy as jnp
from jax import lax
from jax.experimental import pallas as pl
from jax.experimental.pallas import tpu as pltpu

F32 = jnp.float32
BF16 = jnp.bfloat16

D_MODEL = 1024
PAGE = 128
DA_HEADS = 4
DA_HD = 64
DA_VD = 128
NSA_HEADS = 8
NSA_KV = 2
NSA_HPG = 4
NSA_HD = 64
CMP_STRIDE = 16
CMP_LEN = 32
CMP_HID = 64
SEL_BLOCK = 64
N_SEL = 16
WINDOW = 512
D_FF = 4096
N_BUCKETS = 32
MAX_DIST = 128
DEPTH = 1
ALPHA = (2 * DEPTH) ** 0.25
LN_EPS = 1e-5
RMS_EPS = 1e-5
NEG = -1e30
SEL_BIG = 1e9
LAM_INIT = 0.8 - 0.6 * math.exp(-0.3 * 0)
SPLIT_SIZES = (512, 512, 512, 512, 256, 256, 256, 24, 1024, 1024)

LANE = 128
VMEM_LIMIT = 56 * 1024 * 1024
TQ = 128
TKI = 256
TQS = 256
ONES_ROWS = 16
TDA = 512
SEL_LANES = 128
REMOVED = -3e38
PP = 16


def _cparams(sem):
    return pltpu.CompilerParams(dimension_semantics=sem, vmem_limit_bytes=VMEM_LIMIT)


def _mm(a, b):
    return jnp.dot(a.astype(BF16), b.astype(BF16), preferred_element_type=F32)


def _mm_nt(a, b):
    return lax.dot_general(a.astype(BF16), b.astype(BF16), (((1,), (1,)), ((), ())),
                           preferred_element_type=F32)


def _mm3(x, w):
    hi = x.astype(BF16)
    r = x - hi.astype(F32)
    mid = r.astype(BF16)
    lo = (r - mid.astype(F32)).astype(BF16)
    return (jnp.dot(hi, w, preferred_element_type=F32) + jnp.dot(mid, w, preferred_element_type=F32)
            + jnp.dot(lo, w, preferred_element_type=F32))


def _masked_softmax(s, valid):
    l = jnp.where(valid, s, NEG)
    m = jnp.max(l, axis=-1, keepdims=True)
    e = jnp.where(valid, jnp.exp(l - m), 0.0)
    return e / jnp.maximum(jnp.sum(e, axis=-1, keepdims=True), 1e-30)


def _t5_bucket(dist):
    n = jnp.maximum(dist, 0)
    max_exact = N_BUCKETS // 2
    nf = jnp.maximum(n, 1).astype(F32)
    large = max_exact + (jnp.log(nf / max_exact) / math.log(MAX_DIST / max_exact)
                         * (N_BUCKETS - max_exact)).astype(jnp.int32)
    return jnp.where(n < max_exact, n, jnp.minimum(large, N_BUCKETS - 1))


def _dist_bias(tbl, n):
    d = jnp.arange(n, dtype=jnp.int32)
    g = tbl[_t5_bucket(d)] - tbl[N_BUCKETS - 1][None, :]
    return jnp.transpose(g)


def _toeplitz(gd, offset, rows, cols, below=0.0):
    heads, n = gd.shape
    length = rows + cols - 1
    assert offset + rows <= n
    lo = offset - cols + 1
    hvec = gd[:, max(lo, 0):offset + rows]
    if lo < 0:
        hvec = jnp.concatenate([jnp.full((heads, -lo), below, gd.dtype), hvec], axis=1)
    rev = jnp.concatenate([hvec[:, ::-1], jnp.zeros((heads, 1), gd.dtype)], axis=1)
    flat = jnp.tile(rev, (1, rows))[:, :rows * length].reshape(heads, rows, length)
    return flat[:, :, rows - 1:rows - 1 + cols]


def _diff_lambda(lam_ref):
    l = lam_ref[...]
    a = jnp.sum(l[0:1, :] * l[1:2, :], axis=-1, keepdims=True)
    b = jnp.sum(l[2:3, :] * l[3:4, :], axis=-1, keepdims=True)
    return jnp.exp(a) - jnp.exp(b) + LAM_INIT


def _ada_kernel(c_ref, w_ref, b_ref, o_ref):
    c = c_ref[...]
    o_ref[...] = _mm(c * jax.nn.sigmoid(c), w_ref[...]) + b_ref[...]


def _ada(c, w_ada, b_ada):
    m = c.shape[0]
    n = w_ada.shape[1]
    tn = 512
    return pl.pallas_call(
        _ada_kernel,
        grid=(n // tn,),
        in_specs=[pl.BlockSpec((m, D_MODEL), lambda j: (0, 0)),
                  pl.BlockSpec((D_MODEL, tn), lambda j: (0, j)),
                  pl.BlockSpec((1, tn), lambda j: (0, j))],
        out_specs=pl.BlockSpec((m, tn), lambda j: (0, j)),
        out_shape=jax.ShapeDtypeStruct((m, n), F32),
        compiler_params=_cparams(("arbitrary",)),
        name="ada",
    )(c, w_ada, b_ada.reshape(1, n))


PROJ_GROUPS = (("qa", 512, BF16), ("da", 1024, F32), ("qb", 512, BF16), ("nsa", 512, F32), ("kvw", 256, F32),
               ("gate", 128, F32), ("ma", 1024, F32), ("mb", 1024, F32))
PROJ_KEY_GROUPS = (("ksel", 256), ("kwin", 256))
STATE_ROWS = {"da": DA_HEADS, "nsa": NSA_KV}
LOG2E = 1.4426950408889634
Q_SCALE = {"qa": DA_HD ** -0.5 * LOG2E, "qb": NSA_HD ** -0.5 * LOG2E}
PROJ_W = sum(w for _, w, _ in PROJ_GROUPS)
PROJ_W_KV = PROJ_W + sum(w for _, w in PROJ_KEY_GROUPS)
PROJ_VT = (("davt", DA_HEADS * DA_VD), ("selvt", NSA_KV * NSA_HD), ("winvt", NSA_KV * NSA_HD))


def _perm_w_in(w_in):
    parts = jnp.split(w_in, np.cumsum(SPLIT_SIZES)[:-1].tolist(), axis=1)
    qa, ka, va, qn, kvc, kvs, kvw, gb, ma, mb = parts
    da = jnp.concatenate([jnp.concatenate([ka[:, h * 128:(h + 1) * 128], va[:, h * 128:(h + 1) * 128]], 1)
                          for h in range(DA_HEADS)], 1)
    nsa = jnp.concatenate([jnp.concatenate([kvc[:, g * 128:(g + 1) * 128], kvs[:, g * 128:(g + 1) * 128]], 1)
                           for g in range(NSA_KV)], 1)
    gate = jnp.pad(gb, ((0, 0), (0, LANE - gb.shape[1])))
    keys2 = lambda kv: jnp.concatenate([kv[:, g * 128:g * 128 + 64] for g in range(NSA_KV) for _ in range(2)], 1)
    vals = lambda kv: jnp.concatenate([kv[:, g * 128 + 64:(g + 1) * 128] for g in range(NSA_KV)], 1)
    cols = dict(qa=qa, da=da, qb=qn, nsa=nsa, kvw=kvw, gate=gate, ma=ma, mb=mb, ksel=keys2(kvs), kwin=keys2(kvw))
    names = [n for n, _, _ in PROJ_GROUPS] + [n for n, _ in PROJ_KEY_GROUPS]
    w_perm = jnp.concatenate([cols[n] for n in names], 1).astype(BF16)
    w_vt = jnp.transpose(jnp.concatenate([va, vals(kvs), vals(kvw)], 1)).astype(BF16)
    return w_perm, w_vt


def _proj_kernel(x_ref, ada_ref, w_ref, *refs, with_kv):
    h = (x_ref[...] * (1.0 + ada_ref[1]) + ada_ref[0]).astype(BF16)
    o_refs = refs[1:] if with_kv else refs
    off = 0
    k = 0
    for name, width, dt in PROJ_GROUPS:
        acc = jnp.dot(h, w_ref[:, off:off + width], preferred_element_type=F32)
        if with_kv and name in STATE_ROWS:
            units = STATE_ROWS[name]
            tm = acc.shape[0]
            for u in range(units):
                for half in range(2):
                    o_refs[k][pl.ds(half * units + u, tm, stride=2 * units), :] = (
                        acc[:, (u * 2 + half) * LANE:(u * 2 + half + 1) * LANE])
        else:
            o_refs[k][...] = acc.astype(dt)
        k += 1
        if with_kv and name in Q_SCALE:
            o_refs[k][...] = (acc * Q_SCALE[name]).astype(BF16)
            k += 1
        if with_kv and name == "da":
            o_refs[k][...] = jnp.concatenate([acc[:, hd * 256:hd * 256 + 128] for hd in range(DA_HEADS)],
                                             axis=1).astype(BF16)
            k += 1
        off += width
    if with_kv:
        for _, width in PROJ_KEY_GROUPS:
            o_refs[k][...] = jnp.dot(h, w_ref[:, off:off + width], preferred_element_type=F32).astype(BF16)
            k += 1
            off += width
        vt = _mm_nt(refs[0][...], h)
        r0 = 0
        for _, rows in PROJ_VT:
            o_refs[k][...] = vt[r0:r0 + rows].astype(BF16)
            k += 1
            r0 += rows


def _ada_spec(ada, tm, tiles_per_group):
    r = ada.shape[2]
    return pl.BlockSpec((6, None, r, D_MODEL), lambda i, *_: (0, i // tiles_per_group, 0, 0))


def _proj(x2d, ada, w_perm, w_vt, tm, tiles_per_group):
    m = x2d.shape[0]
    with_kv = w_vt is not None
    outs = []
    for name, w, dt in PROJ_GROUPS:
        if with_kv and name in STATE_ROWS:
            rows_pos = w // LANE
            outs.append((name, (m * rows_pos, LANE), (tm * rows_pos, LANE), dt))
        else:
            outs.append((name, (m, w), (tm, w), dt))
        if with_kv and name in Q_SCALE:
            outs.append((name + "2", (m, w), (tm, w), BF16))
        if with_kv and name == "da":
            outs.append(("dak", (m, 512), (tm, 512), BF16))
    in_specs = [pl.BlockSpec((tm, D_MODEL), lambda i: (i, 0)),
                _ada_spec(ada, tm, tiles_per_group),
                pl.BlockSpec((D_MODEL, PROJ_W_KV if with_kv else PROJ_W), lambda i: (0, 0))]
    args = [x2d, ada, w_perm]
    if with_kv:
        assert tm == TKI
        outs += [(n, (m, w), (tm, w), BF16) for n, w in PROJ_KEY_GROUPS]
        outs += [(n, (m // tm, r, tm), (None, r, tm), BF16) for n, r in PROJ_VT]
        in_specs.append(pl.BlockSpec(w_vt.shape, lambda i: (0, 0)))
        args.append(w_vt)
    res = pl.pallas_call(
        functools.partial(_proj_kernel, with_kv=with_kv),
        grid=(m // tm,),
        in_specs=in_specs,
        out_specs=[pl.BlockSpec(blk, (lambda i: (i, 0)) if len(blk) == 2 else (lambda i: (i, 0, 0)))
                   for _, _, blk, _ in outs],
        out_shape=[jax.ShapeDtypeStruct(shape, dt) for _, shape, _, dt in outs],
        compiler_params=_cparams(("arbitrary",)),
        name="proj",
    )(*args)
    return {n: o for (n, _, _, _), o in zip(outs, res)}


class _Chain:
    def __init__(self, s_bufs, p_bufs, al_bufs, m_ref, acc_ref, score_fn, vext_fn, bias_fn):
        self.s, self.p, self.al = s_bufs, p_bufs, al_bufs
        self.m, self.acc = m_ref, acc_ref
        self.score_fn, self.vext_fn, self.bias_fn = score_fn, vext_fn, bias_fn
        self.maps = m_ref.shape[0]
        self.cur = 0

    def prime(self, j):
        self.cur = 0
        self.m[...] = jnp.full(self.m.shape, NEG, F32)
        self.acc[...] = jnp.zeros(self.acc.shape, F32)
        self.p[1][...] = jnp.zeros(self.p[1].shape, BF16)
        self.al[1][...] = jnp.ones(self.al[1].shape, F32)
        self.score_fn(j, self.s[0])

    def _finish(self, k, buf, pv):
        self.acc[k] = self.acc[k] * self.al[buf][k, 0:1, :] + pv

    def step(self, j_prev, j_next, table=None, valid=None, pen=None, prev_valid=None):
        c, o = self.cur, 1 - self.cur
        vext = self.vext_fn(j_prev)
        if prev_valid is not None:
            vext = jnp.where(prev_valid, vext, jnp.zeros_like(vext))
        pv = [jnp.dot(vext, self.p[o][k], preferred_element_type=F32) for k in range(self.maps)]
        if j_next is not None:
            self.score_fn(j_next, self.s[o])
        for k in range(self.maps):
            for l0 in range(0, self.m.shape[2], LANE):
                cols = slice(l0, l0 + LANE)
                s = self.s[c][k, :, cols]
                if table is not None:
                    s = s + self.bias_fn(table, l0)
                if pen is not None:
                    s = s + pen
                m_cur = jnp.max(s, axis=0, keepdims=True)
                if valid is not None:
                    m_cur = jnp.where(valid, m_cur, NEG)
                m_old = self.m[k, 0:1, cols]
                m_new = jnp.maximum(m_old, m_cur)
                self.p[c][k, :, cols] = jnp.exp2(s - m_new).astype(BF16)
                self.al[c][k, :, cols] = jnp.broadcast_to(jnp.exp2(m_old - m_new), (8, LANE))
                self.m[k, :, cols] = jnp.broadcast_to(m_new, (8, LANE))
            self._finish(k, o, pv[k])
        self.cur = o

    def step_eager(self, j_cur, j_next, table=None, valid=None, pen=None):
        c, o = self.cur, 1 - self.cur
        if j_next is not None:
            self.score_fn(j_next, self.s[o])
        vext = self.vext_fn(j_cur)
        if valid is not None:
            vext = jnp.where(valid, vext, jnp.zeros_like(vext))
        for k in range(self.maps):
            ps, alphas = [], []
            for l0 in range(0, self.m.shape[2], LANE):
                cols = slice(l0, l0 + LANE)
                s = self.s[c][k, :, cols]
                if table is not None:
                    s = s + self.bias_fn(table, l0)
                if pen is not None:
                    s = s + pen
                m_cur = jnp.max(s, axis=0, keepdims=True)
                if valid is not None:
                    m_cur = jnp.where(valid, m_cur, NEG)
                m_old = self.m[k, 0:1, cols]
                m_new = jnp.maximum(m_old, m_cur)
                ps.append(jnp.exp2(s - m_new).astype(BF16))
                alphas.append(jnp.exp2(m_old - m_new))
                self.m[k, :, cols] = jnp.broadcast_to(m_new, (8, LANE))
            self.acc[k] = (self.acc[k] * jnp.concatenate(alphas, axis=1)
                           + jnp.dot(vext, jnp.concatenate(ps, axis=1), preferred_element_type=F32))
        self.cur = o

    def flush(self, j_prev):
        o = 1 - self.cur
        vext = self.vext_fn(j_prev)
        for k in range(self.maps):
            self._finish(k, o, jnp.dot(vext, self.p[o][k], preferred_element_type=F32))


def _chain_scratch(maps, acc_rows, queries):
    return [pltpu.VMEM((maps, TKI, queries), F32), pltpu.VMEM((maps, TKI, queries), F32),
            pltpu.VMEM((maps, TKI, queries), BF16), pltpu.VMEM((maps, TKI, queries), BF16),
            pltpu.VMEM((maps, 8, queries), F32), pltpu.VMEM((maps, 8, queries), F32),
            pltpu.VMEM((maps, 8, queries), F32), pltpu.VMEM((maps, acc_rows, queries), F32)]


def _key_rows(ref, j):
    return pl.ds(pl.multiple_of(j * TKI, TKI), TKI)


def _da_kernel(q_ref, k_ref, vt_ref, tb_ref, lam_ref, o_ref, s_a, s_b, p_a, p_b, al_a, al_b, m_ref, acc_ref):
    qi = pl.program_id(2)
    q = q_ref[...]
    lane = lax.broadcasted_iota(jnp.int32, q.shape, 1)
    zero = jnp.zeros_like(q)
    qt_maps = tuple(jnp.where(keep, q, zero).astype(F32).T.astype(BF16) for keep in (lane < DA_HD, lane >= DA_HD))
    ones = jnp.ones((ONES_ROWS, TKI), BF16)

    def scores(j, dst):
        kk = k_ref[_key_rows(k_ref, j), :]
        for c in range(2):
            dst[c] = jnp.dot(kk, qt_maps[c], preferred_element_type=F32)

    chain = _Chain((s_a, s_b), (p_a, p_b), (al_a, al_b), m_ref, acc_ref, scores,
                   lambda j: jnp.concatenate([vt_ref[j], ones], axis=0),
                   lambda table, l0: tb_ref[table, :, l0:l0 + LANE])
    n_far = jnp.maximum(2 * qi - 1, 0)
    j_d = 2 * qi
    j_s = jnp.maximum(2 * qi - 1, 0)
    chain.prime(jnp.where(n_far > 0, 0, j_d))

    def pair(i, carry):
        a = 2 * i
        b = jnp.minimum(a + 1, n_far - 1)
        chain.step_eager(a, b)
        chain.step_eager(b, jnp.where(a + 2 < n_far, a + 2, j_d), valid=a + 1 < n_far)
        return carry

    lax.fori_loop(0, lax.shift_right_logical(n_far + 1, 1), pair, 0)
    chain.step_eager(j_d, j_s, table=1)
    chain.step_eager(j_s, j_d + 1, table=0, pen=jnp.where(qi > 0, 0.0, NEG))
    chain.step_eager(j_d + 1, None, table=2)
    lam = _diff_lambda(lam_ref)
    o_t = (acc_ref[0, 0:DA_VD, :] / acc_ref[0, DA_VD:DA_VD + 1, :]
           - lam * (acc_ref[1, 0:DA_VD, :] / acc_ref[1, DA_VD:DA_VD + 1, :]))
    o_ref[...] = o_t.T


def _da_prompt(qa, dak, davt, gd_a, da_lambda):
    b, s, _ = qa.shape
    tb = jnp.stack([_toeplitz(gd_a, TKI, TDA, TKI), _toeplitz(gd_a, 0, TDA, TKI, NEG),
                    _toeplitz(gd_a, -TKI, TDA, TKI, NEG)], axis=1)
    tb = jnp.swapaxes(tb, 2, 3) * LOG2E
    acc_rows = DA_VD + ONES_ROWS
    return pl.pallas_call(
        _da_kernel,
        grid=(b, DA_HEADS, s // TDA),
        in_specs=[pl.BlockSpec((None, TDA, 128), lambda bi, h, qi: (bi, qi, h)),
                  pl.BlockSpec((None, s, 128), lambda bi, h, qi: (bi, 0, h)),
                  pl.BlockSpec((None, s // TKI, DA_VD, TKI), lambda bi, h, qi: (bi, 0, h, 0)),
                  pl.BlockSpec((None, 3, TKI, TDA), lambda bi, h, qi: (h, 0, 0, 0)),
                  pl.BlockSpec((4, DA_HD), lambda bi, h, qi: (0, 0))],
        out_specs=pl.BlockSpec((None, TDA, 128), lambda bi, h, qi: (bi, qi, h)),
        out_shape=jax.ShapeDtypeStruct((b, s, DA_HEADS * DA_VD), F32),
        scratch_shapes=_chain_scratch(2, acc_rows, TDA),
        compiler_params=_cparams(("arbitrary", "arbitrary", "arbitrary")),
        name="da_prompt",
    )(qa, dak, davt, tb, da_lambda)


def _cmp_weights(cmp_w1, cmp_w2):
    w1 = jnp.zeros((CMP_STRIDE, 2, NSA_HD, 2, 2, CMP_HID), F32)
    for half in range(2):
        for c in range(2):
            blk = cmp_w1[c, half * CMP_STRIDE:(half + 1) * CMP_STRIDE]
            w1 = w1.at[:, c, :, half, c, :].set(blk)
    w1 = w1.reshape(CMP_STRIDE * 2 * NSA_HD, 2 * 2 * CMP_HID).astype(BF16)
    w2 = jnp.zeros((2, CMP_HID, 2, 2, NSA_HD), F32)
    for c in range(2):
        for rep in range(2):
            w2 = w2.at[c, :, c, rep, :].set(cmp_w2[c])
    w2 = w2.reshape(2 * CMP_HID, 2 * 2 * NSA_HD).astype(BF16)
    return w1, w2


def _cmp1_rows_kernel(x_ref, w_ref, o_ref, *, tm):
    rows_pos = 2 * NSA_KV
    for g in range(NSA_KV):
        xg = jnp.concatenate([x_ref[pl.ds(l * rows_pos + g, tm, stride=CMP_STRIDE * rows_pos), :]
                              for l in range(CMP_STRIDE)], axis=1)
        o_ref[:, g * 256:(g + 1) * 256] = _mm(xg, w_ref[...])


def _cmp1_pages_kernel(pt_ref, *refs, n_pg):
    pages, w_ref, o_ref = refs[:n_pg], refs[n_pg], refs[n_pg + 1]
    rows_pos = 2 * NSA_KV
    chunks = PAGE // CMP_STRIDE
    for g in range(NSA_KV):
        xg = jnp.concatenate(
            [jnp.concatenate([pages[k][pl.ds(l * rows_pos + g, chunks, stride=CMP_STRIDE * rows_pos), :]
                              for k in range(n_pg)], axis=0) for l in range(CMP_STRIDE)], axis=1)
        o_ref[:, g * 256:(g + 1) * 256] = _mm(xg, w_ref[...])


def _cmp_stage1_pages(page_table, rows2d, w1):
    b, n_pages = page_table.shape
    rows_page = PAGE * 2 * NSA_KV
    chunks = PAGE // CMP_STRIDE
    n_pg = math.gcd(b * n_pages, PP)
    page_spec = lambda k: pl.BlockSpec((rows_page, LANE), lambda i, pt: (pt[i * n_pg + k], 0))
    grid_spec = pltpu.PrefetchScalarGridSpec(
        num_scalar_prefetch=1,
        grid=(b * n_pages // n_pg,),
        in_specs=[page_spec(k) for k in range(n_pg)] + [pl.BlockSpec(w1.shape, lambda i, pt: (0, 0))],
        out_specs=pl.BlockSpec((n_pg * chunks, 512), lambda i, pt: (i, 0)))
    return pl.pallas_call(
        functools.partial(_cmp1_pages_kernel, n_pg=n_pg),
        grid_spec=grid_spec,
        out_shape=jax.ShapeDtypeStruct((b * n_pages * chunks, 512), F32),
        compiler_params=_cparams(("arbitrary",)),
        name="cmp_stage1_pages",
    )(page_table.reshape(-1), *([rows2d] * n_pg), w1)


def _cmp_stage1_rows(rows2d, w1):
    rows_chunk = CMP_STRIDE * 2 * NSA_KV
    r = rows2d.shape[0] // rows_chunk
    tm = math.gcd(r, 128)
    return pl.pallas_call(
        functools.partial(_cmp1_rows_kernel, tm=tm),
        grid=(r // tm,),
        in_specs=[pl.BlockSpec((tm * rows_chunk, LANE), lambda i: (i, 0)),
                  pl.BlockSpec(w1.shape, lambda i: (0, 0))],
        out_specs=pl.BlockSpec((tm, 512), lambda i: (i, 0)),
        out_shape=jax.ShapeDtypeStruct((r, 512), F32),
        compiler_params=_cparams(("arbitrary",)),
        name="cmp_stage1_rows",
    )(rows2d, w1)


def _cmp_stage2(ab, pe_ref, w1c_ref, w2_ref):
    n = ab.shape[0]
    cst = jnp.concatenate([_mm(pe_ref[c], w1c_ref[c])[0:1, :] for c in range(2)], axis=1)
    outs = []
    for g in range(NSA_KV):
        a = ab[:, g * 256:g * 256 + 128]
        bn = pltpu.roll(ab[:, g * 256 + 128:g * 256 + 256], n - 1, 0)
        hid = jax.nn.gelu(a + bn + cst)
        outs.append(_mm(hid, w2_ref[...]))
    return jnp.concatenate(outs, axis=1)


def _cmp2_kernel(ab_ref, pe_ref, w1c_ref, w2_ref, o_ref):
    o_ref[...] = _cmp_stage2(ab_ref[...], pe_ref, w1c_ref, w2_ref).astype(BF16)


def _cmp_stage2_prompt(ab, pe8, w1c, w2):
    b, n, _ = ab.shape
    return pl.pallas_call(
        _cmp2_kernel,
        grid=(b,),
        in_specs=[pl.BlockSpec((None, n, 512), lambda i: (i, 0, 0)),
                  pl.BlockSpec(pe8.shape, lambda i: (0, 0, 0)),
                  pl.BlockSpec(w1c.shape, lambda i: (0, 0, 0)),
                  pl.BlockSpec(w2.shape, lambda i: (0, 0))],
        out_specs=pl.BlockSpec((None, n, 512), lambda i: (i, 0, 0)),
        out_shape=jax.ShapeDtypeStruct((b, n, 512), BF16),
        compiler_params=_cparams(("arbitrary",)),
        name="cmp_stage2",
    )(ab, pe8, w1c, w2)


def _ovl_t(n_chunks):
    n = np.arange(n_chunks)[:, None]
    j = np.arange(SEL_LANES)[None, :]
    ovl = (n * CMP_STRIDE < j * SEL_BLOCK + SEL_BLOCK) & (j * SEL_BLOCK < n * CMP_STRIDE + CMP_LEN)
    ovl &= n < n_chunks - 1
    return jnp.asarray(ovl.astype(np.float32)).astype(BF16)


def _topk_picks_cols(score_t, n_pick):
    blk = lax.broadcasted_iota(jnp.int32, score_t.shape, 0).astype(F32)
    picks = jnp.zeros(score_t.shape, F32)
    sc = score_t
    for it in range(n_pick):
        m = jnp.max(sc, axis=0, keepdims=True)
        first = jnp.min(jnp.where(sc == m, blk, float(SEL_LANES)), axis=0, keepdims=True)
        picks = jnp.where(blk == float(it), first, picks)
        sc = jnp.where(blk == first, REMOVED, sc)
    return picks


def _topk_mask_cols(score_t, n_pick):
    blk = lax.broadcasted_iota(jnp.int32, score_t.shape, 0).astype(F32)
    sel = jnp.zeros(score_t.shape, F32)
    sc = score_t
    for _ in range(n_pick):
        m = jnp.max(sc, axis=0, keepdims=True)
        first = jnp.min(jnp.where(sc == m, blk, float(SEL_LANES)), axis=0, keepdims=True)
        hit = blk == first
        sel = jnp.where(hit, 1.0, sel)
        sc = jnp.where(hit, REMOVED, sc)
    return sel


def _stack_heads(q):
    lane = lax.broadcasted_iota(jnp.int32, (q.shape[0], LANE), 1)
    zero = jnp.zeros((q.shape[0], LANE), q.dtype)
    parts = []
    for hp in range(NSA_HPG):
        blk = q[:, (hp // 2) * LANE:(hp // 2 + 1) * LANE]
        keep = (lane < NSA_HD) if hp % 2 == 0 else (lane >= NSA_HD)
        parts.append(jnp.where(keep, blk, zero))
    return jnp.concatenate(parts, axis=0)


def _unstack_heads(o, tq):
    lane = lax.broadcasted_iota(jnp.int32, (tq, LANE), 1)
    pairs = [jnp.where(lane < NSA_HD, o[(2 * m) * tq:(2 * m + 1) * tq], o[(2 * m + 1) * tq:(2 * m + 2) * tq])
             for m in range(2)]
    return jnp.concatenate(pairs, axis=1)


def _nsa_cmp_kernel(q_ref, kcvc_ref, ovl_ref, oc_ref, sel_ref, *, n_cmp, n_slc):
    qi = pl.program_id(2)
    n_chunks = kcvc_ref.shape[0]
    qs = _stack_heads(q_ref[...])
    kc2 = kcvc_ref[:, 0:128]
    vc2 = kcvc_ref[:, 128:256]
    s = _mm_nt(qs, kc2) * (NSA_HD ** -0.5)
    rows = NSA_HPG * TQ
    t = qi * TQ + (lax.broadcasted_iota(jnp.int32, (rows, n_chunks), 0) & (TQ - 1))
    n = lax.broadcasted_iota(jnp.int32, (rows, n_chunks), 1)
    p = _masked_softmax(s, (n * CMP_STRIDE + (CMP_LEN - 1) <= t) & (n < n_cmp))
    oc_ref[...] = _unstack_heads(_mm(p, vc2), TQ)
    psum = p[0:TQ] + p[TQ:2 * TQ] + p[2 * TQ:3 * TQ] + p[3 * TQ:4 * TQ]
    imp = _mm3(psum, ovl_ref[...])
    tq = qi * TQ + lax.broadcasted_iota(jnp.int32, (TQ, SEL_LANES), 0)
    blk = lax.broadcasted_iota(jnp.int32, (TQ, SEL_LANES), 1)
    cur = lax.shift_right_logical(tq, 6)
    forced = (blk == 0) | (blk == cur) | (blk == cur - 1)
    score = jnp.where(forced, SEL_BIG, jnp.where(blk * SEL_BLOCK <= tq, imp, -SEL_BIG))
    score = jnp.where(blk < n_slc, score, REMOVED)
    sel_t = _topk_mask_cols(score.T, min(N_SEL, n_slc))
    sel_ref[...] = jnp.where(sel_t > 0.5, 0.0, NEG).astype(BF16)


def _nsa_cmp_prompt(qb, kcvc, ovl):
    b, s, _ = qb.shape
    n_chunks = kcvc.shape[1]
    kern = functools.partial(_nsa_cmp_kernel, n_cmp=n_chunks - 1, n_slc=s // SEL_BLOCK)
    return pl.pallas_call(
        kern,
        grid=(b, NSA_KV, s // TQ),
        in_specs=[pl.BlockSpec((None, TQ, 256), lambda bi, g, qi: (bi, qi, g)),
                  pl.BlockSpec((None, n_chunks, 256), lambda bi, g, qi: (bi, 0, g)),
                  pl.BlockSpec(ovl.shape, lambda bi, g, qi: (0, 0))],
        out_specs=[pl.BlockSpec((None, TQ, 256), lambda bi, g, qi: (bi, qi, g)),
                   pl.BlockSpec((None, None, TQ, SEL_LANES), lambda bi, g, qi: (bi, g, qi, 0))],
        out_shape=[jax.ShapeDtypeStruct((b, s, 512), F32),
                   jax.ShapeDtypeStruct((b, NSA_KV, s, SEL_LANES), BF16)],
        compiler_params=_cparams(("arbitrary", "arbitrary", "arbitrary")),
        name="nsa_cmp",
    )(qb, kcvc, ovl)


def _nsa_sw_kernel(q_ref, ks_ref, kw_ref, vs_ref, vw_ref, sel_ref, e_ref, tb_ref, os_ref, ow_ref, *scratch):
    m = pl.program_id(2)
    qs = _stack_heads(q_ref[...])
    qs_t = qs.astype(F32).T.astype(BF16)
    sel_t = jnp.concatenate([sel_ref[t * TQ:(t + 1) * TQ, :] for t in range(TQS // TQ)] * NSA_HPG, axis=1)
    qs_sel_t = jnp.concatenate([qs_t, sel_t], axis=0)
    ones = jnp.ones((ONES_ROWS, TKI), BF16)

    def sel_scores(j, dst):
        keys = jnp.concatenate([ks_ref[_key_rows(ks_ref, j), :], e_ref[j]], axis=1)
        dst[0] = jnp.dot(keys, qs_sel_t, preferred_element_type=F32)

    def win_scores(j, dst):
        dst[0] = jnp.dot(kw_ref[_key_rows(kw_ref, j), :], qs_t, preferred_element_type=F32)

    bias = lambda table, l0: tb_ref[table, :, l0:l0 + LANE]
    sel = _Chain(scratch[0:2], scratch[2:4], scratch[4:6], scratch[6], scratch[7], sel_scores,
                 lambda j: jnp.concatenate([vs_ref[j], ones], axis=0), bias)
    win = _Chain(scratch[8:10], scratch[10:12], scratch[12:14], scratch[14], scratch[15], win_scores,
                 lambda j: jnp.concatenate([vw_ref[j], ones], axis=0), bias)

    def heads_out(acc_ref):
        o_t = acc_ref[0, 0:NSA_HD, :] / acc_ref[0, NSA_HD:NSA_HD + 1, :]
        pairs = [jnp.concatenate([o_t[:, (2 * k) * TQS:(2 * k + 1) * TQS],
                                  o_t[:, (2 * k + 1) * TQS:(2 * k + 2) * TQS]], axis=0).T
                 for k in range(NSA_HPG // 2)]
        return jnp.concatenate(pairs, axis=1)

    n_far = jnp.maximum(m - 1, 0)
    j1 = jnp.maximum(m - 1, 0)
    j2 = jnp.maximum(m - 2, 0)
    pen1 = jnp.where(m >= 1, 0.0, NEG)
    pen2 = jnp.where(m >= 2, 0.0, NEG)
    sel.prime(jnp.where(n_far > 0, 0, m))
    win.prime(m)

    def pair(i, carry):
        a = 2 * i
        sel.step(jnp.maximum(a - 1, 0), jnp.minimum(a + 1, n_far - 1))
        sel.step(a, jnp.where(a + 2 < n_far, a + 2, m), valid=a + 1 < n_far)
        return carry

    lax.fori_loop(0, lax.shift_right_logical(n_far + 1, 1), pair, 0)
    sel.step(jnp.maximum(n_far - 1, 0), j1, table=0, prev_valid=(n_far & 1) == 0)
    win.step(m, j1, table=0)
    sel.step(m, None, table=1, pen=pen1)
    win.step(m, j2, table=1, pen=pen1)
    sel.flush(j1)
    win.step(j1, None, table=2, pen=pen2)
    win.flush(j2)
    os_ref[...] = heads_out(sel.acc)
    ow_ref[...] = heads_out(win.acc)


def _nsa_sw_prompt(qb, ksel, kwin, selvt, winvt, sel, gd_b):
    b, s, _ = qb.shape
    assert TQS == TKI and WINDOW == 2 * TKI
    cols = NSA_HPG * TQS
    i = np.arange(TQS)[:, None]
    j = np.arange(TKI)[None, :]
    edge = jnp.asarray(np.broadcast_to(np.where(j >= i, 0.0, NEG).astype(np.float32), (NSA_HEADS, TQS, TKI)))
    tb = jnp.stack([_toeplitz(gd_b, 0, TQS, TKI, NEG), _toeplitz(gd_b, TKI, TQS, TKI), edge])
    tb = tb.reshape(3, NSA_KV, NSA_HPG, TQS, TKI).transpose(1, 0, 4, 2, 3).reshape(NSA_KV, 3, TKI, cols) * LOG2E
    nk = s // TKI
    e = np.zeros((nk, TKI, SEL_LANES), np.float32)
    for kj in range(nk):
        for k in range(TKI):
            e[kj, k, kj * (TKI // SEL_BLOCK) + k // SEL_BLOCK] = 1.0
    e = jnp.asarray(e).astype(BF16)
    acc_rows = NSA_HD + ONES_ROWS
    return pl.pallas_call(
        _nsa_sw_kernel,
        grid=(b, NSA_KV, s // TQS),
        in_specs=[pl.BlockSpec((None, TQS, 256), lambda bi, g, qi: (bi, qi, g)),
                  pl.BlockSpec((None, s, LANE), lambda bi, g, qi: (bi, 0, g)),
                  pl.BlockSpec((None, s, LANE), lambda bi, g, qi: (bi, 0, g)),
                  pl.BlockSpec((None, nk, NSA_HD, TKI), lambda bi, g, qi: (bi, 0, g, 0)),
                  pl.BlockSpec((None, nk, NSA_HD, TKI), lambda bi, g, qi: (bi, 0, g, 0)),
                  pl.BlockSpec((None, None, TQS, SEL_LANES), lambda bi, g, qi: (bi, g, qi, 0)),
                  pl.BlockSpec(e.shape, lambda bi, g, qi: (0, 0, 0)),
                  pl.BlockSpec((None, 3, TKI, cols), lambda bi, g, qi: (g, 0, 0, 0))],
        out_specs=[pl.BlockSpec((None, TQS, 256), lambda bi, g, qi: (bi, qi, g)),
                   pl.BlockSpec((None, TQS, 256), lambda bi, g, qi: (bi, qi, g))],
        out_shape=[jax.ShapeDtypeStruct((b, s, 512), F32), jax.ShapeDtypeStruct((b, s, 512), F32)],
        scratch_shapes=_chain_scratch(1, acc_rows, cols) + _chain_scratch(1, acc_rows, cols),
        compiler_params=_cparams(("arbitrary", "arbitrary", "arbitrary")),
        name="nsa_sel_win",
    )(qb, ksel, kwin, selvt, winvt, sel, e, tb)


def _layer_norm(x, g, b):
    mu = jnp.mean(x, axis=-1, keepdims=True)
    xc = x - mu
    var = jnp.mean(xc * xc, axis=-1, keepdims=True)
    return xc * lax.rsqrt(var + LN_EPS) * g + b


def _gate_expand():
    e = np.zeros((3, LANE, NSA_HEADS * NSA_HD), np.float32)
    for h in range(NSA_HEADS):
        for j in range(3):
            e[j, h * 3 + j, h * NSA_HD:(h + 1) * NSA_HD] = 1.0
    return jnp.asarray(e).astype(BF16)


def _tail1_kernel(oa_ref, oc_ref, os_ref, ow_ref, gate_ref, ma_ref, mb_ref, x_ref, ada_ref,
                  wa_ref, wb_ref, wo_ref, sub_ref, eg_ref, g1_ref, b1_ref, o_ref):
    oa = oa_ref[...]
    parts = []
    for h in range(DA_HEADS):
        of = oa[:, h * DA_VD:(h + 1) * DA_VD]
        rr = lax.rsqrt(jnp.mean(of * of, axis=-1, keepdims=True) + RMS_EPS)
        parts.append(of * rr * sub_ref[...] * (1.0 - LAM_INIT))
    oan = jnp.concatenate(parts, axis=1)
    sg = jax.nn.sigmoid(gate_ref[...])
    ob = (_mm3(sg, eg_ref[0]) * oc_ref[...] + _mm3(sg, eg_ref[1]) * os_ref[...]
          + _mm3(sg, eg_ref[2]) * ow_ref[...])
    y = (jax.nn.sigmoid(ma_ref[...]) * _mm(oan, wa_ref[...])
         + jax.nn.sigmoid(mb_ref[...]) * _mm(ob, wb_ref[...]))
    z = ALPHA * x_ref[...] + ada_ref[2] * _mm(y, wo_ref[...])
    o_ref[...] = _layer_norm(z, g1_ref[...], b1_ref[...])


def _tail1(oa, oc, os_, ow, gate, ma, mb, x2d, ada, wa, wb, wo, sub, eg, g1, b1, tm, tiles_per_group):
    m = x2d.shape[0]
    row = lambda w: pl.BlockSpec((tm, w), lambda i: (i, 0))
    full = lambda a: pl.BlockSpec(a.shape, lambda i: (0,) * a.ndim)
    return pl.pallas_call(
        _tail1_kernel,
        grid=(m // tm,),
        in_specs=[row(512), row(512), row(512), row(512), row(128), row(1024), row(1024), row(1024),
                  _ada_spec(ada, tm, tiles_per_group),
                  full(wa), full(wb), full(wo), full(sub), full(eg), full(g1), full(b1)],
        out_specs=row(1024),
        out_shape=jax.ShapeDtypeStruct((m, D_MODEL), F32),
        compiler_params=_cparams(("arbitrary",)),
        name="tail_merge",
    )(oa, oc, os_, ow, gate, ma, mb, x2d, ada, wa, wb, wo, sub, eg, g1, b1)


def _tail2_kernel(x_ref, ada_ref, wu_ref, wd_ref, g2_ref, b2_ref, o_ref, h_scr, acc):
    f = pl.program_id(1)

    @pl.when(f == 0)
    def _():
        h_scr[...] = (x_ref[...] * (1.0 + ada_ref[4]) + ada_ref[3]).astype(BF16)
        acc[...] = jnp.zeros(acc.shape, F32)

    u = jnp.maximum(jnp.dot(h_scr[...], wu_ref[...], preferred_element_type=F32), 0.0)
    acc[...] += _mm(u * u, wd_ref[...])

    @pl.when(f == pl.num_programs(1) - 1)
    def _():
        z = ALPHA * x_ref[...] + ada_ref[5] * acc[...]
        o_ref[...] = _layer_norm(z, g2_ref[...], b2_ref[...])


def _tail2(x1, ada, wu, wd, g2, b2, tm, tiles_per_group):
    m = x1.shape[0]
    tf = 1024
    return pl.pallas_call(
        _tail2_kernel,
        grid=(m // tm, D_FF // tf),
        in_specs=[pl.BlockSpec((tm, D_MODEL), lambda i, f: (i, 0)),
                  _ada_spec(ada, tm, tiles_per_group),
                  pl.BlockSpec((D_MODEL, tf), lambda i, f: (0, f)),
                  pl.BlockSpec((tf, D_MODEL), lambda i, f: (f, 0)),
                  pl.BlockSpec((1, D_MODEL), lambda i, f: (0, 0)),
                  pl.BlockSpec((1, D_MODEL), lambda i, f: (0, 0))],
        out_specs=pl.BlockSpec((tm, D_MODEL), lambda i, f: (i, 0)),
        out_shape=jax.ShapeDtypeStruct((m, D_MODEL), F32),
        scratch_shapes=[pltpu.VMEM((tm, D_MODEL), BF16), pltpu.VMEM((tm, D_MODEL), F32)],
        compiler_params=_cparams(("arbitrary", "arbitrary")),
        name="tail_mlp",
    )(x1, ada, wu, wd, g2, b2)


def _da_decode_kernel(pt_ref, *refs):
    pages = refs[:PP]
    q_ref, kn_ref, vn_ref, bl_ref, b0_ref, lam_ref, o_ref, m_ref, l_ref, a_ref = refs[PP:]
    j = pl.program_id(1)
    last = j == pl.num_programs(1) - 1
    scale = DA_HD ** -0.5
    rows_pg = PAGE * 2 * DA_HEADS

    @pl.when(j == 0)
    def _():
        m_ref[...] = jnp.full(m_ref.shape, NEG, F32)
        l_ref[...] = jnp.zeros(l_ref.shape, F32)
        a_ref[...] = jnp.zeros(a_ref.shape, F32)

    q = q_ref[...]
    row = lax.broadcasted_iota(jnp.int32, (8, rows_pg), 0)
    col = lax.broadcasted_iota(jnp.int32, (8, rows_pg), 1)
    cmask = jnp.where((col & 7) == lax.shift_right_logical(row, 1), 0.0, NEG)
    scores = []
    xs = []
    for k in range(PP):
        x = pages[k][...].astype(BF16)
        sc = _mm_nt(q, x) * scale + cmask
        if k == PP - 1:
            sc = sc + jnp.where(last, bl_ref[...], 0.0)
        scores.append(sc)
        xs.append(x)
    s = jnp.concatenate(scores, axis=1)
    m_old = m_ref[...]
    m_new = jnp.maximum(m_old, jnp.max(s, axis=-1, keepdims=True))
    p = jnp.exp(s - m_new)
    alpha = jnp.exp(m_old - m_new)
    l_ref[...] = alpha * l_ref[...] + jnp.sum(p, axis=-1, keepdims=True)
    acc = alpha * a_ref[...]
    for k in range(PP):
        pv = pltpu.roll(p[:, k * rows_pg:(k + 1) * rows_pg], DA_HEADS, 1)
        acc = acc + _mm(pv, xs[k])
    a_ref[...] = acc
    m_ref[...] = m_new

    @pl.when(last)
    def _():
        s_new = jnp.sum(q.astype(F32) * kn_ref[...], axis=-1, keepdims=True) * scale + b0_ref[:, 0:1]
        m_o = m_ref[...]
        m_n = jnp.maximum(m_o, s_new)
        p_new = jnp.exp(s_new - m_n)
        al = jnp.exp(m_o - m_n)
        raw = (al * a_ref[...] + p_new * vn_ref[...]) / (al * l_ref[...] + p_new)
        o_ref[...] = raw - _diff_lambda(lam_ref) * pltpu.roll(raw, 7, 0)


def _da_decode(page_table, cache_rows, q8, k_new, v_new, bl, b0, da_lambda):
    b, n_pages = page_table.shape
    rows_pg = PAGE * 2 * DA_HEADS
    page_spec = lambda k: pl.BlockSpec((rows_pg, LANE), lambda bi, j, pt: (pt[bi, j * PP + k], 0))
    per_row = lambda: pl.BlockSpec((None, 8, LANE), lambda bi, j, pt: (bi, 0, 0))
    grid_spec = pltpu.PrefetchScalarGridSpec(
        num_scalar_prefetch=1,
        grid=(b, n_pages // PP),
        in_specs=[page_spec(k) for k in range(PP)] + [
            per_row(), per_row(), per_row(),
            pl.BlockSpec((8, rows_pg), lambda bi, j, pt: (0, 0)),
            pl.BlockSpec((8, LANE), lambda bi, j, pt: (0, 0)),
            pl.BlockSpec((4, DA_HD), lambda bi, j, pt: (0, 0))],
        out_specs=per_row(),
        scratch_shapes=[pltpu.VMEM((8, 1), F32), pltpu.VMEM((8, 1), F32), pltpu.VMEM((8, LANE), F32)])
    return pl.pallas_call(
        _da_decode_kernel,
        grid_spec=grid_spec,
        out_shape=jax.ShapeDtypeStruct((b, 8, LANE), F32),
        compiler_params=_cparams(("arbitrary", "arbitrary")),
        name="da_decode",
    )(page_table, *([cache_rows] * PP), q8, k_new, v_new, bl, b0, da_lambda)


def _nsa_decode1_kernel(ab_ref, q_ref, swa_ref, new_ref, bw_ref, pe_ref, w1c_ref, w2_ref, ovl_ref,
                        oc_ref, ow_ref, idx_ref):
    n_chunks = ab_ref.shape[0]
    kcvc = _cmp_stage2(ab_ref[...], pe_ref, w1c_ref, w2_ref)
    scale = NSA_HD ** -0.5
    swa = swa_ref[...].astype(BF16)
    n_win = swa.shape[0]
    score_rows = []
    for g in range(NSA_KV):
        q = q_ref[g]
        s = _mm_nt(q, kcvc[:, g * 256:g * 256 + 128]) * scale
        n = lax.broadcasted_iota(jnp.int32, s.shape, 1)
        p = _masked_softmax(s, n < n_chunks - 1)
        oc_ref[g] = _mm(p, kcvc[:, g * 256 + 128:g * 256 + 256])
        psum = jnp.sum(p[0:NSA_HPG], axis=0, keepdims=True)
        imp = _mm3(jnp.broadcast_to(psum, (8, n_chunks)), ovl_ref[...])
        blk = lax.broadcasted_iota(jnp.int32, imp.shape, 1)
        n_blk = n_chunks * CMP_STRIDE // SEL_BLOCK
        forced = (blk == 0) | (blk == n_blk - 1)
        score_rows.append(jnp.where(blk < n_blk, jnp.where(forced, SEL_BIG, imp), REMOVED)[0:1, :])
        new = new_ref[g:g + 1, :]
        sw = _mm_nt(q, swa) * scale + bw_ref[g][:, 0:n_win]
        s_new = (jnp.sum(q.astype(F32) * new, axis=-1, keepdims=True) * scale
                 + bw_ref[g][:, n_win:n_win + 1])
        m = jnp.maximum(jnp.max(sw, axis=-1, keepdims=True), s_new)
        e = jnp.exp(sw - m)
        e_new = jnp.exp(s_new - m)
        den = jnp.sum(e, axis=-1, keepdims=True) + e_new
        ow_ref[g] = (_mm(e, swa) + e_new * new) / den
    rest = jnp.full((SEL_LANES - NSA_KV, SEL_LANES), REMOVED, F32)
    picks_t = _topk_picks_cols(jnp.concatenate(score_rows + [rest], axis=0).T, N_SEL - 1)
    picks = picks_t.T
    for g in range(NSA_KV):
        idx_ref[g] = jnp.broadcast_to(picks[g:g + 1, :], (8, SEL_LANES)).astype(jnp.int32)


def _nsa_decode1(ab, qc, cache_swa, kvw_new, bw, pe8, w1c, w2, ovl):
    b, n_chunks, _ = ab.shape
    full = lambda a: pl.BlockSpec(a.shape, lambda bi: (0,) * a.ndim)
    out4 = lambda: pl.BlockSpec((None, NSA_KV, 8, LANE), lambda bi: (bi, 0, 0, 0))
    return pl.pallas_call(
        _nsa_decode1_kernel,
        grid=(b,),
        in_specs=[pl.BlockSpec((None, n_chunks, 512), lambda bi: (bi, 0, 0)),
                  pl.BlockSpec((None, NSA_KV, 8, LANE), lambda bi: (bi, 0, 0, 0)),
                  pl.BlockSpec((None, cache_swa.shape[1], LANE), lambda bi: (bi, 0, 0)),
                  pl.BlockSpec((None, NSA_KV, LANE), lambda bi: (bi, 0, 0)),
                  full(bw), full(pe8), full(w1c), full(w2), full(ovl)],
        out_specs=[out4(), out4(), out4()],
        out_shape=[jax.ShapeDtypeStruct((b, NSA_KV, 8, LANE), F32),
                   jax.ShapeDtypeStruct((b, NSA_KV, 8, LANE), F32),
                   jax.ShapeDtypeStruct((b, NSA_KV, 8, LANE), jnp.int32)],
        compiler_params=_cparams(("arbitrary",)),
        name="nsa_decode_cmp_win",
    )(ab, qc, cache_swa, kvw_new, bw, pe8, w1c, w2, ovl)


def _nsa_decode2_kernel(pt_ref, idx_ref, *refs, n_blk, n_pick):
    blks = refs[:n_pick]
    q_ref, new_ref, bs_ref, o_ref = refs[n_pick:]
    b = pl.program_id(0)
    g = pl.program_id(1)
    scale = NSA_HD ** -0.5
    q = q_ref[...]
    cols = blks[0].shape[0]
    scores = []
    xs = []
    for k in range(n_pick):
        x = blks[k][...].astype(BF16)
        blk = idx_ref[b, g, k]
        bias = (jnp.where(blk == n_blk - 1, bs_ref[:, cols:2 * cols], 0.0)
                + jnp.where(blk == n_blk - 2, bs_ref[:, 2 * cols:3 * cols], 0.0))
        scores.append(_mm_nt(q, x) * scale + bs_ref[:, 0:cols] + bias)
        xs.append(x)
    new = new_ref[...]
    s_new = jnp.sum(q.astype(F32) * new, axis=-1, keepdims=True) * scale + bs_ref[:, 3 * cols:3 * cols + 1]
    s = jnp.concatenate(scores, axis=1)
    m = jnp.maximum(jnp.max(s, axis=-1, keepdims=True), s_new)
    p = jnp.exp(s - m)
    p_new = jnp.exp(s_new - m)
    acc = p_new * new
    for k in range(n_pick):
        acc = acc + _mm(p[:, k * cols:(k + 1) * cols], xs[k])
    o_ref[...] = acc / (jnp.sum(p, axis=-1, keepdims=True) + p_new)


def _nsa_decode2(page_table, idx, cache_rows, qc, nsa_new, bs):
    b = page_table.shape[0]
    n_pick = idx.shape[2]
    rows_blk = SEL_BLOCK * 2 * NSA_KV

    def blk_spec(k):
        def blk_map(bi, g, pt, ix):
            blk = ix[bi, g, k]
            return (pt[bi, lax.shift_right_logical(blk, 1)] * 2 + (blk & 1), 0)
        return pl.BlockSpec((rows_blk, LANE), blk_map)

    grid_spec = pltpu.PrefetchScalarGridSpec(
        num_scalar_prefetch=2,
        grid=(b, NSA_KV),
        in_specs=[blk_spec(k) for k in range(n_pick)] + [
            pl.BlockSpec((None, None, 8, LANE), lambda bi, g, pt, ix: (bi, g, 0, 0)),
            pl.BlockSpec((None, None, 1, LANE), lambda bi, g, pt, ix: (bi, g, 0, 0)),
            pl.BlockSpec((None, 8, bs.shape[2]), lambda bi, g, pt, ix: (g, 0, 0))],
        out_specs=pl.BlockSpec((None, None, 8, LANE), lambda bi, g, pt, ix: (bi, g, 0, 0)))
    return pl.pallas_call(
        functools.partial(_nsa_decode2_kernel, n_blk=page_table.shape[1] * PAGE // SEL_BLOCK, n_pick=n_pick),
        grid_spec=grid_spec,
        out_shape=jax.ShapeDtypeStruct((b, NSA_KV, 8, LANE), F32),
        compiler_params=_cparams(("arbitrary", "arbitrary")),
        name="nsa_decode_sel",
    )(page_table, idx, *([cache_rows] * n_pick), qc, nsa_new, bs)


def _prompt_mixers(pr, b, s, gd_a, gd_b, da_lambda, w1, w2, pe8, w1c):
    sh = lambda a: a.reshape(b, s, a.shape[-1])
    nk = s // TKI
    vt = lambda a: a.reshape(b, nk, a.shape[1], TKI)
    o_a = _da_prompt(sh(pr["qa2"]), sh(pr["dak"]), vt(pr["davt"]), gd_a, da_lambda)
    n_chunks = s // CMP_STRIDE
    ab = _cmp_stage1_rows(pr["nsa"], w1)
    kcvc = _cmp_stage2_prompt(ab.reshape(b, n_chunks, 512), pe8, w1c, w2)
    o_c, sel = _nsa_cmp_prompt(sh(pr["qb"]), kcvc, _ovl_t(n_chunks))
    o_s, o_w = _nsa_sw_prompt(sh(pr["qb2"]), sh(pr["ksel"]), sh(pr["kwin"]), vt(pr["selvt"]), vt(pr["winvt"]),
                              sel, gd_b)
    flat = lambda a: a.reshape(b * s, a.shape[-1])
    return flat(o_a), flat(o_c), flat(o_s), flat(o_w)


def _sample_mixers(pr, page_table, cache_da, cache_nsa, cache_swa, gd_a, gd_b, da_lambda, w1, w2, pe8, w1c):
    b, n_pages = page_table.shape
    past = n_pages * PAGE
    n_pool = cache_da.shape[0]
    da_rows = cache_da.reshape(n_pool, PAGE, DA_HEADS, 2, LANE).transpose(0, 1, 3, 2, 4).reshape(-1, LANE)
    nsa_rows = cache_nsa.reshape(n_pool, PAGE, NSA_KV, 2, LANE).transpose(0, 1, 3, 2, 4).reshape(-1, LANE)
    qa = pr["qa"].reshape(b, DA_HEADS, 2, DA_HD)
    q8 = jnp.zeros((b, DA_HEADS, 2, 2, DA_HD), BF16)
    for c in range(2):
        q8 = q8.at[:, :, c, c, :].set(qa[:, :, c])
    q8 = q8.reshape(b, 8, LANE)
    da_new = pr["da"].reshape(b, DA_HEADS, 2, LANE)
    k_new = jnp.repeat(da_new[:, :, 0], 2, axis=1)
    v_new = jnp.repeat(da_new[:, :, 1], 2, axis=1)
    gda8 = jnp.repeat(gd_a, 2, axis=0)
    bl = jnp.repeat(gda8[:, PAGE - jnp.arange(PAGE)], 2 * DA_HEADS, axis=1)
    b0 = jnp.broadcast_to(gda8[:, 0:1], (8, LANE))
    o_a = _da_decode(page_table, da_rows, q8, k_new, v_new, bl, b0, da_lambda)
    o_a = o_a[:, 0::2, :].reshape(b, 512)
    chunks = PAGE // CMP_STRIDE
    ab = _cmp_stage1_pages(page_table, nsa_rows, w1).reshape(b, n_pages * chunks, 512)
    qb = pr["qb"].reshape(b, NSA_KV, NSA_HPG, NSA_HD)
    qc = jnp.zeros((b, NSA_KV, 8, LANE), BF16).at[:, :, :NSA_HPG, :NSA_HD].set(qb)
    gdb = jnp.pad(gd_b.reshape(NSA_KV, NSA_HPG, -1), ((0, 0), (0, 8 - NSA_HPG), (0, 0)))
    own = jnp.arange(NSA_KV)[:, None, None]
    n_win = cache_swa.shape[1]
    grp_w = jnp.arange(n_win * NSA_KV)[None, None, :] % NSA_KV
    bw = jnp.where(grp_w == own, jnp.repeat(gdb[:, :, n_win - jnp.arange(n_win)], NSA_KV, axis=2), NEG)
    bw = jnp.concatenate([bw, jnp.broadcast_to(gdb[:, :, 0:1], (NSA_KV, 8, LANE))], axis=2)
    o_c, o_w, idx = _nsa_decode1(ab, qc, cache_swa.reshape(b, n_win * NSA_KV, LANE),
                                 pr["kvw"].reshape(b, NSA_KV, LANE), bw, pe8, w1c, w2,
                                 _ovl_t(past // CMP_STRIDE))
    idx = idx[:, :, 0, :N_SEL - 1]
    rows_blk = SEL_BLOCK * 2 * NSA_KV
    kind = jnp.arange(rows_blk)[None, None, :] % (2 * NSA_KV)
    keep = kind == NSA_KV + own
    bs = jnp.concatenate([
        jnp.where(keep, 0.0, NEG) * jnp.ones((1, 8, 1), F32),
        jnp.repeat(gdb[:, :, SEL_BLOCK - jnp.arange(SEL_BLOCK)], 2 * NSA_KV, axis=2),
        jnp.repeat(gdb[:, :, 2 * SEL_BLOCK - jnp.arange(SEL_BLOCK)], 2 * NSA_KV, axis=2),
        jnp.broadcast_to(gdb[:, :, 0:1], (NSA_KV, 8, LANE))], axis=2)
    sel_new = pr["nsa"].reshape(b, NSA_KV, 2, LANE)[:, :, 1:2, :]
    o_s = _nsa_decode2(page_table, idx, nsa_rows, qc, sel_new, bs)
    o_c = o_c[:, :, :NSA_HPG, :NSA_HD].reshape(b, 512)
    o_w = o_w[:, :, :NSA_HPG, NSA_HD:].reshape(b, 512)
    o_s = o_s[:, :, :NSA_HPG, NSA_HD:].reshape(b, 512)
    return o_a, o_c, o_s, o_w


def kernel(x_prompt, x_sample, cache_da_kv, cache_nsa_kv, cache_swa_kv, page_table, c_prompt, c_sample, rel_bias, w_ada, b_ada, w_in, da_lambda, da_subln, cmp_pe, cmp_w1, cmp_w2, w_br_a, w_br_b, w_out, ln1_g, ln1_b, w_up, w_down, ln2_g, ln2_b):
    bp, s, _ = x_prompt.shape
    bs_ = x_sample.shape[0]
    w_perm, w_vt = _perm_w_in(w_in[0])
    w1, w2 = _cmp_weights(cmp_w1[0], cmp_w2[0])
    pe8 = jnp.broadcast_to(cmp_pe[0].reshape(2, 1, CMP_LEN * NSA_HD), (2, 8, CMP_LEN * NSA_HD))
    w1c = cmp_w1[0].reshape(2, CMP_LEN * NSA_HD, CMP_HID)
    gd_a = _dist_bias(rel_bias[:, :DA_HEADS], 1024)
    gd_b = _dist_bias(rel_bias[:, DA_HEADS:], 1024)
    wa, wb, wo = w_br_a[0].astype(BF16), w_br_b[0].astype(BF16), w_out[0].astype(BF16)
    wu, wd = w_up[0].astype(BF16), w_down[0].astype(BF16)
    sub = da_subln[0].reshape(1, DA_VD)
    eg = _gate_expand()
    g1, b1 = ln1_g[0].reshape(1, D_MODEL), ln1_b[0].reshape(1, D_MODEL)
    g2, b2 = ln2_g[0].reshape(1, D_MODEL), ln2_b[0].reshape(1, D_MODEL)
    lam = da_lambda[0]

    n_c = bp + bs_
    c_all = jnp.pad(jnp.concatenate([c_prompt, c_sample], 0), ((0, (-n_c) % 8), (0, 0)))
    ada = _ada(c_all, w_ada[0], b_ada[0])[:n_c].reshape(n_c, 6, D_MODEL)
    ada_p = jnp.transpose(ada[:bp], (1, 0, 2)).reshape(6, bp, 1, D_MODEL)
    ada_s = jnp.transpose(ada[bp:], (1, 0, 2)).reshape(6, 1, bs_, D_MODEL)

    def tail(mix, pr, x2d, ada_x, tm, tpg):
        o_a, o_c, o_s, o_w = mix
        x1 = _tail1(o_a, o_c, o_s, o_w, pr["gate"], pr["ma"], pr["mb"], x2d, ada_x,
                    wa, wb, wo, sub, eg, g1, b1, tm, tpg)
        tm2 = min(2 * tm, tm * tpg)
        return _tail2(x1, ada_x, wu, wd, g2, b2, tm2, tm * tpg // tm2)

    xp = x_prompt.reshape(bp * s, D_MODEL)
    pr_p = _proj(xp, ada_p, w_perm, w_vt, TKI, s // TKI)
    mix_p = _prompt_mixers(pr_p, bp, s, gd_a, gd_b, lam, w1, w2, pe8, w1c)
    tm_t = 512
    y_p = tail(mix_p, pr_p, xp, ada_p, tm_t, s // tm_t).reshape(bp, s, D_MODEL)
    xs = x_sample.reshape(bs_, D_MODEL)
    pr_s = _proj(xs, ada_s, w_perm, None, bs_, 1)
    mix_s = _sample_mixers(pr_s, page_table, cache_da_kv[0], cache_nsa_kv[0], cache_swa_kv[0],
                           gd_a, gd_b, lam, w1, w2, pe8, w1c)
    y_s = tail(mix_s, pr_s, xs, ada_s, bs_, 1).reshape(bs_, 1, D_MODEL)

    win = min(WINDOW, s)
    from_rows = lambda a, units: a.reshape(bp, s, 2, units, LANE).transpose(0, 1, 3, 2, 4).reshape(1, bp, s, units, 256)
    new_da_p = from_rows(pr_p["da"], DA_HEADS)
    new_nsa_p = from_rows(pr_p["nsa"], NSA_KV)
    new_swa_p = pr_p["kvw"].reshape(bp, s, NSA_KV, 2 * NSA_HD)[None, :, s - win:]
    new_da_s = pr_s["da"].reshape(1, bs_, 1, DA_HEADS, 4 * DA_HD)
    new_nsa_s = pr_s["nsa"].reshape(1, bs_, 1, NSA_KV, 4 * NSA_HD)
    new_swa_s = jnp.concatenate([cache_swa_kv[0][:, 1:], pr_s["kvw"].reshape(bs_, 1, NSA_KV, 2 * NSA_HD)],
                                axis=1)[None]
    return (y_p, y_s, new_da_p, new_nsa_p, new_swa_p, new_da_s, new_nsa_s, new_swa_s)
```

```python
import functools
import math

import numpy as np
import jax
import jax.numpy as jnp
from jax import lax
from jax.experimental import pallas as pl
from jax.experimental.pallas import tpu as pltpu

F32 = jnp.float32
BF16 = jnp.bfloat16

D_MODEL = 1024
PAGE = 128
DA_HEADS = 4
DA_HD = 64
DA_VD = 128
NSA_HEADS = 8
NSA_KV = 2
NSA_HPG = 4
NSA_HD = 64
CMP_STRIDE = 16
CMP_LEN = 32
CMP_HID = 64
SEL_BLOCK = 64
N_SEL = 16
WINDOW = 512
D_FF = 4096
N_BUCKETS = 32
MAX_DIST = 128
DEPTH = 1
ALPHA = (2 * DEPTH) ** 0.25
LN_EPS = 1e-5
RMS_EPS = 1e-5
NEG = -1e30
SEL_BIG = 1e9
LAM_INIT = 0.8 - 0.6 * math.exp(-0.3 * 0)
SPLIT_SIZES = (512, 512, 512, 512, 256, 256, 256, 24, 1024, 1024)

LANE = 128
VMEM_LIMIT = 56 * 1024 * 1024
TQ = 128
TKI = 256
TQS = 256
ONES_ROWS = 16
TDA = 512
SEL_LANES = 128
REMOVED = -3e38
PP = 32


def _cparams(sem):
    return pltpu.CompilerParams(dimension_semantics=sem, vmem_limit_bytes=VMEM_LIMIT)


def _mm(a, b):
    return jnp.dot(a.astype(BF16), b.astype(BF16), preferred_element_type=F32)


def _mm_nt(a, b):
    return lax.dot_general(a.astype(BF16), b.astype(BF16), (((1,), (1,)), ((), ())),
                           preferred_element_type=F32)


def _mm3(x, w):
    hi = x.astype(BF16)
    r = x - hi.astype(F32)
    mid = r.astype(BF16)
    lo = (r - mid.astype(F32)).astype(BF16)
    return (jnp.dot(hi, w, preferred_element_type=F32) + jnp.dot(mid, w, preferred_element_type=F32)
            + jnp.dot(lo, w, preferred_element_type=F32))


def _masked_softmax(s, valid):
    l = jnp.where(valid, s, NEG)
    m = jnp.max(l, axis=-1, keepdims=True)
    e = jnp.where(valid, jnp.exp(l - m), 0.0)
    return e / jnp.maximum(jnp.sum(e, axis=-1, keepdims=True), 1e-30)


def _t5_bucket(dist):
    n = jnp.maximum(dist, 0)
    max_exact = N_BUCKETS // 2
    nf = jnp.maximum(n, 1).astype(F32)
    large = max_exact + (jnp.log(nf / max_exact) / math.log(MAX_DIST / max_exact)
                         * (N_BUCKETS - max_exact)).astype(jnp.int32)
    return jnp.where(n < max_exact, n, jnp.minimum(large, N_BUCKETS - 1))


def _dist_bias(tbl, n):
    d = jnp.arange(n, dtype=jnp.int32)
    g = tbl[_t5_bucket(d)] - tbl[N_BUCKETS - 1][None, :]
    return jnp.transpose(g)


def _toeplitz(gd, offset, rows, cols, below=0.0):
    heads, n = gd.shape
    length = rows + cols - 1
    assert offset + rows <= n
    lo = offset - cols + 1
    hvec = gd[:, max(lo, 0):offset + rows]
    if lo < 0:
        hvec = jnp.concatenate([jnp.full((heads, -lo), below, gd.dtype), hvec], axis=1)
    rev = jnp.concatenate([hvec[:, ::-1], jnp.zeros((heads, 1), gd.dtype)], axis=1)
    flat = jnp.tile(rev, (1, rows))[:, :rows * length].reshape(heads, rows, length)
    return flat[:, :, rows - 1:rows - 1 + cols]


def _diff_lambda(lam_ref):
    l = lam_ref[...]
    a = jnp.sum(l[0:1, :] * l[1:2, :], axis=-1, keepdims=True)
    b = jnp.sum(l[2:3, :] * l[3:4, :], axis=-1, keepdims=True)
    return jnp.exp(a) - jnp.exp(b) + LAM_INIT


def _ada_kernel(c_ref, w_ref, b_ref, o_ref):
    c = c_ref[...]
    o_ref[...] = _mm(c * jax.nn.sigmoid(c), w_ref[...]) + b_ref[...]


def _ada(c, w_ada, b_ada):
    m = c.shape[0]
    n = w_ada.shape[1]
    tn = 512
    return pl.pallas_call(
        _ada_kernel,
        grid=(n // tn,),
        in_specs=[pl.BlockSpec((m, D_MODEL), lambda j: (0, 0)),
                  pl.BlockSpec((D_MODEL, tn), lambda j: (0, j)),
                  pl.BlockSpec((1, tn), lambda j: (0, j))],
        out_specs=pl.BlockSpec((m, tn), lambda j: (0, j)),
        out_shape=jax.ShapeDtypeStruct((m, n), F32),
        compiler_params=_cparams(("arbitrary",)),
        name="ada",
    )(c, w_ada, b_ada.reshape(1, n))


PROJ_GROUPS = (("qa", 512, BF16), ("da", 1024, F32), ("qb", 512, BF16), ("nsa", 512, F32), ("kvw", 256, F32),
               ("gate", 128, F32), ("ma", 1024, F32), ("mb", 1024, F32))
PROJ_KEY_GROUPS = (("ksel", 256), ("kwin", 256))
STATE_ROWS = {"da": DA_HEADS, "nsa": NSA_KV}
LOG2E = 1.4426950408889634
Q_SCALE = {"qa": DA_HD ** -0.5 * LOG2E, "qb": NSA_HD ** -0.5 * LOG2E}
PROJ_W = sum(w for _, w, _ in PROJ_GROUPS)
PROJ_W_KV = PROJ_W + sum(w for _, w in PROJ_KEY_GROUPS)
PROJ_VT = (("davt", DA_HEADS * DA_VD), ("selvt", NSA_KV * NSA_HD), ("winvt", NSA_KV * NSA_HD))


def _perm_w_in(w_in):
    parts = jnp.split(w_in, np.cumsum(SPLIT_SIZES)[:-1].tolist(), axis=1)
    qa, ka, va, qn, kvc, kvs, kvw, gb, ma, mb = parts
    da = jnp.concatenate([jnp.concatenate([ka[:, h * 128:(h + 1) * 128], va[:, h * 128:(h + 1) * 128]], 1)
                          for h in range(DA_HEADS)], 1)
    nsa = jnp.concatenate([jnp.concatenate([kvc[:, g * 128:(g + 1) * 128], kvs[:, g * 128:(g + 1) * 128]], 1)
                           for g in range(NSA_KV)], 1)
    gate = jnp.pad(gb, ((0, 0), (0, LANE - gb.shape[1])))
    keys2 = lambda kv: jnp.concatenate([kv[:, g * 128:g * 128 + 64] for g in range(NSA_KV) for _ in range(2)], 1)
    vals = lambda kv: jnp.concatenate([kv[:, g * 128 + 64:(g + 1) * 128] for g in range(NSA_KV)], 1)
    cols = dict(qa=qa, da=da, qb=qn, nsa=nsa, kvw=kvw, gate=gate, ma=ma, mb=mb, ksel=keys2(kvs), kwin=keys2(kvw))
    names = [n for n, _, _ in PROJ_GROUPS] + [n for n, _ in PROJ_KEY_GROUPS]
    w_perm = jnp.concatenate([cols[n] for n in names], 1).astype(BF16)
    w_vt = jnp.transpose(jnp.concatenate([va, vals(kvs), vals(kvw)], 1)).astype(BF16)
    return w_perm, w_vt


def _proj_kernel(x_ref, ada_ref, w_ref, *refs, with_kv):
    h = (x_ref[...] * (1.0 + ada_ref[1]) + ada_ref[0]).astype(BF16)
    o_refs = refs[1:] if with_kv else refs
    off = 0
    k = 0
    for name, width, dt in PROJ_GROUPS:
        acc = jnp.dot(h, w_ref[:, off:off + width], preferred_element_type=F32)
        if with_kv and name in STATE_ROWS:
            units = STATE_ROWS[name]
            tm = acc.shape[0]
            for u in range(units):
                for half in range(2):
                    o_refs[k][pl.ds(half * units + u, tm, stride=2 * units), :] = (
                        acc[:, (u * 2 + half) * LANE:(u * 2 + half + 1) * LANE])
        else:
            o_refs[k][...] = acc.astype(dt)
        k += 1
        if with_kv and name in Q_SCALE:
            o_refs[k][...] = (acc * Q_SCALE[name]).astype(BF16)
            k += 1
        if with_kv and name == "da":
            o_refs[k][...] = jnp.concatenate([acc[:, hd * 256:hd * 256 + 128] for hd in range(DA_HEADS)],
                                             axis=1).astype(BF16)
            k += 1
        off += width
    if with_kv:
        for _, width in PROJ_KEY_GROUPS:
            o_refs[k][...] = jnp.dot(h, w_ref[:, off:off + width], preferred_element_type=F32).astype(BF16)
            k += 1
            off += width
        vt = _mm_nt(refs[0][...], h)
        r0 = 0
        for _, rows in PROJ_VT:
            o_refs[k][...] = vt[r0:r0 + rows].astype(BF16)
            k += 1
            r0 += rows


def _ada_spec(ada, tm, tiles_per_group):
    r = ada.shape[2]
    return pl.BlockSpec((6, None, r, D_MODEL), lambda i, *_: (0, i // tiles_per_group, 0, 0))


def _proj(x2d, ada, w_perm, w_vt, tm, tiles_per_group):
    m = x2d.shape[0]
    with_kv = w_vt is not None
    outs = []
    for name, w, dt in PROJ_GROUPS:
        if with_kv and name in STATE_ROWS:
            rows_pos = w // LANE
            outs.append((name, (m * rows_pos, LANE), (tm * rows_pos, LANE), dt))
        else:
            outs.append((name, (m, w), (tm, w), dt))
        if with_kv and name in Q_SCALE:
            outs.append((name + "2", (m, w), (tm, w), BF16))
        if with_kv and name == "da":
            outs.append(("dak", (m, 512), (tm, 512), BF16))
    in_specs = [pl.BlockSpec((tm, D_MODEL), lambda i: (i, 0)),
                _ada_spec(ada, tm, tiles_per_group),
                pl.BlockSpec((D_MODEL, PROJ_W_KV if with_kv else PROJ_W), lambda i: (0, 0))]
    args = [x2d, ada, w_perm]
    if with_kv:
        assert tm == TKI
        outs += [(n, (m, w), (tm, w), BF16) for n, w in PROJ_KEY_GROUPS]
        outs += [(n, (m // tm, r, tm), (None, r, tm), BF16) for n, r in PROJ_VT]
        in_specs.append(pl.BlockSpec(w_vt.shape, lambda i: (0, 0)))
        args.append(w_vt)
    res = pl.pallas_call(
        functools.partial(_proj_kernel, with_kv=with_kv),
        grid=(m // tm,),
        in_specs=in_specs,
        out_specs=[pl.BlockSpec(blk, (lambda i: (i, 0)) if len(blk) == 2 else (lambda i: (i, 0, 0)))
                   for _, _, blk, _ in outs],
        out_shape=[jax.ShapeDtypeStruct(shape, dt) for _, shape, _, dt in outs],
        compiler_params=_cparams(("arbitrary",)),
        name="proj",
    )(*args)
    return {n: o for (n, _, _, _), o in zip(outs, res)}


class _Chain:
    def __init__(self, s_bufs, p_bufs, al_bufs, m_ref, acc_ref, score_fn, vext_fn, bias_fn):
        self.s, self.p, self.al = s_bufs, p_bufs, al_bufs
        self.m, self.acc = m_ref, acc_ref
        self.score_fn, self.vext_fn, self.bias_fn = score_fn, vext_fn, bias_fn
        self.maps = m_ref.shape[0]
        self.cur = 0

    def prime(self, j):
        self.cur = 0
        self.m[...] = jnp.full(self.m.shape, NEG, F32)
        self.acc[...] = jnp.zeros(self.acc.shape, F32)
        self.p[1][...] = jnp.zeros(self.p[1].shape, BF16)
        self.al[1][...] = jnp.ones(self.al[1].shape, F32)
        self.score_fn(j, self.s[0])

    def _finish(self, k, buf, pv):
        self.acc[k] = self.acc[k] * self.al[buf][k, 0:1, :] + pv

    def step(self, j_prev, j_next, table=None, valid=None, pen=None, prev_valid=None):
        c, o = self.cur, 1 - self.cur
        vext = self.vext_fn(j_prev)
        if prev_valid is not None:
            vext = jnp.where(prev_valid, vext, jnp.zeros_like(vext))
        pv = [jnp.dot(vext, self.p[o][k], preferred_element_type=F32) for k in range(self.maps)]
        if j_next is not None:
            self.score_fn(j_next, self.s[o])
        for k in range(self.maps):
            for l0 in range(0, self.m.shape[2], LANE):
                cols = slice(l0, l0 + LANE)
                s = self.s[c][k, :, cols]
                if table is not None:
                    s = s + self.bias_fn(table, l0)
                if pen is not None:
                    s = s + pen
                m_cur = jnp.max(s, axis=0, keepdims=True)
                if valid is not None:
                    m_cur = jnp.where(valid, m_cur, NEG)
                m_old = self.m[k, 0:1, cols]
                m_new = jnp.maximum(m_old, m_cur)
                self.p[c][k, :, cols] = jnp.exp2(s - m_new).astype(BF16)
                self.al[c][k, :, cols] = jnp.broadcast_to(jnp.exp2(m_old - m_new), (8, LANE))
                self.m[k, :, cols] = jnp.broadcast_to(m_new, (8, LANE))
            self._finish(k, o, pv[k])
        self.cur = o

    def step_eager(self, j_cur, j_next, table=None, valid=None, pen=None):
        c, o = self.cur, 1 - self.cur
        if j_next is not None:
            self.score_fn(j_next, self.s[o])
        vext = self.vext_fn(j_cur)
        if valid is not None:
            vext = jnp.where(valid, vext, jnp.zeros_like(vext))
        for k in range(self.maps):
            ps, alphas = [], []
            for l0 in range(0, self.m.shape[2], LANE):
                cols = slice(l0, l0 + LANE)
                s = self.s[c][k, :, cols]
                if table is not None:
                    s = s + self.bias_fn(table, l0)
                if pen is not None:
                    s = s + pen
                m_cur = jnp.max(s, axis=0, keepdims=True)
                if valid is not None:
                    m_cur = jnp.where(valid, m_cur, NEG)
                m_old = self.m[k, 0:1, cols]
                m_new = jnp.maximum(m_old, m_cur)
                ps.append(jnp.exp2(s - m_new).astype(BF16))
                alphas.append(jnp.exp2(m_old - m_new))
                self.m[k, :, cols] = jnp.broadcast_to(m_new, (8, LANE))
            self.acc[k] = (self.acc[k] * jnp.concatenate(alphas, axis=1)
                           + jnp.dot(vext, jnp.concatenate(ps, axis=1), preferred_element_type=F32))
        self.cur = o

    def flush(self, j_prev):
        o = 1 - self.cur
        vext = self.vext_fn(j_prev)
        for k in range(self.maps):
            self._finish(k, o, jnp.dot(vext, self.p[o][k], preferred_element_type=F32))


def _chain_scratch(maps, acc_rows, queries):
    return [pltpu.VMEM((maps, TKI, queries), F32), pltpu.VMEM((maps, TKI, queries), F32),
            pltpu.VMEM((maps, TKI, queries), BF16), pltpu.VMEM((maps, TKI, queries), BF16),
            pltpu.VMEM((maps, 8, queries), F32), pltpu.VMEM((maps, 8, queries), F32),
            pltpu.VMEM((maps, 8, queries), F32), pltpu.VMEM((maps, acc_rows, queries), F32)]


def _key_rows(ref, j):
    return pl.ds(pl.multiple_of(j * TKI, TKI), TKI)


def _da_kernel(q_ref, k_ref, vt_ref, tb_ref, lam_ref, o_ref, s_a, s_b, p_a, p_b, al_a, al_b, m_ref, acc_ref):
    qi = pl.program_id(2)
    q = q_ref[...]
    lane = lax.broadcasted_iota(jnp.int32, q.shape, 1)
    zero = jnp.zeros_like(q)
    qt_maps = tuple(jnp.where(keep, q, zero).astype(F32).T.astype(BF16) for keep in (lane < DA_HD, lane >= DA_HD))
    ones = jnp.ones((ONES_ROWS, TKI), BF16)

    def scores(j, dst):
        kk = k_ref[_key_rows(k_ref, j), :]
        for c in range(2):
            dst[c] = jnp.dot(kk, qt_maps[c], preferred_element_type=F32)

    chain = _Chain((s_a, s_b), (p_a, p_b), (al_a, al_b), m_ref, acc_ref, scores,
                   lambda j: jnp.concatenate([vt_ref[j], ones], axis=0),
                   lambda table, l0: tb_ref[table, :, l0:l0 + LANE])
    n_far = jnp.maximum(2 * qi - 1, 0)
    j_d = 2 * qi
    j_s = jnp.maximum(2 * qi - 1, 0)
    chain.prime(jnp.where(n_far > 0, 0, j_d))

    def pair(i, carry):
        a = 2 * i
        b = jnp.minimum(a + 1, n_far - 1)
        chain.step_eager(a, b)
        chain.step_eager(b, jnp.where(a + 2 < n_far, a + 2, j_d), valid=a + 1 < n_far)
        return carry

    lax.fori_loop(0, lax.shift_right_logical(n_far + 1, 1), pair, 0)
    chain.step_eager(j_d, j_s, table=1)
    chain.step_eager(j_s, j_d + 1, table=0, pen=jnp.where(qi > 0, 0.0, NEG))
    chain.step_eager(j_d + 1, None, table=2)
    lam = _diff_lambda(lam_ref)
    o_t = (acc_ref[0, 0:DA_VD, :] / acc_ref[0, DA_VD:DA_VD + 1, :]
           - lam * (acc_ref[1, 0:DA_VD, :] / acc_ref[1, DA_VD:DA_VD + 1, :]))
    o_ref[...] = o_t.T


def _da_prompt(qa, dak, davt, gd_a, da_lambda):
    b, s, _ = qa.shape
    tb = jnp.stack([_toeplitz(gd_a, TKI, TDA, TKI), _toeplitz(gd_a, 0, TDA, TKI, NEG),
                    _toeplitz(gd_a, -TKI, TDA, TKI, NEG)], axis=1)
    tb = jnp.swapaxes(tb, 2, 3) * LOG2E
    acc_rows = DA_VD + ONES_ROWS
    return pl.pallas_call(
        _da_kernel,
        grid=(b, DA_HEADS, s // TDA),
        in_specs=[pl.BlockSpec((None, TDA, 128), lambda bi, h, qi: (bi, qi, h)),
                  pl.BlockSpec((None, s, 128), lambda bi, h, qi: (bi, 0, h)),
                  pl.BlockSpec((None, s // TKI, DA_VD, TKI), lambda bi, h, qi: (bi, 0, h, 0)),
                  pl.BlockSpec((None, 3, TKI, TDA), lambda bi, h, qi: (h, 0, 0, 0)),
                  pl.BlockSpec((4, DA_HD), lambda bi, h, qi: (0, 0))],
        out_specs=pl.BlockSpec((None, TDA, 128), lambda bi, h, qi: (bi, qi, h)),
        out_shape=jax.ShapeDtypeStruct((b, s, DA_HEADS * DA_VD), F32),
        scratch_shapes=_chain_scratch(2, acc_rows, TDA),
        compiler_params=_cparams(("arbitrary", "arbitrary", "arbitrary")),
        name="da_prompt",
    )(qa, dak, davt, tb, da_lambda)


def _cmp_weights(cmp_w1, cmp_w2):
    w1 = jnp.zeros((CMP_STRIDE, 2, NSA_HD, 2, 2, CMP_HID), F32)
    for half in range(2):
        for c in range(2):
            blk = cmp_w1[c, half * CMP_STRIDE:(half + 1) * CMP_STRIDE]
            w1 = w1.at[:, c, :, half, c, :].set(blk)
    w1 = w1.reshape(CMP_STRIDE * 2 * NSA_HD, 2 * 2 * CMP_HID).astype(BF16)
    w2 = jnp.zeros((2, CMP_HID, 2, 2, NSA_HD), F32)
    for c in range(2):
        for rep in range(2):
            w2 = w2.at[c, :, c, rep, :].set(cmp_w2[c])
    w2 = w2.reshape(2 * CMP_HID, 2 * 2 * NSA_HD).astype(BF16)
    return w1, w2


def _cmp1_rows_kernel(x_ref, w_ref, o_ref, *, tm):
    rows_pos = 2 * NSA_KV
    for g in range(NSA_KV):
        xg = jnp.concatenate([x_ref[pl.ds(l * rows_pos + g, tm, stride=CMP_STRIDE * rows_pos), :]
                              for l in range(CMP_STRIDE)], axis=1)
        o_ref[:, g * 256:(g + 1) * 256] = _mm(xg, w_ref[...])


def _cmp1_pages_kernel(pt_ref, *refs, n_pg):
    pages, w_ref, o_ref = refs[:n_pg], refs[n_pg], refs[n_pg + 1]
    rows_pos = 2 * NSA_KV
    chunks = PAGE // CMP_STRIDE
    for g in range(NSA_KV):
        xg = jnp.concatenate(
            [jnp.concatenate([pages[k][pl.ds(l * rows_pos + g, chunks, stride=CMP_STRIDE * rows_pos), :]
                              for k in range(n_pg)], axis=0) for l in range(CMP_STRIDE)], axis=1)
        o_ref[:, g * 256:(g + 1) * 256] = _mm(xg, w_ref[...])


def _cmp_stage1_pages(page_table, rows2d, w1):
    b, n_pages = page_table.shape
    rows_page = PAGE * 2 * NSA_KV
    chunks = PAGE // CMP_STRIDE
    n_pg = math.gcd(b * n_pages, PP)
    page_spec = lambda k: pl.BlockSpec((rows_page, LANE), lambda i, pt: (pt[i * n_pg + k], 0))
    grid_spec = pltpu.PrefetchScalarGridSpec(
        num_scalar_prefetch=1,
        grid=(b * n_pages // n_pg,),
        in_specs=[page_spec(k) for k in range(n_pg)] + [pl.BlockSpec(w1.shape, lambda i, pt: (0, 0))],
        out_specs=pl.BlockSpec((n_pg * chunks, 512), lambda i, pt: (i, 0)))
    return pl.pallas_call(
        functools.partial(_cmp1_pages_kernel, n_pg=n_pg),
        grid_spec=grid_spec,
        out_shape=jax.ShapeDtypeStruct((b * n_pages * chunks, 512), F32),
        compiler_params=_cparams(("arbitrary",)),
        name="cmp_stage1_pages",
    )(page_table.reshape(-1), *([rows2d] * n_pg), w1)


def _cmp_stage1_rows(rows2d, w1):
    rows_chunk = CMP_STRIDE * 2 * NSA_KV
    r = rows2d.shape[0] // rows_chunk
    tm = math.gcd(r, 128)
    return pl.pallas_call(
        functools.partial(_cmp1_rows_kernel, tm=tm),
        grid=(r // tm,),
        in_specs=[pl.BlockSpec((tm * rows_chunk, LANE), lambda i: (i, 0)),
                  pl.BlockSpec(w1.shape, lambda i: (0, 0))],
        out_specs=pl.BlockSpec((tm, 512), lambda i: (i, 0)),
        out_shape=jax.ShapeDtypeStruct((r, 512), F32),
        compiler_params=_cparams(("arbitrary",)),
        name="cmp_stage1_rows",
    )(rows2d, w1)


def _cmp_stage2(ab, pe_ref, w1c_ref, w2_ref):
    n = ab.shape[0]
    cst = jnp.concatenate([_mm(pe_ref[c], w1c_ref[c])[0:1, :] for c in range(2)], axis=1)
    outs = []
    for g in range(NSA_KV):
        a = ab[:, g * 256:g * 256 + 128]
        bn = pltpu.roll(ab[:, g * 256 + 128:g * 256 + 256], n - 1, 0)
        hid = jax.nn.gelu(a + bn + cst)
        outs.append(_mm(hid, w2_ref[...]))
    return jnp.concatenate(outs, axis=1)


def _cmp2_kernel(ab_ref, pe_ref, w1c_ref, w2_ref, o_ref):
    o_ref[...] = _cmp_stage2(ab_ref[...], pe_ref, w1c_ref, w2_ref).astype(BF16)


def _cmp_stage2_prompt(ab, pe8, w1c, w2):
    b, n, _ = ab.shape
    return pl.pallas_call(
        _cmp2_kernel,
        grid=(b,),
        in_specs=[pl.BlockSpec((None, n, 512), lambda i: (i, 0, 0)),
                  pl.BlockSpec(pe8.shape, lambda i: (0, 0, 0)),
                  pl.BlockSpec(w1c.shape, lambda i: (0, 0, 0)),
                  pl.BlockSpec(w2.shape, lambda i: (0, 0))],
        out_specs=pl.BlockSpec((None, n, 512), lambda i: (i, 0, 0)),
        out_shape=jax.ShapeDtypeStruct((b, n, 512), BF16),
        compiler_params=_cparams(("arbitrary",)),
        name="cmp_stage2",
    )(ab, pe8, w1c, w2)


def _ovl_t(n_chunks):
    n = np.arange(n_chunks)[:, None]
    j = np.arange(SEL_LANES)[None, :]
    ovl = (n * CMP_STRIDE < j * SEL_BLOCK + SEL_BLOCK) & (j * SEL_BLOCK < n * CMP_STRIDE + CMP_LEN)
    ovl &= n < n_chunks - 1
    return jnp.asarray(ovl.astype(np.float32)).astype(BF16)


def _topk_picks_cols(score_t, n_pick):
    blk = lax.broadcasted_iota(jnp.int32, score_t.shape, 0).astype(F32)
    picks = jnp.zeros(score_t.shape, F32)
    sc = score_t
    for it in range(n_pick):
        m = jnp.max(sc, axis=0, keepdims=True)
        first = jnp.min(jnp.where(sc == m, blk, float(SEL_LANES)), axis=0, keepdims=True)
        picks = jnp.where(blk == float(it), first, picks)
        sc = jnp.where(blk == first, REMOVED, sc)
    return picks


def _topk_mask_cols(score_t, n_pick):
    blk = lax.broadcasted_iota(jnp.int32, score_t.shape, 0).astype(F32)
    sel = jnp.zeros(score_t.shape, F32)
    sc = score_t
    for _ in range(n_pick):
        m = jnp.max(sc, axis=0, keepdims=True)
        first = jnp.min(jnp.where(sc == m, blk, float(SEL_LANES)), axis=0, keepdims=True)
        hit = blk == first
        sel = jnp.where(hit, 1.0, sel)
        sc = jnp.where(hit, REMOVED, sc)
    return sel


def _stack_heads(q):
    lane = lax.broadcasted_iota(jnp.int32, (q.shape[0], LANE), 1)
    zero = jnp.zeros((q.shape[0], LANE), q.dtype)
    parts = []
    for hp in range(NSA_HPG):
        blk = q[:, (hp // 2) * LANE:(hp // 2 + 1) * LANE]
        keep = (lane < NSA_HD) if hp % 2 == 0 else (lane >= NSA_HD)
        parts.append(jnp.where(keep, blk, zero))
    return jnp.concatenate(parts, axis=0)


def _unstack_heads(o, tq):
    lane = lax.broadcasted_iota(jnp.int32, (tq, LANE), 1)
    pairs = [jnp.where(lane < NSA_HD, o[(2 * m) * tq:(2 * m + 1) * tq], o[(2 * m + 1) * tq:(2 * m + 2) * tq])
             for m in range(2)]
    return jnp.concatenate(pairs, axis=1)


def _nsa_cmp_kernel(q_ref, kcvc_ref, ovl_ref, oc_ref, sel_ref, *, n_cmp, n_slc):
    qi = pl.program_id(2)
    n_chunks = kcvc_ref.shape[0]
    qs = _stack_heads(q_ref[...])
    kc2 = kcvc_ref[:, 0:128]
    vc2 = kcvc_ref[:, 128:256]
    s = _mm_nt(qs, kc2) * (NSA_HD ** -0.5)
    rows = NSA_HPG * TQ
    t = qi * TQ + (lax.broadcasted_iota(jnp.int32, (rows, n_chunks), 0) & (TQ - 1))
    n = lax.broadcasted_iota(jnp.int32, (rows, n_chunks), 1)
    p = _masked_softmax(s, (n * CMP_STRIDE + (CMP_LEN - 1) <= t) & (n < n_cmp))
    oc_ref[...] = _unstack_heads(_mm(p, vc2), TQ)
    psum = p[0:TQ] + p[TQ:2 * TQ] + p[2 * TQ:3 * TQ] + p[3 * TQ:4 * TQ]
    imp = _mm3(psum, ovl_ref[...])
    tq = qi * TQ + lax.broadcasted_iota(jnp.int32, (TQ, SEL_LANES), 0)
    blk = lax.broadcasted_iota(jnp.int32, (TQ, SEL_LANES), 1)
    cur = lax.shift_right_logical(tq, 6)
    forced = (blk == 0) | (blk == cur) | (blk == cur - 1)
    score = jnp.where(forced, SEL_BIG, jnp.where(blk * SEL_BLOCK <= tq, imp, -SEL_BIG))
    score = jnp.where(blk < n_slc, score, REMOVED)
    sel_t = _topk_mask_cols(score.T, min(N_SEL, n_slc))
    sel_ref[...] = jnp.where(sel_t > 0.5, 0.0, NEG).astype(BF16)


def _nsa_cmp_prompt(qb, kcvc, ovl):
    b, s, _ = qb.shape
    n_chunks = kcvc.shape[1]
    kern = functools.partial(_nsa_cmp_kernel, n_cmp=n_chunks - 1, n_slc=s // SEL_BLOCK)
    return pl.pallas_call(
        kern,
        grid=(b, NSA_KV, s // TQ),
        in_specs=[pl.BlockSpec((None, TQ, 256), lambda bi, g, qi: (bi, qi, g)),
                  pl.BlockSpec((None, n_chunks, 256), lambda bi, g, qi: (bi, 0, g)),
                  pl.BlockSpec(ovl.shape, lambda bi, g, qi: (0, 0))],
        out_specs=[pl.BlockSpec((None, TQ, 256), lambda bi, g, qi: (bi, qi, g)),
                   pl.BlockSpec((None, None, TQ, SEL_LANES), lambda bi, g, qi: (bi, g, qi, 0))],
        out_shape=[jax.ShapeDtypeStruct((b, s, 512), F32),
                   jax.ShapeDtypeStruct((b, NSA_KV, s, SEL_LANES), BF16)],
        compiler_params=_cparams(("arbitrary", "arbitrary", "arbitrary")),
        name="nsa_cmp",
    )(qb, kcvc, ovl)


def _nsa_sw_kernel(q_ref, ks_ref, kw_ref, vs_ref, vw_ref, sel_ref, e_ref, tb_ref, os_ref, ow_ref, *scratch):
    m = pl.program_id(2)
    qs = _stack_heads(q_ref[...])
    qs_t = qs.astype(F32).T.astype(BF16)
    sel_t = jnp.concatenate([sel_ref[t * TQ:(t + 1) * TQ, :] for t in range(TQS // TQ)] * NSA_HPG, axis=1)
    qs_sel_t = jnp.concatenate([qs_t, sel_t], axis=0)
    ones = jnp.ones((ONES_ROWS, TKI), BF16)

    def sel_scores(j, dst):
        keys = jnp.concatenate([ks_ref[_key_rows(ks_ref, j), :], e_ref[j]], axis=1)
        dst[0] = jnp.dot(keys, qs_sel_t, preferred_element_type=F32)

    def win_scores(j, dst):
        dst[0] = jnp.dot(kw_ref[_key_rows(kw_ref, j), :], qs_t, preferred_element_type=F32)

    bias = lambda table, l0: tb_ref[table, :, l0:l0 + LANE]
    sel = _Chain(scratch[0:2], scratch[2:4], scratch[4:6], scratch[6], scratch[7], sel_scores,
                 lambda j: jnp.concatenate([vs_ref[j], ones], axis=0), bias)
    win = _Chain(scratch[8:10], scratch[10:12], scratch[12:14], scratch[14], scratch[15], win_scores,
                 lambda j: jnp.concatenate([vw_ref[j], ones], axis=0), bias)

    def heads_out(acc_ref):
        o_t = acc_ref[0, 0:NSA_HD, :] / acc_ref[0, NSA_HD:NSA_HD + 1, :]
        pairs = [jnp.concatenate([o_t[:, (2 * k) * TQS:(2 * k + 1) * TQS],
                                  o_t[:, (2 * k + 1) * TQS:(2 * k + 2) * TQS]], axis=0).T
                 for k in range(NSA_HPG // 2)]
        return jnp.concatenate(pairs, axis=1)

    n_far = jnp.maximum(m - 1, 0)
    j1 = jnp.maximum(m - 1, 0)
    j2 = jnp.maximum(m - 2, 0)
    pen1 = jnp.where(m >= 1, 0.0, NEG)
    pen2 = jnp.where(m >= 2, 0.0, NEG)
    sel.prime(jnp.where(n_far > 0, 0, m))
    win.prime(m)

    def pair(i, carry):
        a = 2 * i
        sel.step(jnp.maximum(a - 1, 0), jnp.minimum(a + 1, n_far - 1))
        sel.step(a, jnp.where(a + 2 < n_far, a + 2, m), valid=a + 1 < n_far)
        return carry

    lax.fori_loop(0, lax.shift_right_logical(n_far + 1, 1), pair, 0)
    sel.step(jnp.maximum(n_far - 1, 0), j1, table=0, prev_valid=(n_far & 1) == 0)
    win.step(m, j1, table=0)
    sel.step(m, None, table=1, pen=pen1)
    win.step(m, j2, table=1, pen=pen1)
    sel.flush(j1)
    win.step(j1, None, table=2, pen=pen2)
    win.flush(j2)
    os_ref[...] = heads_out(sel.acc)
    ow_ref[...] = heads_out(win.acc)


def _nsa_sw_prompt(qb, ksel, kwin, selvt, winvt, sel, gd_b):
    b, s, _ = qb.shape
    assert TQS == TKI and WINDOW == 2 * TKI
    cols = NSA_HPG * TQS
    i = np.arange(TQS)[:, None]
    j = np.arange(TKI)[None, :]
    edge = jnp.asarray(np.broadcast_to(np.where(j >= i, 0.0, NEG).astype(np.float32), (NSA_HEADS, TQS, TKI)))
    tb = jnp.stack([_toeplitz(gd_b, 0, TQS, TKI, NEG), _toeplitz(gd_b, TKI, TQS, TKI), edge])
    tb = tb.reshape(3, NSA_KV, NSA_HPG, TQS, TKI).transpose(1, 0, 4, 2, 3).reshape(NSA_KV, 3, TKI, cols) * LOG2E
    nk = s // TKI
    e = np.zeros((nk, TKI, SEL_LANES), np.float32)
    for kj in range(nk):
        for k in range(TKI):
            e[kj, k, kj * (TKI // SEL_BLOCK) + k // SEL_BLOCK] = 1.0
    e = jnp.asarray(e).astype(BF16)
    acc_rows = NSA_HD + ONES_ROWS
    return pl.pallas_call(
        _nsa_sw_kernel,
        grid=(b, NSA_KV, s // TQS),
        in_specs=[pl.BlockSpec((None, TQS, 256), lambda bi, g, qi: (bi, qi, g)),
                  pl.BlockSpec((None, s, LANE), lambda bi, g, qi: (bi, 0, g)),
                  pl.BlockSpec((None, s, LANE), lambda bi, g, qi: (bi, 0, g)),
                  pl.BlockSpec((None, nk, NSA_HD, TKI), lambda bi, g, qi: (bi, 0, g, 0)),
                  pl.BlockSpec((None, nk, NSA_HD, TKI), lambda bi, g, qi: (bi, 0, g, 0)),
                  pl.BlockSpec((None, None, TQS, SEL_LANES), lambda bi, g, qi: (bi, g, qi, 0)),
                  pl.BlockSpec(e.shape, lambda bi, g, qi: (0, 0, 0)),
                  pl.BlockSpec((None, 3, TKI, cols), lambda bi, g, qi: (g, 0, 0, 0))],
        out_specs=[pl.BlockSpec((None, TQS, 256), lambda bi, g, qi: (bi, qi, g)),
                   pl.BlockSpec((None, TQS, 256), lambda bi, g, qi: (bi, qi, g))],
        out_shape=[jax.ShapeDtypeStruct((b, s, 512), F32), jax.ShapeDtypeStruct((b, s, 512), F32)],
        scratch_shapes=_chain_scratch(1, acc_rows, cols) + _chain_scratch(1, acc_rows, cols),
        compiler_params=_cparams(("arbitrary", "arbitrary", "arbitrary")),
        name="nsa_sel_win",
    )(qb, ksel, kwin, selvt, winvt, sel, e, tb)


def _layer_norm(x, g, b):
    mu = jnp.mean(x, axis=-1, keepdims=True)
    xc = x - mu
    var = jnp.mean(xc * xc, axis=-1, keepdims=True)
    return xc * lax.rsqrt(var + LN_EPS) * g + b


def _gate_expand():
    e = np.zeros((3, LANE, NSA_HEADS * NSA_HD), np.float32)
    for h in range(NSA_HEADS):
        for j in range(3):
            e[j, h * 3 + j, h * NSA_HD:(h + 1) * NSA_HD] = 1.0
    return jnp.asarray(e).astype(BF16)


def _tail1_kernel(oa_ref, oc_ref, os_ref, ow_ref, gate_ref, ma_ref, mb_ref, x_ref, ada_ref,
                  wa_ref, wb_ref, wo_ref, sub_ref, eg_ref, g1_ref, b1_ref, o_ref):
    oa = oa_ref[...]
    parts = []
    for h in range(DA_HEADS):
        of = oa[:, h * DA_VD:(h + 1) * DA_VD]
        rr = lax.rsqrt(jnp.mean(of * of, axis=-1, keepdims=True) + RMS_EPS)
        parts.append(of * rr * sub_ref[...] * (1.0 - LAM_INIT))
    oan = jnp.concatenate(parts, axis=1)
    sg = jax.nn.sigmoid(gate_ref[...])
    ob = (_mm3(sg, eg_ref[0]) * oc_ref[...] + _mm3(sg, eg_ref[1]) * os_ref[...]
          + _mm3(sg, eg_ref[2]) * ow_ref[...])
    y = (jax.nn.sigmoid(ma_ref[...]) * _mm(oan, wa_ref[...])
         + jax.nn.sigmoid(mb_ref[...]) * _mm(ob, wb_ref[...]))
    z = ALPHA * x_ref[...] + ada_ref[2] * _mm(y, wo_ref[...])
    o_ref[...] = _layer_norm(z, g1_ref[...], b1_ref[...])


def _tail1(oa, oc, os_, ow, gate, ma, mb, x2d, ada, wa, wb, wo, sub, eg, g1, b1, tm, tiles_per_group):
    m = x2d.shape[0]
    row = lambda w: pl.BlockSpec((tm, w), lambda i: (i, 0))
    full = lambda a: pl.BlockSpec(a.shape, lambda i: (0,) * a.ndim)
    return pl.pallas_call(
        _tail1_kernel,
        grid=(m // tm,),
        in_specs=[row(512), row(512), row(512), row(512), row(128), row(1024), row(1024), row(1024),
                  _ada_spec(ada, tm, tiles_per_group),
                  full(wa), full(wb), full(wo), full(sub), full(eg), full(g1), full(b1)],
        out_specs=row(1024),
        out_shape=jax.ShapeDtypeStruct((m, D_MODEL), F32),
        compiler_params=_cparams(("arbitrary",)),
        name="tail_merge",
    )(oa, oc, os_, ow, gate, ma, mb, x2d, ada, wa, wb, wo, sub, eg, g1, b1)


def _tail2_kernel(x_ref, ada_ref, wu_ref, wd_ref, g2_ref, b2_ref, o_ref, h_scr, acc):
    f = pl.program_id(1)

    @pl.when(f == 0)
    def _():
        h_scr[...] = (x_ref[...] * (1.0 + ada_ref[4]) + ada_ref[3]).astype(BF16)
        acc[...] = jnp.zeros(acc.shape, F32)

    u = jnp.maximum(jnp.dot(h_scr[...], wu_ref[...], preferred_element_type=F32), 0.0)
    acc[...] += _mm(u * u, wd_ref[...])

    @pl.when(f == pl.num_programs(1) - 1)
    def _():
        z = ALPHA * x_ref[...] + ada_ref[5] * acc[...]
        o_ref[...] = _layer_norm(z, g2_ref[...], b2_ref[...])


def _tail2(x1, ada, wu, wd, g2, b2, tm, tiles_per_group):
    m = x1.shape[0]
    tf = 1024
    return pl.pallas_call(
        _tail2_kernel,
        grid=(m // tm, D_FF // tf),
        in_specs=[pl.BlockSpec((tm, D_MODEL), lambda i, f: (i, 0)),
                  _ada_spec(ada, tm, tiles_per_group),
                  pl.BlockSpec((D_MODEL, tf), lambda i, f: (0, f)),
                  pl.BlockSpec((tf, D_MODEL), lambda i, f: (f, 0)),
                  pl.BlockSpec((1, D_MODEL), lambda i, f: (0, 0)),
                  pl.BlockSpec((1, D_MODEL), lambda i, f: (0, 0))],
        out_specs=pl.BlockSpec((tm, D_MODEL), lambda i, f: (i, 0)),
        out_shape=jax.ShapeDtypeStruct((m, D_MODEL), F32),
        scratch_shapes=[pltpu.VMEM((tm, D_MODEL), BF16), pltpu.VMEM((tm, D_MODEL), F32)],
        compiler_params=_cparams(("arbitrary", "arbitrary")),
        name="tail_mlp",
    )(x1, ada, wu, wd, g2, b2)


def _da_decode_kernel(pt_ref, *refs):
    pages = refs[:PP]
    q_ref, kn_ref, vn_ref, bl_ref, b0_ref, lam_ref, o_ref, m_ref, l_ref, a_ref = refs[PP:]
    j = pl.program_id(1)
    last = j == pl.num_programs(1) - 1
    scale = DA_HD ** -0.5
    rows_pg = PAGE * 2 * DA_HEADS

    @pl.when(j == 0)
    def _():
        m_ref[...] = jnp.full(m_ref.shape, NEG, F32)
        l_ref[...] = jnp.zeros(l_ref.shape, F32)
        a_ref[...] = jnp.zeros(a_ref.shape, F32)

    q = q_ref[...]
    row = lax.broadcasted_iota(jnp.int32, (8, rows_pg), 0)
    col = lax.broadcasted_iota(jnp.int32, (8, rows_pg), 1)
    cmask = jnp.where((col & 7) == lax.shift_right_logical(row, 1), 0.0, NEG)
    scores = []
    xs = []
    for k in range(PP):
        x = pages[k][...].astype(BF16)
        sc = _mm_nt(q, x) * scale + cmask
        if k == PP - 1:
            sc = sc + jnp.where(last, bl_ref[...], 0.0)
        scores.append(sc)
        xs.append(x)
    s = jnp.concatenate(scores, axis=1)
    m_old = m_ref[...]
    m_new = jnp.maximum(m_old, jnp.max(s, axis=-1, keepdims=True))
    p = jnp.exp(s - m_new)
    alpha = jnp.exp(m_old - m_new)
    l_ref[...] = alpha * l_ref[...] + jnp.sum(p, axis=-1, keepdims=True)
    acc = alpha * a_ref[...]
    for k in range(PP):
        pv = pltpu.roll(p[:, k * rows_pg:(k + 1) * rows_pg], DA_HEADS, 1)
        acc = acc + _mm(pv, xs[k])
    a_ref[...] = acc
    m_ref[...] = m_new

    @pl.when(last)
    def _():
        s_new = jnp.sum(q.astype(F32) * kn_ref[...], axis=-1, keepdims=True) * scale + b0_ref[:, 0:1]
        m_o = m_ref[...]
        m_n = jnp.maximum(m_o, s_new)
        p_new = jnp.exp(s_new - m_n)
        al = jnp.exp(m_o - m_n)
        raw = (al * a_ref[...] + p_new * vn_ref[...]) / (al * l_ref[...] + p_new)
        o_ref[...] = raw - _diff_lambda(lam_ref) * pltpu.roll(raw, 7, 0)


def _da_decode(page_table, cache_rows, q8, k_new, v_new, bl, b0, da_lambda):
    b, n_pages = page_table.shape
    rows_pg = PAGE * 2 * DA_HEADS
    page_spec = lambda k: pl.BlockSpec((rows_pg, LANE), lambda bi, j, pt: (pt[bi, j * PP + k], 0))
    per_row = lambda: pl.BlockSpec((None, 8, LANE), lambda bi, j, pt: (bi, 0, 0))
    grid_spec = pltpu.PrefetchScalarGridSpec(
        num_scalar_prefetch=1,
        grid=(b, n_pages // PP),
        in_specs=[page_spec(k) for k in range(PP)] + [
            per_row(), per_row(), per_row(),
            pl.BlockSpec((8, rows_pg), lambda bi, j, pt: (0, 0)),
            pl.BlockSpec((8, LANE), lambda bi, j, pt: (0, 0)),
            pl.BlockSpec((4, DA_HD), lambda bi, j, pt: (0, 0))],
        out_specs=per_row(),
        scratch_shapes=[pltpu.VMEM((8, 1), F32), pltpu.VMEM((8, 1), F32), pltpu.VMEM((8, LANE), F32)])
    return pl.pallas_call(
        _da_decode_kernel,
        grid_spec=grid_spec,
        out_shape=jax.ShapeDtypeStruct((b, 8, LANE), F32),
        compiler_params=_cparams(("arbitrary", "arbitrary")),
        name="da_decode",
    )(page_table, *([cache_rows] * PP), q8, k_new, v_new, bl, b0, da_lambda)


def _nsa_decode1_kernel(ab_ref, q_ref, swa_ref, new_ref, bw_ref, pe_ref, w1c_ref, w2_ref, ovl_ref,
                        oc_ref, ow_ref, idx_ref):
    n_chunks = ab_ref.shape[0]
    kcvc = _cmp_stage2(ab_ref[...], pe_ref, w1c_ref, w2_ref)
    scale = NSA_HD ** -0.5
    swa = swa_ref[...].astype(BF16)
    n_win = swa.shape[0]
    score_rows = []
    for g in range(NSA_KV):
        q = q_ref[g]
        s = _mm_nt(q, kcvc[:, g * 256:g * 256 + 128]) * scale
        n = lax.broadcasted_iota(jnp.int32, s.shape, 1)
        p = _masked_softmax(s, n < n_chunks - 1)
        oc_ref[g] = _mm(p, kcvc[:, g * 256 + 128:g * 256 + 256])
        psum = jnp.sum(p[0:NSA_HPG], axis=0, keepdims=True)
        imp = _mm3(jnp.broadcast_to(psum, (8, n_chunks)), ovl_ref[...])
        blk = lax.broadcasted_iota(jnp.int32, imp.shape, 1)
        n_blk = n_chunks * CMP_STRIDE // SEL_BLOCK
        forced = (blk == 0) | (blk == n_blk - 1)
        score_rows.append(jnp.where(blk < n_blk, jnp.where(forced, SEL_BIG, imp), REMOVED)[0:1, :])
        new = new_ref[g:g + 1, :]
        sw = _mm_nt(q, swa) * scale + bw_ref[g][:, 0:n_win]
        s_new = (jnp.sum(q.astype(F32) * new, axis=-1, keepdims=True) * scale
                 + bw_ref[g][:, n_win:n_win + 1])
        m = jnp.maximum(jnp.max(sw, axis=-1, keepdims=True), s_new)
        e = jnp.exp(sw - m)
        e_new = jnp.exp(s_new - m)
        den = jnp.sum(e, axis=-1, keepdims=True) + e_new
        ow_ref[g] = (_mm(e, swa) + e_new * new) / den
    rest = jnp.full((SEL_LANES - NSA_KV, SEL_LANES), REMOVED, F32)
    picks_t = _topk_picks_cols(jnp.concatenate(score_rows + [rest], axis=0).T, N_SEL - 1)
    picks = picks_t.T
    for g in range(NSA_KV):
        idx_ref[g] = jnp.broadcast_to(picks[g:g + 1, :], (8, SEL_LANES)).astype(jnp.int32)


def _nsa_decode1(ab, qc, cache_swa, kvw_new, bw, pe8, w1c, w2, ovl):
    b, n_chunks, _ = ab.shape
    full = lambda a: pl.BlockSpec(a.shape, lambda bi: (0,) * a.ndim)
    out4 = lambda: pl.BlockSpec((None, NSA_KV, 8, LANE), lambda bi: (bi, 0, 0, 0))
    return pl.pallas_call(
        _nsa_decode1_kernel,
        grid=(b,),
        in_specs=[pl.BlockSpec((None, n_chunks, 512), lambda bi: (bi, 0, 0)),
                  pl.BlockSpec((None, NSA_KV, 8, LANE), lambda bi: (bi, 0, 0, 0)),
                  pl.BlockSpec((None, cache_swa.shape[1], LANE), lambda bi: (bi, 0, 0)),
                  pl.BlockSpec((None, NSA_KV, LANE), lambda bi: (bi, 0, 0)),
                  full(bw), full(pe8), full(w1c), full(w2), full(ovl)],
        out_specs=[out4(), out4(), out4()],
        out_shape=[jax.ShapeDtypeStruct((b, NSA_KV, 8, LANE), F32),
                   jax.ShapeDtypeStruct((b, NSA_KV, 8, LANE), F32),
                   jax.ShapeDtypeStruct((b, NSA_KV, 8, LANE), jnp.int32)],
        compiler_params=_cparams(("arbitrary",)),
        name="nsa_decode_cmp_win",
    )(ab, qc, cache_swa, kvw_new, bw, pe8, w1c, w2, ovl)


def _nsa_decode2_kernel(pt_ref, idx_ref, *refs, n_blk, n_pick):
    blks = refs[:n_pick]
    q_ref, new_ref, bs_ref, o_ref = refs[n_pick:]
    b = pl.program_id(0)
    g = pl.program_id(1)
    scale = NSA_HD ** -0.5
    q = q_ref[...]
    cols = blks[0].shape[0]
    scores = []
    xs = []
    for k in range(n_pick):
        x = blks[k][...].astype(BF16)
        blk = idx_ref[b, g, k]
        bias = (jnp.where(blk == n_blk - 1, bs_ref[:, cols:2 * cols], 0.0)
                + jnp.where(blk == n_blk - 2, bs_ref[:, 2 * cols:3 * cols], 0.0))
        scores.append(_mm_nt(q, x) * scale + bs_ref[:, 0:cols] + bias)
        xs.append(x)
    new = new_ref[...]
    s_new = jnp.sum(q.astype(F32) * new, axis=-1, keepdims=True) * scale + bs_ref[:, 3 * cols:3 * cols + 1]
    s = jnp.concatenate(scores, axis=1)
    m = jnp.maximum(jnp.max(s, axis=-1, keepdims=True), s_new)
    p = jnp.exp(s - m)
    p_new = jnp.exp(s_new - m)
    acc = p_new * new
    for k in range(n_pick):
        acc = acc + _mm(p[:, k * cols:(k + 1) * cols], xs[k])
    o_ref[...] = acc / (jnp.sum(p, axis=-1, keepdims=True) + p_new)


def _nsa_decode2(page_table, idx, cache_rows, qc, nsa_new, bs):
    b = page_table.shape[0]
    n_pick = idx.shape[2]
    rows_blk = SEL_BLOCK * 2 * NSA_KV

    def blk_spec(k):
        def blk_map(bi, g, pt, ix):
            blk = ix[bi, g, k]
            return (pt[bi, lax.shift_right_logical(blk, 1)] * 2 + (blk & 1), 0)
        return pl.BlockSpec((rows_blk, LANE), blk_map)

    grid_spec = pltpu.PrefetchScalarGridSpec(
        num_scalar_prefetch=2,
        grid=(b, NSA_KV),
        in_specs=[blk_spec(k) for k in range(n_pick)] + [
            pl.BlockSpec((None, None, 8, LANE), lambda bi, g, pt, ix: (bi, g, 0, 0)),
            pl.BlockSpec((None, None, 1, LANE), lambda bi, g, pt, ix: (bi, g, 0, 0)),
            pl.BlockSpec((None, 8, bs.shape[2]), lambda bi, g, pt, ix: (g, 0, 0))],
        out_specs=pl.BlockSpec((None, None, 8, LANE), lambda bi, g, pt, ix: (bi, g, 0, 0)))
    return pl.pallas_call(
        functools.partial(_nsa_decode2_kernel, n_blk=page_table.shape[1] * PAGE // SEL_BLOCK, n_pick=n_pick),
        grid_spec=grid_spec,
        out_shape=jax.ShapeDtypeStruct((b, NSA_KV, 8, LANE), F32),
        compiler_params=_cparams(("arbitrary", "arbitrary")),
        name="nsa_decode_sel",
    )(page_table, idx, *([cache_rows] * n_pick), qc, nsa_new, bs)


def _prompt_mixers(pr, b, s, gd_a, gd_b, da_lambda, w1, w2, pe8, w1c):
    sh = lambda a: a.reshape(b, s, a.shape[-1])
    nk = s // TKI
    vt = lambda a: a.reshape(b, nk, a.shape[1], TKI)
    o_a = _da_prompt(sh(pr["qa2"]), sh(pr["dak"]), vt(pr["davt"]), gd_a, da_lambda)
    n_chunks = s // CMP_STRIDE
    ab = _cmp_stage1_rows(pr["nsa"], w1)
    kcvc = _cmp_stage2_prompt(ab.reshape(b, n_chunks, 512), pe8, w1c, w2)
    o_c, sel = _nsa_cmp_prompt(sh(pr["qb"]), kcvc, _ovl_t(n_chunks))
    o_s, o_w = _nsa_sw_prompt(sh(pr["qb2"]), sh(pr["ksel"]), sh(pr["kwin"]), vt(pr["selvt"]), vt(pr["winvt"]),
                              sel, gd_b)
    flat = lambda a: a.reshape(b * s, a.shape[-1])
    return flat(o_a), flat(o_c), flat(o_s), flat(o_w)


def _sample_mixers(pr, page_table, cache_da, cache_nsa, cache_swa, gd_a, gd_b, da_lambda, w1, w2, pe8, w1c):
    b, n_pages = page_table.shape
    past = n_pages * PAGE
    n_pool = cache_da.shape[0]
    da_rows = cache_da.reshape(n_pool, PAGE, DA_HEADS, 2, LANE).transpose(0, 1, 3, 2, 4).reshape(-1, LANE)
    nsa_rows = cache_nsa.reshape(n_pool, PAGE, NSA_KV, 2, LANE).transpose(0, 1, 3, 2, 4).reshape(-1, LANE)
    qa = pr["qa"].reshape(b, DA_HEADS, 2, DA_HD)
    q8 = jnp.zeros((b, DA_HEADS, 2, 2, DA_HD), BF16)
    for c in range(2):
        q8 = q8.at[:, :, c, c, :].set(qa[:, :, c])
    q8 = q8.reshape(b, 8, LANE)
    da_new = pr["da"].reshape(b, DA_HEADS, 2, LANE)
    k_new = jnp.repeat(da_new[:, :, 0], 2, axis=1)
    v_new = jnp.repeat(da_new[:, :, 1], 2, axis=1)
    gda8 = jnp.repeat(gd_a, 2, axis=0)
    bl = jnp.repeat(gda8[:, PAGE - jnp.arange(PAGE)], 2 * DA_HEADS, axis=1)
    b0 = jnp.broadcast_to(gda8[:, 0:1], (8, LANE))
    o_a = _da_decode(page_table, da_rows, q8, k_new, v_new, bl, b0, da_lambda)
    o_a = o_a[:, 0::2, :].reshape(b, 512)
    chunks = PAGE // CMP_STRIDE
    ab = _cmp_stage1_pages(page_table, nsa_rows, w1).reshape(b, n_pages * chunks, 512)
    qb = pr["qb"].reshape(b, NSA_KV, NSA_HPG, NSA_HD)
    qc = jnp.zeros((b, NSA_KV, 8, LANE), BF16).at[:, :, :NSA_HPG, :NSA_HD].set(qb)
    gdb = jnp.pad(gd_b.reshape(NSA_KV, NSA_HPG, -1), ((0, 0), (0, 8 - NSA_HPG), (0, 0)))
    own = jnp.arange(NSA_KV)[:, None, None]
    n_win = cache_swa.shape[1]
    grp_w = jnp.arange(n_win * NSA_KV)[None, None, :] % NSA_KV
    bw = jnp.where(grp_w == own, jnp.repeat(gdb[:, :, n_win - jnp.arange(n_win)], NSA_KV, axis=2), NEG)
    bw = jnp.concatenate([bw, jnp.broadcast_to(gdb[:, :, 0:1], (NSA_KV, 8, LANE))], axis=2)
    o_c, o_w, idx = _nsa_decode1(ab, qc, cache_swa.reshape(b, n_win * NSA_KV, LANE),
                                 pr["kvw"].reshape(b, NSA_KV, LANE), bw, pe8, w1c, w2,
                                 _ovl_t(past // CMP_STRIDE))
    idx = idx[:, :, 0, :N_SEL - 1]
    rows_blk = SEL_BLOCK * 2 * NSA_KV
    kind = jnp.arange(rows_blk)[None, None, :] % (2 * NSA_KV)
    keep = kind == NSA_KV + own
    bs = jnp.concatenate([
        jnp.where(keep, 0.0, NEG) * jnp.ones((1, 8, 1), F32),
        jnp.repeat(gdb[:, :, SEL_BLOCK - jnp.arange(SEL_BLOCK)], 2 * NSA_KV, axis=2),
        jnp.repeat(gdb[:, :, 2 * SEL_BLOCK - jnp.arange(SEL_BLOCK)], 2 * NSA_KV, axis=2),
        jnp.broadcast_to(gdb[:, :, 0:1], (NSA_KV, 8, LANE))], axis=2)
    sel_new = pr["nsa"].reshape(b, NSA_KV, 2, LANE)[:, :, 1:2, :]
    o_s = _nsa_decode2(page_table, idx, nsa_rows, qc, sel_new, bs)
    o_c = o_c[:, :, :NSA_HPG, :NSA_HD].reshape(b, 512)
    o_w = o_w[:, :, :NSA_HPG, NSA_HD:].reshape(b, 512)
    o_s = o_s[:, :, :NSA_HPG, NSA_HD:].reshape(b, 512)
    return o_a, o_c, o_s, o_w


def kernel(x_prompt, x_sample, cache_da_kv, cache_nsa_kv, cache_swa_kv, page_table, c_prompt, c_sample, rel_bias, w_ada, b_ada, w_in, da_lambda, da_subln, cmp_pe, cmp_w1, cmp_w2, w_br_a, w_br_b, w_out, ln1_g, ln1_b, w_up, w_down, ln2_g, ln2_b):
    bp, s, _ = x_prompt.shape
    bs_ = x_sample.shape[0]
    w_perm, w_vt = _perm_w_in(w_in[0])
    w1, w2 = _cmp_weights(cmp_w1[0], cmp_w2[0])
    pe8 = jnp.broadcast_to(cmp_pe[0].reshape(2, 1, CMP_LEN * NSA_HD), (2, 8, CMP_LEN * NSA_HD))
    w1c = cmp_w1[0].reshape(2, CMP_LEN * NSA_HD, CMP_HID)
    gd_a = _dist_bias(rel_bias[:, :DA_HEADS], 1024)
    gd_b = _dist_bias(rel_bias[:, DA_HEADS:], 1024)
    wa, wb, wo = w_br_a[0].astype(BF16), w_br_b[0].astype(BF16), w_out[0].astype(BF16)
    wu, wd = w_up[0].astype(BF16), w_down[0].astype(BF16)
    sub = da_subln[0].reshape(1, DA_VD)
    eg = _gate_expand()
    g1, b1 = ln1_g[0].reshape(1, D_MODEL), ln1_b[0].reshape(1, D_MODEL)
    g2, b2 = ln2_g[0].reshape(1, D_MODEL), ln2_b[0].reshape(1, D_MODEL)
    lam = da_lambda[0]

    n_c = bp + bs_
    c_all = jnp.pad(jnp.concatenate([c_prompt, c_sample], 0), ((0, (-n_c) % 8), (0, 0)))
    ada = _ada(c_all, w_ada[0], b_ada[0])[:n_c].reshape(n_c, 6, D_MODEL)
    ada_p = jnp.transpose(ada[:bp], (1, 0, 2)).reshape(6, bp, 1, D_MODEL)
    ada_s = jnp.transpose(ada[bp:], (1, 0, 2)).reshape(6, 1, bs_, D_MODEL)

    def tail(mix, pr, x2d, ada_x, tm, tpg):
        o_a, o_c, o_s, o_w = mix
        x1 = _tail1(o_a, o_c, o_s, o_w, pr["gate"], pr["ma"], pr["mb"], x2d, ada_x,
                    wa, wb, wo, sub, eg, g1, b1, tm, tpg)
        tm2 = min(2 * tm, tm * tpg)
        return _tail2(x1, ada_x, wu, wd, g2, b2, tm2, tm * tpg // tm2)

    xp = x_prompt.reshape(bp * s, D_MODEL)
    pr_p = _proj(xp, ada_p, w_perm, w_vt, TKI, s // TKI)
    mix_p = _prompt_mixers(pr_p, bp, s, gd_a, gd_b, lam, w1, w2, pe8, w1c)
    tm_t = 512
    y_p = tail(mix_p, pr_p, xp, ada_p, tm_t, s // tm_t).reshape(bp, s, D_MODEL)
    xs = x_sample.reshape(bs_, D_MODEL)
    pr_s = _proj(xs, ada_s, w_perm, None, bs_, 1)
    mix_s = _sample_mixers(pr_s, page_table, cache_da_kv[0], cache_nsa_kv[0], cache_swa_kv[0],
                           gd_a, gd_b, lam, w1, w2, pe8, w1c)
    y_s = tail(mix_s, pr_s, xs, ada_s, bs_, 1).reshape(bs_, 1, D_MODEL)

    win = min(WINDOW, s)
    from_rows = lambda a, units: a.reshape(bp, s, 2, units, LANE).transpose(0, 1, 3, 2, 4).reshape(1, bp, s, units, 256)
    new_da_p = from_rows(pr_p["da"], DA_HEADS)
    new_nsa_p = from_rows(pr_p["nsa"], NSA_KV)
    new_swa_p = pr_p["kvw"].reshape(bp, s, NSA_KV, 2 * NSA_HD)[None, :, s - win:]
    new_da_s = pr_s["da"].reshape(1, bs_, 1, DA_HEADS, 4 * DA_HD)
    new_nsa_s = pr_s["nsa"].reshape(1, bs_, 1, NSA_KV, 4 * NSA_HD)
    new_swa_s = jnp.concatenate([cache_swa_kv[0][:, 1:], pr_s["kvw"].reshape(bs_, 1, NSA_KV, 2 * NSA_HD)],
                                axis=1)[None]
    return (y_p, y_s, new_da_p, new_nsa_p, new_swa_p, new_da_s, new_nsa_s, new_swa_s)
```
